```python
import math, functools
import jax, jax.numpy as jnp
from jax import lax
import numpy as np

D_MODEL = 1024
BATCH = 8
SEQ = 2048
DEPTH = 2
DEC_BATCH = 128
DEC_SEQ = 8
PAST_LEN = 2048
PAGE_SIZE = 128

D_MIX = D_MODEL
C_POOL = D_MIX // 2
POOL_WINDOWS = (2, 4, 8, 16)
POOL_GC = C_POOL // len(POOL_WINDOWS)
POOL_HIST = max(POOL_WINDOWS) - 1
D_ATT = D_MIX - C_POOL
HEAD_DIM = 64
N_HEADS = D_ATT // HEAD_DIM
IDX_HEADS = 8
IDX_DIM = 64
TOPK_MAX = 256
N_BUCKETS = 32
MAX_DISTANCE = 128
D_FF = 4 * D_MODEL
Q_BLOCK = 128
EPS = 1e-6
OFF_Q = C_POOL
OFF_K = OFF_Q + D_ATT
OFF_V = OFF_K + D_ATT
OFF_QI = OFF_V + D_ATT
OFF_KI = OFF_QI + IDX_HEADS * IDX_DIM
OFF_WI = OFF_KI + IDX_DIM
D_IN = OFF_WI + IDX_HEADS

kernel_name = "hybrid_pool_dsa_decoder_step"


def rms_norm(x, g):
    x32 = x.astype(jnp.float32)
    y = x32 * lax.rsqrt(jnp.mean(x32 * x32, axis=-1, keepdims=True) + EPS)
    return (y * g.astype(jnp.float32)).astype(x.dtype)


def t5_bucket(dist):
    n = jnp.maximum(dist, 0)
    max_exact = N_BUCKETS // 2
    large = max_exact + (jnp.log(jnp.maximum(n, 1).astype(jnp.float32) / max_exact)
                         / math.log(MAX_DISTANCE / max_exact)
                         * (N_BUCKETS - max_exact)).astype(jnp.int32)
    large = jnp.minimum(large, N_BUCKETS - 1)
    return jnp.where(n < max_exact, n, large)


def split_proj(z):
    B, T, _ = z.shape
    u = z[..., :OFF_Q]
    q = z[..., OFF_Q:OFF_K].reshape(B, T, N_HEADS, HEAD_DIM)
    k = z[..., OFF_K:OFF_V].reshape(B, T, N_HEADS, HEAD_DIM)
    v = z[..., OFF_V:OFF_QI].reshape(B, T, N_HEADS, HEAD_DIM)
    qi = z[..., OFF_QI:OFF_KI].reshape(B, T, IDX_HEADS, IDX_DIM)
    ki = z[..., OFF_KI:OFF_WI]
    wi = z[..., OFF_WI:]
    return u, q, k, v, qi, ki, wi


def pool_mix(u_ext, n_hist, pos0, pool_w_l, pool_scale_l):
    B, E, C = u_ext.shape
    u32 = u_ext.astype(jnp.float32)
    cs = jnp.concatenate([jnp.zeros((B, 1, C), jnp.float32), jnp.cumsum(u32, axis=1)], axis=1)
    row = jnp.arange(n_hist, E)
    pos = pos0 + row
    outs = []
    for g, w in enumerate(POOL_WINDOWS):
        sl = slice(g * POOL_GC, (g + 1) * POOL_GC)
        win_sum = cs[:, row + 1, sl] - cs[:, jnp.maximum(row + 1 - w, 0), sl]
        cnt = jnp.minimum(pos + 1, w).astype(jnp.float32)[None, :, None]
        pooled = win_sum / cnt - u32[:, n_hist:, sl]
        outs.append(jnp.einsum('btc,cd->btd', pooled.astype(u_ext.dtype), pool_w_l[g]))
    return jnp.concatenate(outs, axis=-1) * pool_scale_l


def sparse_attention(q, qi, wi, qpos, k_all, v_all, ki_all, n_sel, rel_bias):
    B, Tq = q.shape[:2]
    L = k_all.shape[1]
    kpos = jnp.arange(L, dtype=jnp.int32)
    causal = kpos[None, :] <= qpos[:, None]
    dots = jnp.einsum('bthd,bsd->bths', qi.astype(jnp.float32), ki_all.astype(jnp.float32)) * IDX_DIM ** -0.5
    score = jnp.einsum('bth,bths->bts', wi.astype(jnp.float32) * IDX_HEADS ** -0.5, jax.nn.relu(dots))
    score = jnp.where(causal[None], score, -jnp.inf)
    _, sel = lax.top_k(score, n_sel)
    gather = jax.vmap(lambda rows, idx: rows[idx])
    k_sel = gather(k_all, sel)
    v_sel = gather(v_all, sel)
    logits = jnp.einsum('bthd,btnhd->bthn', q, k_sel).astype(jnp.float32) * HEAD_DIM ** -0.5
    dist = qpos[None, :, None] - sel
    bias = rel_bias[t5_bucket(dist)].astype(jnp.float32)
    logits = logits + jnp.transpose(bias, (0, 1, 3, 2))
    valid = (dist >= 0)[:, :, None, :]
    logits = jnp.where(valid, logits, -jnp.inf)
    p = jax.nn.softmax(logits, axis=-1)
    out = jnp.einsum('bthn,btnhd->bthd', p.astype(v_sel.dtype), v_sel)
    return out.reshape(B, Tq, N_HEADS * HEAD_DIM)


def prompt_mixer(z, pool_w_l, pool_scale_l, rel_bias, n_sel):
    u, q, k, v, qi, ki, wi = split_proj(z)
    B, T, _ = z.shape
    y_pool = pool_mix(u, 0, 0, pool_w_l, pool_scale_l)
    nb = T // Q_BLOCK
    blk = lambda a: jnp.moveaxis(a.reshape((B, nb, Q_BLOCK) + a.shape[2:]), 1, 0)
    qpos = jnp.arange(T, dtype=jnp.int32).reshape(nb, Q_BLOCK)
    y_att = lax.map(lambda a: sparse_attention(a[0], a[1], a[2], a[3], k, v, ki, n_sel, rel_bias),
                    (blk(q), blk(qi), blk(wi), qpos))
    y_att = jnp.moveaxis(y_att, 0, 1).reshape(B, T, D_ATT)
    return jnp.concatenate([y_pool, y_att], axis=-1), (k, v, ki, u[:, T - POOL_HIST:])


def sample_mixer(z, layer, cache_k, cache_v, cache_kidx, state_pool, page_table, pool_w_l, pool_scale_l, rel_bias, n_sel):
    u, q, k, v, qi, ki, wi = split_proj(z)
    B, T, _ = z.shape
    past = lambda pool: pool[layer, page_table].reshape((B, -1) + pool.shape[3:]).astype(z.dtype)
    k_all = jnp.concatenate([past(cache_k), k], axis=1)
    v_all = jnp.concatenate([past(cache_v), v], axis=1)
    ki_all = jnp.concatenate([past(cache_kidx), ki], axis=1)
    qpos = PAST_LEN + jnp.arange(T, dtype=jnp.int32)
    y_att = sparse_attention(q, qi, wi, qpos, k_all, v_all, ki_all, n_sel, rel_bias)
    u_ext = jnp.concatenate([state_pool[layer].astype(u.dtype), u], axis=1)
    y_pool = pool_mix(u_ext, POOL_HIST, PAST_LEN - POOL_HIST, pool_w_l, pool_scale_l)
    return jnp.concatenate([y_pool, y_att], axis=-1), (k, v, ki, u_ext[:, -POOL_HIST:])


def decoder_layer(x, c, mix_fn, ada_w_l, ada_b_l, ln1_pre_l, ln1_post_l, ln2_pre_l, ln2_post_l,
                  w_in_l, w_out_l, w_ff1_l, w_ff2_l):
    mod = (c @ ada_w_l + ada_b_l)[:, None, :]
    sh1, sc1, g1, sh2, sc2, g2 = jnp.split(mod, 6, axis=-1)
    h = rms_norm(x, ln1_pre_l) * (1 + sc1) + sh1
    mixed, new_state = mix_fn(h @ w_in_l)
    x = x + g1 * rms_norm(mixed @ w_out_l, ln1_post_l)
    h = rms_norm(x, ln2_pre_l) * (1 + sc2) + sh2
    f = jnp.square(jax.nn.relu(h @ w_ff1_l)) @ w_ff2_l
    x = x + g2 * rms_norm(f, ln2_post_l)
    return x, new_state


def setup_inputs(seed: int = 0) -> dict:
    key = jax.random.key(seed)
    ks = jax.random.split(key, 24)
    f32 = jnp.float32
    n_pages = PAST_LEN // PAGE_SIZE
    n_phys = (DEC_BATCH * n_pages * 5) // 4
    nrm = lambda k, shape, s: jax.random.normal(k, shape, f32) * s
    page_table = jax.random.permutation(ks[0], n_phys)[:DEC_BATCH * n_pages]
    page_table = page_table.reshape(DEC_BATCH, n_pages).astype(jnp.int32)
    return {
        "x_prompt": nrm(ks[1], (BATCH, SEQ, D_MODEL), 1.0),
        "x_sample": nrm(ks[2], (DEC_BATCH, DEC_SEQ, D_MODEL), 1.0),
        "cache_k": nrm(ks[3], (DEPTH, n_phys, PAGE_SIZE, N_HEADS, HEAD_DIM), 1.0),
        "cache_v": nrm(ks[4], (DEPTH, n_phys, PAGE_SIZE, N_HEADS, HEAD_DIM), 1.0),
        "cache_kidx": nrm(ks[5], (DEPTH, n_phys, PAGE_SIZE, IDX_DIM), 1.0),
        "state_pool": nrm(ks[6], (DEPTH, DEC_BATCH, POOL_HIST, C_POOL), 1.0),
        "page_table": page_table,
        "c_prompt": nrm(ks[7], (BATCH, D_MODEL), 1.0),
        "c_sample": nrm(ks[8], (DEC_BATCH, D_MODEL), 1.0),
        "rel_bias": nrm(ks[9], (N_BUCKETS, N_HEADS), 0.5),
        "ada_w": nrm(ks[10], (DEPTH, D_MODEL, 6 * D_MODEL), 0.3 * D_MODEL ** -0.5),
        "ada_b": nrm(ks[11], (DEPTH, 6 * D_MODEL), 0.02),
        "ln1_pre": 1.0 + nrm(ks[12], (DEPTH, D_MODEL), 0.05),
        "ln1_post": 1.0 + nrm(ks[13], (DEPTH, D_MODEL), 0.05),
        "ln2_pre": 1.0 + nrm(ks[14], (DEPTH, D_MODEL), 0.05),
        "ln2_post": 1.0 + nrm(ks[15], (DEPTH, D_MODEL), 0.05),
        "w_in": nrm(ks[16], (DEPTH, D_MODEL, D_IN), D_MODEL ** -0.5),
        "pool_w": nrm(ks[17], (DEPTH, len(POOL_WINDOWS), POOL_GC, POOL_GC), POOL_GC ** -0.5),
        "pool_scale": 1.0 + nrm(ks[18], (DEPTH, C_POOL), 0.1),
        "w_out": nrm(ks[19], (DEPTH, D_MIX, D_MODEL), D_MIX ** -0.5),
        "w_ff1": nrm(ks[20], (DEPTH, D_MODEL, D_FF), D_MODEL ** -0.5),
        "w_ff2": nrm(ks[21], (DEPTH, D_FF, D_MODEL), D_FF ** -0.5),
    }


def reference(x_prompt, x_sample, cache_k, cache_v, cache_kidx, state_pool, page_table, c_prompt, c_sample,
              rel_bias, ada_w, ada_b, ln1_pre, ln1_post, ln2_pre, ln2_post, w_in, pool_w, pool_scale,
              w_out, w_ff1, w_ff2):
    n_sel_prompt = min(TOPK_MAX, SEQ // 4)
    n_sel_sample = min(TOPK_MAX, (PAST_LEN + DEC_SEQ) // 4)
    xp, xs = x_prompt, x_sample
    outs_p = [[], [], [], []]
    outs_s = [[], [], [], []]
    for l in range(DEPTH):
        shared = (ada_w[l], ada_b[l], ln1_pre[l], ln1_post[l], ln2_pre[l], ln2_post[l],
                  w_in[l], w_out[l], w_ff1[l], w_ff2[l])
        mix_p = functools.partial(prompt_mixer, pool_w_l=pool_w[l], pool_scale_l=pool_scale[l],
                                  rel_bias=rel_bias, n_sel=n_sel_prompt)
        xp, st_p = decoder_layer(xp, c_prompt, mix_p, *shared)
        mix_s = functools.partial(sample_mixer, layer=l, cache_k=cache_k, cache_v=cache_v,
                                  cache_kidx=cache_kidx, state_pool=state_pool, page_table=page_table,
                                  pool_w_l=pool_w[l], pool_scale_l=pool_scale[l],
                                  rel_bias=rel_bias, n_sel=n_sel_sample)
        xs, st_s = decoder_layer(xs, c_sample, mix_s, *shared)
        for lst, a in zip(outs_p, st_p):
            lst.append(a)
        for lst, a in zip(outs_s, st_s):
            lst.append(a)
    new_k_p, new_v_p, new_ki_p, new_pool_p = [jnp.stack(a) for a in outs_p]
    new_k_s, new_v_s, new_ki_s, new_pool_s = [jnp.stack(a) for a in outs_s]
    return (xp, xs, new_k_p, new_v_p, new_ki_p, new_pool_p, new_k_s, new_v_s, new_ki_s, new_pool_s)
```

```python
import functools
import math
from typing import NamedTuple

import jax
import jax.numpy as jnp
from jax import lax
from jax.experimental import pallas as pl
from jax.experimental.pallas import tpu as pltpu

F32 = jnp.float32
BF16 = jnp.bfloat16
I32 = jnp.int32

D_MODEL = 1024
C_POOL = 512
D_ATT = 512
HEAD_DIM = 64
N_HEADS = 8
IDX_HEADS = 8
IDX_DIM = 64
POOL_WINDOWS = (2, 4, 8, 16)
POOL_GC = C_POOL // len(POOL_WINDOWS)
POOL_HIST = max(POOL_WINDOWS) - 1
TOPK_MAX = 256
N_BUCKETS = 32
MAX_DISTANCE = 128
D_FF = 4 * D_MODEL
EPS = 1e-6
PAGE_SIZE = 128
OFF_Q = C_POOL
OFF_K = OFF_Q + D_ATT
OFF_V = OFF_K + D_ATT
OFF_QI = OFF_V + D_ATT
OFF_KI = OFF_QI + IDX_HEADS * IDX_DIM
OFF_WI = OFF_KI + IDX_DIM
D_IN = OFF_WI + IDX_HEADS

LANES = 128
SUBLANES = 8
D_IN_PAD = ((D_IN + LANES - 1) // LANES) * LANES
TAIL = D_IN_PAD - OFF_KI
POOL_PAD = 16
VMEM_LIMIT = 56 * 1024 * 1024

NEG_INF = float("-inf")
INT_MIN = -(2 ** 31)


class Cfg(NamedTuple):
    batch: int
    seq: int
    dec_batch: int
    dec_seq: int
    past_len: int
    blk: int
    tm: int


def _rms(x, g):
    ms = jnp.mean(x * x, axis=-1, keepdims=True)
    return x * lax.rsqrt(ms + EPS) * g


def _nt_dot(a, b):
    return lax.dot_general(a, b, (((1,), (1,)), ((), ())), preferred_element_type=F32)


def _key_to_float(key):
    bits = key ^ ((key >> 31) & jnp.int32(0x7FFFFFFF))
    return lax.bitcast_convert_type(bits, F32)


def _lane_rep(col, width=LANES):
    return jnp.broadcast_to(col, (col.shape[0], width))


def _ada_kernel(c_ref, w_ref, b_ref, o_ref):
    o_ref[...] = jnp.dot(c_ref[...].astype(BF16), w_ref[...].astype(BF16),
                         preferred_element_type=F32) + b_ref[...]


def _ada_mod(c_all, ada_w, ada_b):
    depth, d, n = ada_w.shape
    rc = c_all.shape[0]
    tn = 1536
    return pl.pallas_call(
        _ada_kernel,
        grid=(depth, n // tn),
        in_specs=[pl.BlockSpec((rc, d), lambda l, j: (0, 0)),
                  pl.BlockSpec((None, d, tn), lambda l, j: (l, 0, j)),
                  pl.BlockSpec((None, 1, tn), lambda l, j: (l, 0, j))],
        out_specs=pl.BlockSpec((None, rc, tn), lambda l, j: (l, 0, j)),
        out_shape=jax.ShapeDtypeStruct((depth, rc, n), F32),
        name="ada_mod",
    )(c_all, ada_w, ada_b.reshape(depth, 1, n))


def _mod_spec(per_batch_rows, tm, chunk):
    if per_batch_rows:
        tiles = per_batch_rows // tm
        return pl.BlockSpec((None, 1, D_MODEL), lambda i: (i // tiles, 0, chunk))
    return pl.BlockSpec((tm, D_MODEL), lambda i: (i, chunk))


def _in_kernel(transposed, x_ref, sh_ref, sc_ref, ln_ref, w_ref,
               u_ref, q_ref, qi_ref, tail_ref, k_ref, v_ref, *bf16_refs):
    h = _rms(x_ref[...], ln_ref[...]) * (1.0 + sc_ref[...]) + sh_ref[...]
    z = jnp.dot(h.astype(BF16), w_ref[...], preferred_element_type=F32)
    u_ref[...] = z[:, :OFF_Q]
    q_ref[...] = (z[:, OFF_Q:OFF_K] * HEAD_DIM ** -0.5).astype(BF16)
    qi_ref[...] = z[:, OFF_QI:OFF_KI].astype(BF16)
    tail = z[:, OFF_KI:]
    tail_ref[...] = tail
    k = z[:, OFF_K:OFF_V]
    v = z[:, OFF_V:OFF_QI]
    if transposed:
        ki_ref, kb_ref, vb_ref, kib_ref = bf16_refs
        kt, vt, kit = k.T, v.T, tail.T[:IDX_DIM]
        k_ref[...] = kt
        v_ref[...] = vt
        ki_ref[...] = kit
        kb_ref[...] = kt.astype(BF16)
        vb_ref[...] = vt.astype(BF16)
        kib_ref[...] = kit.astype(BF16)
    else:
        k_ref[...] = k
        v_ref[...] = v


def _in_proj(x, mod, per_batch_rows, ln, w_in_b, tm):
    r = x.shape[0]
    row = lambda width: pl.BlockSpec((tm, width), lambda i: (i, 0))
    const = lambda shape: pl.BlockSpec(shape, lambda i: (0, 0))
    outs = [(C_POOL, F32), (D_ATT, BF16), (IDX_HEADS * IDX_DIM, BF16), (TAIL, F32)]
    out_specs = [row(w) for w, _ in outs]
    out_shape = [jax.ShapeDtypeStruct((r, w), dt) for w, dt in outs]
    if per_batch_rows:
        tiles = per_batch_rows // tm
        nb = r // per_batch_rows
        for width, dt in [(D_ATT, F32), (D_ATT, F32), (IDX_DIM, F32), (D_ATT, BF16), (D_ATT, BF16), (IDX_DIM, BF16)]:
            out_specs.append(pl.BlockSpec((None, width, tm), lambda i: (i // tiles, 0, i % tiles)))
            out_shape.append(jax.ShapeDtypeStruct((nb, width, per_batch_rows), dt))
    else:
        out_specs += [row(D_ATT), row(D_ATT)]
        out_shape += [jax.ShapeDtypeStruct((r, D_ATT), F32)] * 2
    return pl.pallas_call(
        functools.partial(_in_kernel, bool(per_batch_rows)),
        grid=(r // tm,),
        in_specs=[row(D_MODEL), _mod_spec(per_batch_rows, tm, 0), _mod_spec(per_batch_rows, tm, 1),
                  const((1, D_MODEL)), const((D_MODEL, D_IN_PAD))],
        out_specs=out_specs,
        out_shape=out_shape,
        compiler_params=pltpu.CompilerParams(vmem_limit_bytes=VMEM_LIMIT),
        name="in_proj",
    )(x, mod, mod, ln, w_in_b)


def _out_kernel(x_ref, yp_ref, ya_ref, g1_ref, sh2_ref, sc2_ref, g2_ref,
                ln1_ref, ln2a_ref, ln2b_ref, wo_ref, w1_ref, w2_ref, o_ref):
    mixed = (jnp.dot(yp_ref[...].astype(BF16), wo_ref[:C_POOL, :], preferred_element_type=F32)
             + jnp.dot(ya_ref[...].astype(BF16), wo_ref[C_POOL:, :], preferred_element_type=F32))
    x1 = x_ref[...] + g1_ref[...] * _rms(mixed, ln1_ref[...])
    h = (_rms(x1, ln2a_ref[...]) * (1.0 + sc2_ref[...]) + sh2_ref[...]).astype(BF16)
    f = jnp.zeros(x1.shape, F32)
    for c in range(D_FF // D_MODEL):
        sl = slice(c * D_MODEL, (c + 1) * D_MODEL)
        a = jnp.maximum(jnp.dot(h, w1_ref[:, sl], preferred_element_type=F32), 0.0)
        f = f + jnp.dot((a * a).astype(BF16), w2_ref[sl, :], preferred_element_type=F32)
    o_ref[...] = x1 + g2_ref[...] * _rms(f, ln2b_ref[...])


def _out_ffn(x, yp, ya, mod, per_batch_rows, ln1_post, ln2_pre, ln2_post, wo_b, w1_b, w2_b, tm):
    r = x.shape[0]
    row = lambda width: pl.BlockSpec((tm, width), lambda i: (i, 0))
    const = lambda shape: pl.BlockSpec(shape, lambda i: (0, 0), pipeline_mode=pl.Buffered(1))
    ms = lambda chunk: _mod_spec(per_batch_rows, tm, chunk)
    return pl.pallas_call(
        _out_kernel,
        grid=(r // tm,),
        in_specs=[row(D_MODEL), row(C_POOL), row(D_ATT), ms(2), ms(3), ms(4), ms(5),
                  const((1, D_MODEL)), const((1, D_MODEL)), const((1, D_MODEL)),
                  const((D_MODEL, D_MODEL)), const((D_MODEL, D_FF)), const((D_FF, D_MODEL))],
        out_specs=row(D_MODEL),
        out_shape=jax.ShapeDtypeStruct((r, D_MODEL), F32),
        compiler_params=pltpu.CompilerParams(vmem_limit_bytes=VMEM_LIMIT),
        name="out_ffn",
    )(x, yp, ya, mod, mod, mod, mod, ln1_post, ln2_pre, ln2_post, wo_b, w1_b, w2_b)


def _pool_kernel(pos0, t_rows, chunk, ue_ref, pw_ref, ps_ref, o_ref):
    g_elems = ue_ref.shape[0]
    n_chunks = t_rows // chunk
    for ci in range(n_chunks):
        r0 = ci * chunk
        pos = pos0 + r0 + lax.broadcasted_iota(I32, (chunk, POOL_GC), 0)
        for g, w in enumerate(POOL_WINDOWS):
            lanes = pl.ds(g * POOL_GC, POOL_GC)
            cnt = jnp.minimum(pos + 1, w).astype(F32)
            pooled = []
            for e in range(g_elems):
                cur = ue_ref[e, pl.ds(POOL_PAD + r0, chunk), lanes]
                acc = cur
                for j in range(1, w):
                    acc = acc + ue_ref[e, pl.ds(POOL_PAD + r0 - j, chunk), lanes]
                pooled.append(acc / cnt - cur)
            pooled = pooled[0] if g_elems == 1 else jnp.concatenate(pooled, axis=0)
            y = jnp.dot(pooled.astype(BF16), pw_ref[g].astype(BF16), preferred_element_type=F32)
            y = y * ps_ref[:, lanes]
            for e in range(g_elems):
                o_ref[e, pl.ds(r0, chunk), lanes] = y[e * chunk:(e + 1) * chunk].astype(o_ref.dtype)


def _pool_mix(u_ext, pos0, pool_w_l, pool_scale_l, g_elems, chunk):
    b, e_rows, _ = u_ext.shape
    t_rows = e_rows - POOL_PAD
    return pl.pallas_call(
        functools.partial(_pool_kernel, pos0, t_rows, chunk),
        grid=(b // g_elems,),
        in_specs=[pl.BlockSpec((g_elems, e_rows, C_POOL), lambda i: (i, 0, 0)),
                  pl.BlockSpec((len(POOL_WINDOWS), POOL_GC, POOL_GC), lambda i: (0, 0, 0)),
                  pl.BlockSpec((1, C_POOL), lambda i: (0, 0))],
        out_specs=pl.BlockSpec((g_elems, t_rows, C_POOL), lambda i: (i, 0, 0)),
        out_shape=jax.ShapeDtypeStruct((b, t_rows, C_POOL), F32),
        compiler_params=pltpu.CompilerParams(vmem_limit_bytes=VMEM_LIMIT),
        name="pool_mix",
    )(u_ext, pool_w_l, pool_scale_l)


def _bias_of_dist(dist, rb_ref, h):
    n = jnp.maximum(dist, 0)
    max_exact = N_BUCKETS // 2
    large = max_exact + (jnp.log(jnp.maximum(n, 1).astype(F32) / max_exact)
                         / math.log(MAX_DISTANCE / max_exact) * (N_BUCKETS - max_exact)).astype(I32)
    large = jnp.minimum(large, N_BUCKETS - 1)
    bucket = jnp.where(n < max_exact, n, large)
    out = jnp.zeros(dist.shape, F32)
    for b in range(N_BUCKETS):
        out = jnp.where(bucket == b, rb_ref[b, h], out)
    return out


def _bias_kernel(blk, rb_ref, t0_ref, t1_ref, t2_ref, s_far_ref, s_last_ref, s_new_ref):
    i = lax.broadcasted_iota(I32, (blk, blk), 0)
    j = lax.broadcasted_iota(I32, (blk, blk), 1)
    far = jnp.full((SUBLANES, blk), 2 * blk, I32)
    t = lax.broadcasted_iota(I32, (SUBLANES, LANES), 0)
    jj = lax.broadcasted_iota(I32, (SUBLANES, LANES), 1)
    for h in range(N_HEADS):
        t0_ref[h] = _bias_of_dist(i - j, rb_ref, h)
        t1_ref[h] = _bias_of_dist(blk + i - j, rb_ref, h)
        t2_ref[h] = _bias_of_dist(far, rb_ref, h)
        rows = pl.ds(h * SUBLANES, SUBLANES)
        s_far_ref[rows, :] = _bias_of_dist(jnp.full((SUBLANES, LANES), 2 * PAGE_SIZE, I32), rb_ref, h)
        s_last_ref[rows, :] = _bias_of_dist(PAGE_SIZE + t - jj, rb_ref, h)
        s_new_ref[rows, :] = _bias_of_dist(t - jj, rb_ref, h)


def _bias_tables(rel_bias, blk):
    sq = jax.ShapeDtypeStruct((N_HEADS, blk, blk), F32)
    sm = jax.ShapeDtypeStruct((N_HEADS * SUBLANES, LANES), F32)
    return pl.pallas_call(
        functools.partial(_bias_kernel, blk),
        in_specs=[pl.BlockSpec(memory_space=pltpu.SMEM)],
        out_shape=[sq, sq, jax.ShapeDtypeStruct((N_HEADS, SUBLANES, blk), F32), sm, sm, sm],
        compiler_params=pltpu.CompilerParams(vmem_limit_bytes=VMEM_LIMIT),
        name="bias_tables",
    )(rel_bias)


def _kth_largest(count_ge, k, shape):
    def body(it, carry):
        tkey, nge = carry
        ckey = tkey + lax.shift_left(jnp.int32(1), 31 - it)
        cnt = count_ge(_key_to_float(ckey))
        ok = cnt >= k
        return jnp.where(ok, ckey, tkey), jnp.where(ok, cnt, nge)

    tkey, nge = lax.fori_loop(0, 32, body, (jnp.full(shape, INT_MIN, I32), jnp.zeros(shape, F32)))
    thr = jnp.where(tkey == INT_MIN, NEG_INF, _key_to_float(tkey))
    return thr, nge


def _tie_limit(count_eq_below, need, shape, bits):
    def body(it, lim):
        cand = lim + lax.shift_left(jnp.int32(1), bits - 1 - it)
        return jnp.where(count_eq_below(cand) <= need, cand, lim)

    return lax.fori_loop(0, bits, body, jnp.zeros(shape, I32))


def _select_mask(s, idx, thr, lim):
    tie = jnp.where(s == thr, jnp.where(idx < lim, 0.0, NEG_INF), NEG_INF)
    m = jnp.where(s > thr, 0.0, tie)
    return jnp.where(s == NEG_INF, NEG_INF, m)


def _attn_prompt_kernel(k_sel, blk, seq,
                        q_ref, qi_ref, tail_ref, kbt_ref, vbt_ref, kibt_ref, t0_ref, t1_ref, t2_ref,
                        o_ref,
                        score_ref, wrep_ref, lg_ref, mrun_ref, lsum_ref, acc_ref, thr_ref, lim_ref):
    qb = pl.program_id(1)
    n_chunks = qb + 1
    rg_rows = 64
    halves = blk // LANES
    idx_bits = max(1, (seq - 1).bit_length()) + 1
    no_limit = jnp.int32(1 << idx_bits)

    tail = tail_ref[...]
    for h in range(IDX_HEADS):
        col = OFF_WI - OFF_KI + h
        wrep_ref[h] = _lane_rep(tail[:, col:col + 1] * IDX_HEADS ** -0.5)

    row = lax.broadcasted_iota(I32, (blk, blk), 0)
    col = lax.broadcasted_iota(I32, (blk, blk), 1)

    def chunk_off(c):
        return pl.multiple_of(c * blk, blk)

    def score_chunk(c, carry):
        off = chunk_off(c)
        kic = kibt_ref[:, pl.ds(off, blk)]
        s = jnp.zeros((blk, blk), F32)
        for h in range(IDX_HEADS):
            d = jnp.dot(qi_ref[:, h * IDX_DIM:(h + 1) * IDX_DIM], kic, preferred_element_type=F32)
            w = jnp.concatenate([wrep_ref[h]] * halves, axis=1)
            s = s + w * jnp.maximum(d * IDX_DIM ** -0.5, 0.0)
        s = jnp.where(col + c * blk <= row + qb * blk, s, NEG_INF)
        score_ref[:, pl.ds(off, blk)] = s
        return carry

    lax.fori_loop(0, n_chunks, score_chunk, 0)

    lane_idx = lax.broadcasted_iota(I32, (rg_rows, LANES), 1)
    for rg in range(blk // rg_rows):
        rows = pl.ds(rg * rg_rows, rg_rows)

        def count_where(pred):
            def body(c, acc):
                off = chunk_off(c)
                s = score_ref[rows, pl.ds(off, blk)]
                for j in range(halves):
                    acc = acc + pred(s[:, j * LANES:(j + 1) * LANES], lane_idx + (c * blk + j * LANES))
                return acc
            acc = lax.fori_loop(0, n_chunks, body, jnp.zeros((rg_rows, LANES), F32))
            return _lane_rep(jnp.sum(acc, axis=1, keepdims=True))

        thr, nge = _kth_largest(
            lambda cf: count_where(lambda s, idx: jnp.where(s >= cf, 1.0, 0.0)), k_sel, (rg_rows, LANES))
        thr_ref[rows, :] = thr
        lim_ref[rows, :] = jnp.full((rg_rows, LANES), no_limit, I32)

        @pl.when(jnp.max(nge) > k_sel)
        def _():
            n_gt = count_where(lambda s, idx: jnp.where(s > thr, 1.0, 0.0))
            need = k_sel - n_gt
            lim_ref[rows, :] = _tie_limit(
                lambda cand: count_where(
                    lambda s, idx: jnp.where(s == thr, jnp.where(idx < cand, 1.0, 0.0), 0.0)),
                need, (rg_rows, LANES), idx_bits)

    thr_all = jnp.concatenate([thr_ref[...]] * halves, axis=1)
    lim_all = jnp.concatenate([lim_ref[...]] * halves, axis=1)

    def mask_chunk(c, carry):
        off = chunk_off(c)
        s = score_ref[:, pl.ds(off, blk)]
        score_ref[:, pl.ds(off, blk)] = _select_mask(s, col + c * blk, thr_all, lim_all)
        return carry

    lax.fori_loop(0, n_chunks, mask_chunk, 0)

    mrun_ref[...] = jnp.full(mrun_ref.shape, NEG_INF, F32)

    def logits_chunk(c, bias_of_head):
        off = chunk_off(c)
        madd = score_ref[:, pl.ds(off, blk)]
        for h in range(N_HEADS):
            hs = slice(h * HEAD_DIM, (h + 1) * HEAD_DIM)
            s = (jnp.dot(q_ref[:, hs], kbt_ref[hs, pl.ds(off, blk)], preferred_element_type=F32)
                 + bias_of_head(h) + madd)
            lg_ref[h, :, pl.ds(off, blk)] = s
            m = s[:, :LANES]
            for j in range(1, halves):
                m = jnp.maximum(m, s[:, j * LANES:(j + 1) * LANES])
            mrun_ref[h] = jnp.maximum(mrun_ref[h], m)

    def far_chunk(c, carry):
        logits_chunk(c, lambda h: t2_ref[h, 0:1, :])
        return carry

    lax.fori_loop(0, jnp.maximum(qb - 1, 0), far_chunk, 0)

    @pl.when(qb >= 1)
    def _():
        logits_chunk(qb - 1, lambda h: t1_ref[h])

    logits_chunk(qb, lambda h: t0_ref[h])

    for h in range(N_HEADS):
        mrun_ref[h] = _lane_rep(jnp.max(mrun_ref[h], axis=1, keepdims=True))
    lsum_ref[...] = jnp.zeros(lsum_ref.shape, F32)
    acc_ref[...] = jnp.zeros(acc_ref.shape, F32)

    def pv_chunk(c, carry):
        off = chunk_off(c)
        for h in range(N_HEADS):
            hs = slice(h * HEAD_DIM, (h + 1) * HEAD_DIM)
            m = jnp.concatenate([mrun_ref[h]] * halves, axis=1)
            p = jnp.exp(lg_ref[h, :, pl.ds(off, blk)] - m)
            ps = p[:, :LANES]
            for j in range(1, halves):
                ps = ps + p[:, j * LANES:(j + 1) * LANES]
            lsum_ref[h] += ps
            acc_ref[h] += _nt_dot(p.astype(BF16), vbt_ref[hs, pl.ds(off, blk)])
        return carry

    lax.fori_loop(0, n_chunks, pv_chunk, 0)

    outs = []
    for h in range(N_HEADS):
        l = jnp.sum(lsum_ref[h], axis=1, keepdims=True)
        outs.append(acc_ref[h] / l)
    o_ref[...] = jnp.concatenate(outs, axis=1).astype(o_ref.dtype)


def _attn_prompt(cfg, k_sel, q, qi, tail, kbt, vbt, kibt, t0, t1, t2):
    blk, seq = cfg.blk, cfg.seq
    nq = seq // blk
    qspec = lambda width: pl.BlockSpec((blk, width), lambda b, i: (b * nq + i, 0))
    kspec = lambda width: pl.BlockSpec((None, width, seq), lambda b, i: (b, 0, 0))
    tspec = lambda shape: pl.BlockSpec(shape, lambda b, i: (0, 0, 0), pipeline_mode=pl.Buffered(1))
    return pl.pallas_call(
        functools.partial(_attn_prompt_kernel, k_sel, blk, seq),
        grid=(cfg.batch, nq),
        in_specs=[qspec(D_ATT), qspec(IDX_HEADS * IDX_DIM), qspec(TAIL),
                  kspec(D_ATT), kspec(D_ATT), kspec(IDX_DIM),
                  tspec((N_HEADS, blk, blk)), tspec((N_HEADS, blk, blk)), tspec((N_HEADS, SUBLANES, blk))],
        out_specs=qspec(D_ATT),
        out_shape=jax.ShapeDtypeStruct((cfg.batch * seq, D_ATT), BF16),
        scratch_shapes=[pltpu.VMEM((blk, seq), F32),
                        pltpu.VMEM((IDX_HEADS, blk, LANES), F32),
                        pltpu.VMEM((N_HEADS, blk, seq), F32),
                        pltpu.VMEM((N_HEADS, blk, LANES), F32),
                        pltpu.VMEM((N_HEADS, blk, LANES), F32),
                        pltpu.VMEM((N_HEADS, blk, HEAD_DIM), F32),
                        pltpu.VMEM((blk, LANES), F32),
                        pltpu.VMEM((blk, LANES), I32)],
        compiler_params=pltpu.CompilerParams(vmem_limit_bytes=VMEM_LIMIT),
        name="attn_prompt",
    )(q, qi, tail, kbt, vbt, kibt, t0, t1, t2)


def _attn_sample_kernel(k_sel, n_pages, pt_ref,
                        qi_ref, wrep_ref, q_ref, tailn_ref, kn_ref, vn_ref, bfar_ref, blast_ref, bnew_ref,
                        *rest):
    ki_refs = rest[:n_pages]
    k_refs = rest[n_pages:2 * n_pages]
    v_refs = rest[2 * n_pages:3 * n_pages]
    o_ref, lim_ref = rest[3 * n_pages:]
    t_rows = SUBLANES
    shape = (t_rows, LANES)
    n_blocks = n_pages + 1
    idx_bits = (n_blocks * PAGE_SIZE - 1).bit_length() + 1
    pad_rows = PAGE_SIZE - t_rows

    qi = qi_ref[...]
    w = wrep_ref[...]
    trow = lax.broadcasted_iota(I32, shape, 0)
    lane = lax.broadcasted_iota(I32, shape, 1)

    def head_sum(x):
        s = x[0:t_rows]
        for h in range(1, IDX_HEADS):
            s = s + x[h * t_rows:(h + 1) * t_rows]
        return s

    def score_block(dots):
        return head_sum(w * jnp.maximum(dots * IDX_DIM ** -0.5, 0.0))

    def pad_new(x):
        return jnp.concatenate([x, jnp.zeros((pad_rows, x.shape[1]), x.dtype)], axis=0)

    scores = [score_block(jnp.dot(qi, ki_refs[p][...].astype(BF16), preferred_element_type=F32))
              for p in range(n_pages)]
    s_new = score_block(_nt_dot(qi, pad_new(tailn_ref[...][:, :IDX_DIM]).astype(BF16)))
    scores.append(jnp.where(lane <= trow, s_new, NEG_INF))
    idxs = [lane + b * PAGE_SIZE for b in range(n_blocks)]

    def count_where(pred):
        acc = jnp.zeros(shape, F32)
        for s, idx in zip(scores, idxs):
            acc = acc + pred(s, idx)
        return _lane_rep(jnp.sum(acc, axis=1, keepdims=True))

    thr, nge = _kth_largest(lambda cf: count_where(lambda s, idx: jnp.where(s >= cf, 1.0, 0.0)), k_sel, shape)
    lim_ref[...] = jnp.full(shape, 1 << idx_bits, I32)

    @pl.when(jnp.max(nge) > k_sel)
    def _():
        n_gt = count_where(lambda s, idx: jnp.where(s > thr, 1.0, 0.0))
        need = k_sel - n_gt
        lim_ref[...] = _tie_limit(
            lambda cand: count_where(lambda s, idx: jnp.where(s == thr, jnp.where(idx < cand, 1.0, 0.0), 0.0)),
            need, shape, idx_bits)

    lim = lim_ref[...]
    masks = [_select_mask(s, idx, thr, lim) for s, idx in zip(scores, idxs)]

    rows = N_HEADS * t_rows
    same_head = (lax.broadcasted_iota(I32, (rows, D_ATT), 0) // t_rows
                 == lax.broadcasted_iota(I32, (rows, D_ATT), 1) // HEAD_DIM)
    q_bd = jnp.where(same_head, jnp.concatenate([q_ref[...]] * N_HEADS, axis=0), 0.0).astype(BF16)

    def tile_heads(m):
        return jnp.concatenate([m] * N_HEADS, axis=0)

    logits = []
    for p in range(n_pages):
        bias = blast_ref[...] if p == n_pages - 1 else bfar_ref[...]
        qk = jnp.dot(q_bd, k_refs[p][...].astype(BF16), preferred_element_type=F32)
        logits.append(qk + bias + tile_heads(masks[p]))
    logits.append(_nt_dot(q_bd, pad_new(kn_ref[...]).astype(BF16)) + bnew_ref[...] + tile_heads(masks[n_pages]))

    m = logits[0]
    for lg in logits[1:]:
        m = jnp.maximum(m, lg)
    m = _lane_rep(jnp.max(m, axis=1, keepdims=True))
    lsum = jnp.zeros((rows, LANES), F32)
    acc = jnp.zeros((rows, D_ATT), F32)
    for b in range(n_blocks):
        p = jnp.exp(logits[b] - m)
        lsum = lsum + p
        if b < n_pages:
            acc = acc + _nt_dot(p.astype(BF16), v_refs[b][...].astype(BF16))
        else:
            acc = acc + jnp.dot(p.astype(BF16), pad_new(vn_ref[...]).astype(BF16), preferred_element_type=F32)
    out = jnp.where(same_head, acc / jnp.sum(lsum, axis=1, keepdims=True), 0.0)
    o_ref[...] = head_sum(out)


def _attn_sample(cfg, layer, k_sel, page_table, qi_ht, wrep, q, tail_new, k_new, v_new,
                 s_far, s_last, s_new, cache_ki, cache_k, cache_v):
    b = cfg.dec_batch
    t_rows = cfg.dec_seq
    n_pages = cfg.past_len // PAGE_SIZE
    rows = N_HEADS * t_rows
    per_b = lambda r, w: pl.BlockSpec((None, r, w), lambda i, pt: (i, 0, 0))
    const = lambda r, w: pl.BlockSpec((r, w), lambda i, pt: (0, 0))

    def page_spec(width, p):
        return pl.BlockSpec((None, None, width, PAGE_SIZE), lambda i, pt: (layer, pt[i, p], 0, 0))

    in_specs = ([per_b(rows, IDX_DIM), per_b(rows, LANES), per_b(t_rows, D_ATT), per_b(t_rows, TAIL),
                 per_b(t_rows, D_ATT), per_b(t_rows, D_ATT),
                 const(rows, LANES), const(rows, LANES), const(rows, LANES)]
                + [page_spec(IDX_DIM, p) for p in range(n_pages)]
                + [page_spec(D_ATT, p) for p in range(n_pages)]
                + [page_spec(D_ATT, p) for p in range(n_pages)])
    grid_spec = pltpu.PrefetchScalarGridSpec(
        num_scalar_prefetch=1,
        grid=(b,),
        in_specs=in_specs,
        out_specs=pl.BlockSpec((None, t_rows, D_ATT), lambda i, pt: (i, 0, 0)),
        scratch_shapes=[pltpu.VMEM((t_rows, LANES), I32)])
    return pl.pallas_call(
        functools.partial(_attn_sample_kernel, k_sel, n_pages),
        grid_spec=grid_spec,
        out_shape=jax.ShapeDtypeStruct((b, t_rows, D_ATT), F32),
        compiler_params=pltpu.CompilerParams(vmem_limit_bytes=VMEM_LIMIT),
        name="attn_sample",
    )(page_table, qi_ht, wrep, q, tail_new, k_new, v_new, s_far, s_last, s_new,
      *([cache_ki] * n_pages), *([cache_k] * n_pages), *([cache_v] * n_pages))


def _forward(cfg, x_prompt, x_sample, cache_k, cache_v, cache_kidx, state_pool, page_table, c_prompt,
             c_sample, rel_bias, ada_w, ada_b, ln1_pre, ln1_post, ln2_pre, ln2_post, w_in, pool_w,
             pool_scale, w_out, w_ff1, w_ff2):
    depth = ada_w.shape[0]
    bp, seq, bs, ts = cfg.batch, cfg.seq, cfg.dec_batch, cfg.dec_seq
    assert ts == SUBLANES and seq % cfg.blk == 0 and cfg.past_len % PAGE_SIZE == 0
    assert cfg.blk >= PAGE_SIZE and seq >= TOPK_MAX
    k_prompt = min(TOPK_MAX, seq // 4)
    k_sample = min(TOPK_MAX, (cfg.past_len + ts) // 4)
    rp, rs = bp * seq, bs * ts
    tm_s = min(cfg.tm, rs)

    mod = _ada_mod(jnp.concatenate([c_prompt, c_sample], axis=0), ada_w, ada_b)
    t0, t1, t2, s_far, s_last, s_new = _bias_tables(rel_bias, cfg.blk)

    n_phys = cache_k.shape[1]
    cache_kt = cache_k.transpose(0, 1, 3, 4, 2).reshape(depth, n_phys, D_ATT, PAGE_SIZE)
    cache_vt = cache_v.transpose(0, 1, 3, 4, 2).reshape(depth, n_phys, D_ATT, PAGE_SIZE)
    cache_kit = cache_kidx.transpose(0, 1, 3, 2)

    xp = x_prompt.reshape(rp, D_MODEL)
    xs = x_sample.reshape(rs, D_MODEL)
    outs_p = [[], [], [], []]
    outs_s = [[], [], [], []]
    for l in range(depth):
        w_in_b = jnp.pad(w_in[l], ((0, 0), (0, D_IN_PAD - D_IN))).astype(BF16)
        wo_b, w1_b, w2_b = w_out[l].astype(BF16), w_ff1[l].astype(BF16), w_ff2[l].astype(BF16)
        lnrow = lambda a: a[l].reshape(1, D_MODEL)
        mod_p = mod[l, :bp].reshape(bp, 1, 6 * D_MODEL)
        mod_s = jnp.repeat(mod[l, bp:], ts, axis=0)
        psc = pool_scale[l].reshape(1, C_POOL)

        u, q, qi, tail, kt, vt, kit, kbt, vbt, kibt = _in_proj(xp, mod_p, seq, lnrow(ln1_pre), w_in_b, cfg.tm)
        u3 = u.reshape(bp, seq, C_POOL)
        y_pool = _pool_mix(jnp.pad(u3, ((0, 0), (POOL_PAD, 0), (0, 0))), 0, pool_w[l], psc, 1,
                           min(256, seq)).reshape(rp, C_POOL)
        y_att = _attn_prompt(cfg, k_prompt, q, qi, tail, kbt, vbt, kibt, t0, t1, t2)
        xp = _out_ffn(xp, y_pool, y_att, mod_p, seq, lnrow(ln1_post), lnrow(ln2_pre), lnrow(ln2_post),
                      wo_b, w1_b, w2_b, cfg.tm)
        heads_last = lambda a: a.reshape(bp, N_HEADS, HEAD_DIM, seq).transpose(0, 3, 1, 2)
        for lst, a in zip(outs_p, (heads_last(kt), heads_last(vt), kit.transpose(0, 2, 1),
                                   u3[:, seq - POOL_HIST:])):
            lst.append(a)

        u, q, qi, tail, k, v = _in_proj(xs, mod_s, 0, lnrow(ln1_pre), w_in_b, tm_s)
        u3 = u.reshape(bs, ts, C_POOL)
        u_ext = jnp.concatenate([jnp.zeros((bs, POOL_PAD - POOL_HIST, C_POOL), F32), state_pool[l], u3], axis=1)
        y_pool = _pool_mix(u_ext, cfg.past_len, pool_w[l], psc, min(16, bs), ts).reshape(rs, C_POOL)
        qi_ht = qi.reshape(bs, ts, IDX_HEADS, IDX_DIM).transpose(0, 2, 1, 3).reshape(bs, IDX_HEADS * ts, IDX_DIM)
        wi = tail[:, OFF_WI - OFF_KI:OFF_WI - OFF_KI + IDX_HEADS].reshape(bs, ts, IDX_HEADS) * IDX_HEADS ** -0.5
        wrep = jnp.broadcast_to(wi.transpose(0, 2, 1).reshape(bs, IDX_HEADS * ts, 1), (bs, IDX_HEADS * ts, LANES))
        y_att = _attn_sample(cfg, l, k_sample, page_table, qi_ht, wrep,
                             q.astype(F32).reshape(bs, ts, D_ATT), tail.reshape(bs, ts, TAIL),
                             k.reshape(bs, ts, D_ATT), v.reshape(bs, ts, D_ATT),
                             s_far, s_last, s_new, cache_kit, cache_kt, cache_vt).reshape(rs, D_ATT)
        xs = _out_ffn(xs, y_pool, y_att, mod_s, 0, lnrow(ln1_post), lnrow(ln2_pre), lnrow(ln2_post),
                      wo_b, w1_b, w2_b, tm_s)
        for lst, a in zip(outs_s, (k.reshape(bs, ts, N_HEADS, HEAD_DIM), v.reshape(bs, ts, N_HEADS, HEAD_DIM),
                                   tail[:, :IDX_DIM].reshape(bs, ts, IDX_DIM), u_ext[:, -POOL_HIST:])):
            lst.append(a)

    stack = lambda lists: [jnp.stack(a) for a in lists]
    return (xp.reshape(bp, seq, D_MODEL), xs.reshape(bs, ts, D_MODEL), *stack(outs_p), *stack(outs_s))


def kernel(x_prompt, x_sample, cache_k, cache_v, cache_kidx, state_pool, page_table, c_prompt, c_sample,
           rel_bias, ada_w, ada_b, ln1_pre, ln1_post, ln2_pre, ln2_post, w_in, pool_w, pool_scale,
           w_out, w_ff1, w_ff2):
    cfg = Cfg(batch=x_prompt.shape[0], seq=x_prompt.shape[1], dec_batch=x_sample.shape[0],
              dec_seq=x_sample.shape[1], past_len=page_table.shape[1] * PAGE_SIZE, blk=256, tm=256)
    return _forward(cfg, x_prompt, x_sample, cache_k, cache_v, cache_kidx, state_pool, page_table, c_prompt,
                    c_sample, rel_bias, ada_w, ada_b, ln1_pre, ln1_post, ln2_pre, ln2_post, w_in, pool_w,
                    pool_scale, w_out, w_ff1, w_ff2)
```

```python
import functools
import math
from typing import NamedTuple

import jax
import jax.numpy as jnp
from jax import lax
from jax.experimental import pallas as pl
from jax.experimental.pallas import tpu as pltpu

F32 = jnp.float32
BF16 = jnp.bfloat16
I32 = jnp.int32

D_MODEL = 1024
C_POOL = 512
D_ATT = 512
HEAD_DIM = 64
N_HEADS = 8
IDX_HEADS = 8
IDX_DIM = 64
POOL_WINDOWS = (2, 4, 8, 16)
POOL_GC = C_POOL // len(POOL_WINDOWS)
POOL_HIST = max(POOL_WINDOWS) - 1
TOPK_MAX = 256
N_BUCKETS = 32
MAX_DISTANCE = 128
D_FF = 4 * D_MODEL
EPS = 1e-6
PAGE_SIZE = 128
OFF_Q = C_POOL
OFF_K = OFF_Q + D_ATT
OFF_V = OFF_K + D_ATT
OFF_QI = OFF_V + D_ATT
OFF_KI = OFF_QI + IDX_HEADS * IDX_DIM
OFF_WI = OFF_KI + IDX_DIM
D_IN = OFF_WI + IDX_HEADS

LANES = 128
SUBLANES = 8
BF16_ROWS = 2 * SUBLANES
D_IN_PAD = ((D_IN + LANES - 1) // LANES) * LANES
TAIL = D_IN_PAD - OFF_KI
POOL_PAD = 16
VMEM_LIMIT = 56 * 1024 * 1024

NEG_INF = float("-inf")
INT_MIN = -(2 ** 31)


class Cfg(NamedTuple):
    batch: int
    seq: int
    dec_batch: int
    dec_seq: int
    past_len: int
    blk: int
    tm: int


def _rms(x, g):
    ms = jnp.mean(x * x, axis=-1, keepdims=True)
    return x * lax.rsqrt(ms + EPS) * g


def _nt_dot(a, b):
    return lax.dot_general(a, b, (((1,), (1,)), ((), ())), preferred_element_type=F32)


def _dot(a, b):
    return jnp.dot(a, b, preferred_element_type=F32)


def _key_to_float(key):
    bits = key ^ ((key >> 31) & jnp.int32(0x7FFFFFFF))
    return lax.bitcast_convert_type(bits, F32)


def _lane_rep(col, width=LANES):
    return jnp.broadcast_to(col, (col.shape[0], width))


def _fold_tiles(x, op=jnp.add):
    acc = x[0:SUBLANES]
    for r in range(1, x.shape[0] // SUBLANES):
        acc = op(acc, x[r * SUBLANES:(r + 1) * SUBLANES])
    return acc


def _ada_kernel(c_ref, w_ref, b_ref, o_ref):
    o_ref[...] = _dot(c_ref[...].astype(BF16), w_ref[...].astype(BF16)) + b_ref[...]


def _ada_mod(c_all, ada_w, ada_b):
    depth, d, n = ada_w.shape
    rc = c_all.shape[0]
    tn = 1536
    return pl.pallas_call(
        _ada_kernel,
        grid=(depth, n // tn),
        in_specs=[pl.BlockSpec((rc, d), lambda l, j: (0, 0)),
                  pl.BlockSpec((None, d, tn), lambda l, j: (l, 0, j)),
                  pl.BlockSpec((None, 1, tn), lambda l, j: (l, 0, j))],
        out_specs=pl.BlockSpec((None, rc, tn), lambda l, j: (l, 0, j)),
        out_shape=jax.ShapeDtypeStruct((depth, rc, n), F32),
        name="ada_mod",
    )(c_all, ada_w, ada_b.reshape(depth, 1, n))


def _mod_spec(per_batch_rows, tm, chunk):
    if per_batch_rows:
        tiles = per_batch_rows // tm
        return pl.BlockSpec((None, 1, D_MODEL), lambda i: (i // tiles, 0, chunk))
    return pl.BlockSpec((tm, D_MODEL), lambda i: (i, chunk))


def _in_kernel(transposed, x_ref, sh_ref, sc_ref, ln_ref, w_ref, u_ref, *refs):
    h = _rms(x_ref[...], ln_ref[...]) * (1.0 + sc_ref[...]) + sh_ref[...]
    z = _dot(h.astype(BF16), w_ref[...])
    u_ref[...] = z[:, :OFF_Q]
    q = z[:, OFF_Q:OFF_K] * HEAD_DIM ** -0.5
    k = z[:, OFF_K:OFF_V]
    v = z[:, OFF_V:OFF_QI]
    qi = z[:, OFF_QI:OFF_KI]
    tail = z[:, OFF_KI:]
    if transposed:
        kb_ref, tailb_ref, kt_ref, vt_ref, kit_ref, wit_ref, qt_ref, qit_ref, vbt_ref = refs
        kb_ref[...] = k.astype(BF16)
        tailb_ref[...] = tail.astype(BF16)
        kt_ref[...] = k.T
        vt = v.T
        vt_ref[...] = vt
        vbt_ref[...] = vt.astype(BF16)
        tail_t = tail.T
        kit_ref[...] = tail_t[:IDX_DIM]
        wit_ref[...] = tail_t[OFF_WI - OFF_KI:OFF_WI - OFF_KI + IDX_HEADS]
        qt_ref[...] = q.T.astype(BF16)
        qit_ref[...] = qi.T.astype(BF16)
    else:
        q_ref, qi_ref, tail_ref, k_ref, v_ref = refs
        q_ref[...] = q.astype(BF16)
        qi_ref[...] = qi.astype(BF16)
        tail_ref[...] = tail
        k_ref[...] = k
        v_ref[...] = v


def _in_proj(x, mod, per_batch_rows, ln, w_in_b, tm):
    r = x.shape[0]
    row = lambda width: pl.BlockSpec((tm, width), lambda i: (i, 0))
    const = lambda shape: pl.BlockSpec(shape, lambda i: (0, 0))
    if per_batch_rows:
        tiles = per_batch_rows // tm
        nb = r // per_batch_rows
        rows_out = [(C_POOL, F32), (D_ATT, BF16), (TAIL, BF16)]
        cols_out = [(D_ATT, F32), (D_ATT, F32), (IDX_DIM, F32), (IDX_HEADS, F32),
                    (D_ATT, BF16), (IDX_HEADS * IDX_DIM, BF16), (D_ATT, BF16)]
    else:
        rows_out = [(C_POOL, F32), (D_ATT, BF16), (IDX_HEADS * IDX_DIM, BF16), (TAIL, F32), (D_ATT, F32), (D_ATT, F32)]
        cols_out = []
    out_specs = [row(w) for w, _ in rows_out]
    out_shape = [jax.ShapeDtypeStruct((r, w), dt) for w, dt in rows_out]
    for width, dt in cols_out:
        out_specs.append(pl.BlockSpec((None, width, tm), lambda i: (i // tiles, 0, i % tiles)))
        out_shape.append(jax.ShapeDtypeStruct((nb, width, per_batch_rows), dt))
    return pl.pallas_call(
        functools.partial(_in_kernel, bool(per_batch_rows)),
        grid=(r // tm,),
        in_specs=[row(D_MODEL), _mod_spec(per_batch_rows, tm, 0), _mod_spec(per_batch_rows, tm, 1),
                  const((1, D_MODEL)), const((D_MODEL, D_IN_PAD))],
        out_specs=out_specs,
        out_shape=out_shape,
        compiler_params=pltpu.CompilerParams(vmem_limit_bytes=VMEM_LIMIT),
        name="in_proj",
    )(x, mod, mod, ln, w_in_b)


def _out_kernel(x_ref, yp_ref, ya_ref, g1_ref, sh2_ref, sc2_ref, g2_ref,
                ln1_ref, ln2a_ref, ln2b_ref, wo_ref, w1_ref, w2_ref, o_ref):
    mixed = (_dot(yp_ref[...].astype(BF16), wo_ref[:C_POOL, :])
             + _dot(ya_ref[...].astype(BF16), wo_ref[C_POOL:, :]))
    x1 = x_ref[...] + g1_ref[...] * _rms(mixed, ln1_ref[...])
    h = (_rms(x1, ln2a_ref[...]) * (1.0 + sc2_ref[...]) + sh2_ref[...]).astype(BF16)
    f = jnp.zeros(x1.shape, F32)
    for c in range(D_FF // D_MODEL):
        sl = slice(c * D_MODEL, (c + 1) * D_MODEL)
        a = jnp.maximum(_dot(h, w1_ref[:, sl]), 0.0)
        f = f + _dot((a * a).astype(BF16), w2_ref[sl, :])
    o_ref[...] = x1 + g2_ref[...] * _rms(f, ln2b_ref[...])


def _out_ffn(x, yp, ya, mod, per_batch_rows, ln1_post, ln2_pre, ln2_post, wo_b, w1_b, w2_b, tm):
    r = x.shape[0]
    row = lambda width: pl.BlockSpec((tm, width), lambda i: (i, 0))
    const = lambda shape: pl.BlockSpec(shape, lambda i: (0, 0), pipeline_mode=pl.Buffered(1))
    ms = lambda chunk: _mod_spec(per_batch_rows, tm, chunk)
    return pl.pallas_call(
        _out_kernel,
        grid=(r // tm,),
        in_specs=[row(D_MODEL), row(C_POOL), row(D_ATT), ms(2), ms(3), ms(4), ms(5),
                  const((1, D_MODEL)), const((1, D_MODEL)), const((1, D_MODEL)),
                  const((D_MODEL, D_MODEL)), const((D_MODEL, D_FF)), const((D_FF, D_MODEL))],
        out_specs=row(D_MODEL),
        out_shape=jax.ShapeDtypeStruct((r, D_MODEL), F32),
        compiler_params=pltpu.CompilerParams(vmem_limit_bytes=VMEM_LIMIT),
        name="out_ffn",
    )(x, yp, ya, mod, mod, mod, mod, ln1_post, ln2_pre, ln2_post, wo_b, w1_b, w2_b)


def _pool_kernel(pos0, t_rows, chunk, ue_ref, pw_ref, ps_ref, o_ref):
    g_elems = ue_ref.shape[0]
    n_chunks = t_rows // chunk
    for ci in range(n_chunks):
        r0 = ci * chunk
        pos = pos0 + r0 + lax.broadcasted_iota(I32, (chunk, POOL_GC), 0)
        for g, w in enumerate(POOL_WINDOWS):
            lanes = pl.ds(g * POOL_GC, POOL_GC)
            cnt = jnp.minimum(pos + 1, w).astype(F32)
            pooled = []
            for e in range(g_elems):
                cur = ue_ref[e, pl.ds(POOL_PAD + r0, chunk), lanes]
                acc = cur
                for j in range(1, w):
                    acc = acc + ue_ref[e, pl.ds(POOL_PAD + r0 - j, chunk), lanes]
                pooled.append(acc / cnt - cur)
            pooled = pooled[0] if g_elems == 1 else jnp.concatenate(pooled, axis=0)
            y = _dot(pooled.astype(BF16), pw_ref[g].astype(BF16))
            y = y * ps_ref[:, lanes]
            for e in range(g_elems):
                o_ref[e, pl.ds(r0, chunk), lanes] = y[e * chunk:(e + 1) * chunk].astype(o_ref.dtype)


def _pool_mix(u_ext, pos0, pool_w_l, pool_scale_l, g_elems, chunk):
    b, e_rows, _ = u_ext.shape
    t_rows = e_rows - POOL_PAD
    return pl.pallas_call(
        functools.partial(_pool_kernel, pos0, t_rows, chunk),
        grid=(b // g_elems,),
        in_specs=[pl.BlockSpec((g_elems, e_rows, C_POOL), lambda i: (i, 0, 0)),
                  pl.BlockSpec((len(POOL_WINDOWS), POOL_GC, POOL_GC), lambda i: (0, 0, 0)),
                  pl.BlockSpec((1, C_POOL), lambda i: (0, 0))],
        out_specs=pl.BlockSpec((g_elems, t_rows, C_POOL), lambda i: (i, 0, 0)),
        out_shape=jax.ShapeDtypeStruct((b, t_rows, C_POOL), F32),
        compiler_params=pltpu.CompilerParams(vmem_limit_bytes=VMEM_LIMIT),
        name="pool_mix",
    )(u_ext, pool_w_l, pool_scale_l)


def _bias_of_dist(dist, rb_ref, h):
    n = jnp.maximum(dist, 0)
    max_exact = N_BUCKETS // 2
    large = max_exact + (jnp.log(jnp.maximum(n, 1).astype(F32) / max_exact)
                         / math.log(MAX_DISTANCE / max_exact) * (N_BUCKETS - max_exact)).astype(I32)
    large = jnp.minimum(large, N_BUCKETS - 1)
    bucket = jnp.where(n < max_exact, n, large)
    out = jnp.zeros(dist.shape, F32)
    for b in range(N_BUCKETS):
        out = jnp.where(bucket == b, rb_ref[b, h], out)
    return out


def _bias_kernel(blk, rb_ref, t0_ref, t1_ref, t2_ref, s_far_ref, s_last_ref, s_new_ref):
    key = lax.broadcasted_iota(I32, (blk, blk), 0)
    qry = lax.broadcasted_iota(I32, (blk, blk), 1)
    far = jnp.full((SUBLANES, blk), 2 * blk, I32)
    t = lax.broadcasted_iota(I32, (SUBLANES, LANES), 0)
    jj = lax.broadcasted_iota(I32, (SUBLANES, LANES), 1)
    for h in range(N_HEADS):
        t0_ref[h] = _bias_of_dist(qry - key, rb_ref, h)
        t1_ref[h] = _bias_of_dist(blk + qry - key, rb_ref, h)
        t2_ref[h] = _bias_of_dist(far, rb_ref, h)
        rows = pl.ds(h * SUBLANES, SUBLANES)
        s_far_ref[rows, :] = _bias_of_dist(jnp.full((SUBLANES, LANES), 2 * PAGE_SIZE, I32), rb_ref, h)
        s_last_ref[rows, :] = _bias_of_dist(PAGE_SIZE + t - jj, rb_ref, h)
        s_new_ref[rows, :] = _bias_of_dist(t - jj, rb_ref, h)


def _bias_tables(rel_bias, blk):
    sq = jax.ShapeDtypeStruct((N_HEADS, blk, blk), F32)
    sm = jax.ShapeDtypeStruct((N_HEADS * SUBLANES, LANES), F32)
    return pl.pallas_call(
        functools.partial(_bias_kernel, blk),
        in_specs=[pl.BlockSpec(memory_space=pltpu.SMEM)],
        out_shape=[sq, sq, jax.ShapeDtypeStruct((N_HEADS, SUBLANES, blk), F32), sm, sm, sm],
        compiler_params=pltpu.CompilerParams(vmem_limit_bytes=VMEM_LIMIT),
        name="bias_tables",
    )(rel_bias)


def _kth_start(shape):
    return jnp.full(shape, INT_MIN, I32), jnp.zeros(shape, F32)


def _kth_rounds(count_ge_key, k, carry, first_round, last_round, digit_bits=1):
    def body(r, carry):
        tkey, nge = carry
        shift = 32 - digit_bits * (r + 1)
        best_key, best_cnt = tkey, nge
        for j in range(1, 1 << digit_bits):
            ckey = tkey + lax.shift_left(jnp.int32(j), shift)
            cnt = count_ge_key(ckey)
            ok = cnt >= k
            best_key = jnp.where(ok, ckey, best_key)
            best_cnt = jnp.where(ok, cnt, best_cnt)
        return best_key, best_cnt

    return lax.fori_loop(first_round, last_round, body, carry)


def _kth_finish(carry):
    tkey, nge = carry
    return jnp.where(tkey == INT_MIN, NEG_INF, _key_to_float(tkey)), nge


def _kth_largest(count_ge, k, shape, digit_bits=1):
    carry = _kth_rounds(lambda ckey: count_ge(_key_to_float(ckey)), k, _kth_start(shape),
                        0, 32 // digit_bits, digit_bits)
    return _kth_finish(carry)


def _tie_limit(count_eq_below, need, shape, bits):
    def body(it, lim):
        cand = lim + lax.shift_left(jnp.int32(1), bits - 1 - it)
        return jnp.where(count_eq_below(cand) <= need, cand, lim)

    return lax.fori_loop(0, bits, body, jnp.zeros(shape, I32))


def _select_mask(s, idx, thr, lim):
    tie = jnp.where(s == thr, jnp.where(idx < lim, 0.0, NEG_INF), NEG_INF)
    m = jnp.where(s > thr, 0.0, tie)
    return jnp.where(s == NEG_INF, NEG_INF, m)


def _attn_prompt_kernel(k_sel, blk, seq,
                        qt_ref, qit_ref, wit_ref, kb_ref, kib_ref, vbt_ref, t0_ref, t1_ref, t2_ref,
                        o_ref,
                        score_ref, trunc_ref, lg_ref, mrun_ref, lsum_ref, acc_ref, thr_ref, lim_ref):
    qb = pl.program_id(1)
    n_chunks = qb + 1
    rep = (SUBLANES, blk)
    idx_bits = max(1, (seq - 1).bit_length()) + 1

    key_row = lax.broadcasted_iota(I32, (blk, blk), 0)
    qry_col = lax.broadcasted_iota(I32, (blk, blk), 1)

    def chunk_keys(c):
        return pl.ds(pl.multiple_of(c * blk, blk), blk)

    def all_sublanes(x, op):
        return jnp.broadcast_to(op(x, axis=0, keepdims=True), rep)

    w_idx = wit_ref[...] * IDX_HEADS ** -0.5 * IDX_DIM ** -0.5

    def score_chunk(c, carry):
        keys = chunk_keys(c)
        kic = kib_ref[keys, :][:, :IDX_DIM]
        s = jnp.zeros((blk, blk), F32)
        for h in range(IDX_HEADS):
            d = _dot(kic, qit_ref[h * IDX_DIM:(h + 1) * IDX_DIM, :])
            s = s + w_idx[h:h + 1, :] * jnp.maximum(d, 0.0)
        s = jnp.where(key_row + c * blk <= qry_col + qb * blk, s, NEG_INF)
        score_ref[keys, :] = s
        hi = lax.bitcast_convert_type(s, I32) & jnp.int32(-(1 << 16))
        trunc_ref[keys, :] = lax.bitcast_convert_type(hi, F32).astype(BF16)
        return carry

    lax.fori_loop(0, n_chunks, score_chunk, 0)

    def count_where(pred):
        def body(c, acc):
            return acc + _fold_tiles(pred(score_ref[chunk_keys(c), :], key_row + c * blk))
        acc = lax.fori_loop(0, n_chunks, body, jnp.zeros(rep, F32))
        return all_sublanes(acc, jnp.sum)

    def count_upper_half(ckey):
        bits = ckey ^ ((ckey >> 31) & jnp.int32(0x7FFFFFFF))
        cf = lax.bitcast_convert_type(bits & jnp.int32(-(1 << 16)), F32)
        cb = jnp.concatenate([cf, cf], axis=0).astype(BF16)
        one, zero = jnp.ones((BF16_ROWS, blk), BF16), jnp.zeros((BF16_ROWS, blk), BF16)

        def body(c, acc):
            t = trunc_ref[chunk_keys(c), :]
            for r in range(blk // BF16_ROWS):
                acc = acc + jnp.where(t[r * BF16_ROWS:(r + 1) * BF16_ROWS] >= cb, one, zero)
            return acc
        acc = lax.fori_loop(0, n_chunks, body, zero)
        return all_sublanes(acc.astype(F32), jnp.sum)

    carry = _kth_rounds(count_upper_half, k_sel, _kth_start(rep), 0, 16)
    carry = _kth_rounds(
        lambda ckey: count_where(lambda s, idx: jnp.where(s >= _key_to_float(ckey)[0:1], 1.0, 0.0)),
        k_sel, carry, 16, 32)
    thr, nge = _kth_finish(carry)
    thr_ref[...] = thr
    lim_ref[...] = jnp.full(rep, 1 << idx_bits, I32)

    @pl.when(jnp.max(nge) > k_sel)
    def _():
        n_gt = count_where(lambda s, idx: jnp.where(s > thr[0:1], 1.0, 0.0))
        need = k_sel - n_gt
        lim_ref[...] = _tie_limit(
            lambda cand: count_where(
                lambda s, idx: jnp.where(s == thr[0:1], jnp.where(idx < cand[0:1], 1.0, 0.0), 0.0)),
            need, rep, idx_bits)

    def mask_chunk(c, carry):
        keys = chunk_keys(c)
        score_ref[keys, :] = _select_mask(score_ref[keys, :], key_row + c * blk,
                                          thr_ref[0:1, :], lim_ref[0:1, :])
        return carry

    lax.fori_loop(0, n_chunks, mask_chunk, 0)

    mrun_ref[...] = jnp.full(mrun_ref.shape, NEG_INF, F32)

    def logits_chunk(c, bias_of_head):
        keys = chunk_keys(c)
        madd = score_ref[keys, :]
        for h in range(N_HEADS):
            hs = slice(h * HEAD_DIM, (h + 1) * HEAD_DIM)
            s = _dot(kb_ref[keys, hs], qt_ref[hs, :]) + bias_of_head(h) + madd
            lg_ref[h, keys, :] = s
            mrun_ref[h] = jnp.maximum(mrun_ref[h], _fold_tiles(s, jnp.maximum))

    def far_chunk(c, carry):
        logits_chunk(c, lambda h: t2_ref[h, 0:1, :])
        return carry

    lax.fori_loop(0, jnp.maximum(qb - 1, 0), far_chunk, 0)

    @pl.when(qb >= 1)
    def _():
        logits_chunk(qb - 1, lambda h: t1_ref[h])

    logits_chunk(qb, lambda h: t0_ref[h])

    for h in range(N_HEADS):
        mrun_ref[h] = all_sublanes(mrun_ref[h], jnp.max)
    lsum_ref[...] = jnp.zeros(lsum_ref.shape, F32)
    acc_ref[...] = jnp.zeros(acc_ref.shape, F32)

    def pv_chunk(c, carry):
        keys = chunk_keys(c)
        for h in range(N_HEADS):
            hs = slice(h * HEAD_DIM, (h + 1) * HEAD_DIM)
            p = jnp.exp(lg_ref[h, keys, :] - mrun_ref[h, 0:1, :])
            lsum_ref[h] += _fold_tiles(p)
            acc_ref[h] += _dot(vbt_ref[hs, keys], p.astype(BF16))
        return carry

    lax.fori_loop(0, n_chunks, pv_chunk, 0)

    outs = [acc_ref[h] / jnp.sum(lsum_ref[h], axis=0, keepdims=True) for h in range(N_HEADS)]
    o_ref[...] = jnp.concatenate(outs, axis=0).T.astype(o_ref.dtype)


def _attn_prompt(cfg, k_sel, qt, qit, wit, kb, kib, vbt, t0, t1, t2):
    blk, seq = cfg.blk, cfg.seq
    nq = seq // blk
    qspec = lambda width: pl.BlockSpec((None, width, blk), lambda b, i: (b, 0, i))
    kspec = lambda width: pl.BlockSpec((seq, width), lambda b, i: (b, 0))
    tspec = lambda shape: pl.BlockSpec(shape, lambda b, i: (0, 0, 0), pipeline_mode=pl.Buffered(1))
    return pl.pallas_call(
        functools.partial(_attn_prompt_kernel, k_sel, blk, seq),
        grid=(cfg.batch, nq),
        in_specs=[qspec(D_ATT), qspec(IDX_HEADS * IDX_DIM), qspec(IDX_HEADS),
                  kspec(D_ATT), kspec(TAIL), pl.BlockSpec((None, D_ATT, seq), lambda b, i: (b, 0, 0)),
                  tspec((N_HEADS, blk, blk)), tspec((N_HEADS, blk, blk)), tspec((N_HEADS, SUBLANES, blk))],
        out_specs=pl.BlockSpec((blk, D_ATT), lambda b, i: (b * nq + i, 0)),
        out_shape=jax.ShapeDtypeStruct((cfg.batch * seq, D_ATT), BF16),
        scratch_shapes=[pltpu.VMEM((seq, blk), F32),
                        pltpu.VMEM((seq, blk), BF16),
                        pltpu.VMEM((N_HEADS, seq, blk), F32),
                        pltpu.VMEM((N_HEADS, SUBLANES, blk), F32),
                        pltpu.VMEM((N_HEADS, SUBLANES, blk), F32),
                        pltpu.VMEM((N_HEADS, HEAD_DIM, blk), F32),
                        pltpu.VMEM((SUBLANES, blk), F32),
                        pltpu.VMEM((SUBLANES, blk), I32)],
        compiler_params=pltpu.CompilerParams(vmem_limit_bytes=VMEM_LIMIT),
        name="attn_prompt",
    )(qt, qit, wit, kb, kib, vbt, t0, t1, t2)


def _attn_sample_kernel(k_sel, n_pages, pt_ref,
                        qi_ref, wrep_ref, q_ref, tailn_ref, kn_ref, vn_ref, bfar_ref, blast_ref, bnew_ref,
                        *rest):
    ki_refs = rest[:n_pages]
    k_refs = rest[n_pages:2 * n_pages]
    v_refs = rest[2 * n_pages:3 * n_pages]
    o_ref, lim_ref = rest[3 * n_pages:]
    t_rows = SUBLANES
    shape = (t_rows, LANES)
    n_blocks = n_pages + 1
    idx_bits = (n_blocks * PAGE_SIZE - 1).bit_length() + 1
    pad_rows = PAGE_SIZE - t_rows

    qi = qi_ref[...]
    w = wrep_ref[...]
    trow = lax.broadcasted_iota(I32, shape, 0)
    lane = lax.broadcasted_iota(I32, shape, 1)

    def score_block(dots):
        return _fold_tiles(w * jnp.maximum(dots * IDX_DIM ** -0.5, 0.0))

    def pad_new(x):
        return jnp.concatenate([x, jnp.zeros((pad_rows, x.shape[1]), x.dtype)], axis=0)

    scores = [score_block(_dot(qi, ki_refs[p][...].astype(BF16))) for p in range(n_pages)]
    s_new = score_block(_nt_dot(qi, pad_new(tailn_ref[...][:, :IDX_DIM]).astype(BF16)))
    scores.append(jnp.where(lane <= trow, s_new, NEG_INF))
    idxs = [lane + b * PAGE_SIZE for b in range(n_blocks)]

    def count_where(pred):
        acc = jnp.zeros(shape, F32)
        for s, idx in zip(scores, idxs):
            acc = acc + pred(s, idx)
        return _lane_rep(jnp.sum(acc, axis=1, keepdims=True))

    thr, nge = _kth_largest(lambda cf: count_where(lambda s, idx: jnp.where(s >= cf, 1.0, 0.0)), k_sel, shape,
                            digit_bits=4)
    lim_ref[...] = jnp.full(shape, 1 << idx_bits, I32)

    @pl.when(jnp.max(nge) > k_sel)
    def _():
        n_gt = count_where(lambda s, idx: jnp.where(s > thr, 1.0, 0.0))
        need = k_sel - n_gt
        lim_ref[...] = _tie_limit(
            lambda cand: count_where(lambda s, idx: jnp.where(s == thr, jnp.where(idx < cand, 1.0, 0.0), 0.0)),
            need, shape, idx_bits)

    lim = lim_ref[...]
    masks = [_select_mask(s, idx, thr, lim) for s, idx in zip(scores, idxs)]

    rows = N_HEADS * t_rows
    same_head = (lax.broadcasted_iota(I32, (rows, D_ATT), 0) // t_rows
                 == lax.broadcasted_iota(I32, (rows, D_ATT), 1) // HEAD_DIM)
    q_bd = jnp.where(same_head, jnp.concatenate([q_ref[...]] * N_HEADS, axis=0), 0.0).astype(BF16)

    def tile_heads(m):
        return jnp.concatenate([m] * N_HEADS, axis=0)

    logits = []
    for p in range(n_pages):
        bias = blast_ref[...] if p == n_pages - 1 else bfar_ref[...]
        logits.append(_dot(q_bd, k_refs[p][...].astype(BF16)) + bias + tile_heads(masks[p]))
    logits.append(_nt_dot(q_bd, pad_new(kn_ref[...]).astype(BF16)) + bnew_ref[...] + tile_heads(masks[n_pages]))

    m = logits[0]
    for lg in logits[1:]:
        m = jnp.maximum(m, lg)
    m = _lane_rep(jnp.max(m, axis=1, keepdims=True))
    lsum = jnp.zeros((rows, LANES), F32)
    acc = jnp.zeros((rows, D_ATT), F32)
    for b in range(n_blocks):
        p = jnp.exp(logits[b] - m)
        lsum = lsum + p
        if b < n_pages:
            acc = acc + _nt_dot(p.astype(BF16), v_refs[b][...].astype(BF16))
        else:
            acc = acc + _dot(p.astype(BF16), pad_new(vn_ref[...]).astype(BF16))
    out = jnp.where(same_head, acc / jnp.sum(lsum, axis=1, keepdims=True), 0.0)
    o_ref[...] = _fold_tiles(out)


def _attn_sample(cfg, layer, k_sel, page_table, qi_ht, wrep, q, tail_new, k_new, v_new,
                 s_far, s_last, s_new, cache_ki, cache_k, cache_v):
    b = cfg.dec_batch
    t_rows = cfg.dec_seq
    n_pages = cfg.past_len // PAGE_SIZE
    rows = N_HEADS * t_rows
    per_b = lambda r, w: pl.BlockSpec((None, r, w), lambda i, pt: (i, 0, 0))
    const = lambda r, w: pl.BlockSpec((r, w), lambda i, pt: (0, 0))

    def page_spec(width, p):
        return pl.BlockSpec((None, None, width, PAGE_SIZE), lambda i, pt: (layer, pt[i, p], 0, 0))

    in_specs = ([per_b(rows, IDX_DIM), per_b(rows, LANES), per_b(t_rows, D_ATT), per_b(t_rows, TAIL),
                 per_b(t_rows, D_ATT), per_b(t_rows, D_ATT),
                 const(rows, LANES), const(rows, LANES), const(rows, LANES)]
                + [page_spec(IDX_DIM, p) for p in range(n_pages)]
                + [page_spec(D_ATT, p) for p in range(n_pages)]
                + [page_spec(D_ATT, p) for p in range(n_pages)])
    grid_spec = pltpu.PrefetchScalarGridSpec(
        num_scalar_prefetch=1,
        grid=(b,),
        in_specs=in_specs,
        out_specs=pl.BlockSpec((None, t_rows, D_ATT), lambda i, pt: (i, 0, 0)),
        scratch_shapes=[pltpu.VMEM((t_rows, LANES), I32)])
    return pl.pallas_call(
        functools.partial(_attn_sample_kernel, k_sel, n_pages),
        grid_spec=grid_spec,
        out_shape=jax.ShapeDtypeStruct((b, t_rows, D_ATT), F32),
        compiler_params=pltpu.CompilerParams(vmem_limit_bytes=VMEM_LIMIT),
        name="attn_sample",
    )(page_table, qi_ht, wrep, q, tail_new, k_new, v_new, s_far, s_last, s_new,
      *([cache_ki] * n_pages), *([cache_k] * n_pages), *([cache_v] * n_pages))


def _forward(cfg, x_prompt, x_sample, cache_k, cache_v, cache_kidx, state_pool, page_table, c_prompt,
             c_sample, rel_bias, ada_w, ada_b, ln1_pre, ln1_post, ln2_pre, ln2_post, w_in, pool_w,
             pool_scale, w_out, w_ff1, w_ff2):
    depth = ada_w.shape[0]
    bp, seq, bs, ts = cfg.batch, cfg.seq, cfg.dec_batch, cfg.dec_seq
    assert ts == SUBLANES and seq % cfg.blk == 0 and cfg.past_len % PAGE_SIZE == 0
    assert cfg.blk >= PAGE_SIZE and seq >= TOPK_MAX and seq // BF16_ROWS <= 256
    k_prompt = min(TOPK_MAX, seq // 4)
    k_sample = min(TOPK_MAX, (cfg.past_len + ts) // 4)
    rp, rs = bp * seq, bs * ts
    tm_s = min(cfg.tm, rs)

    mod = _ada_mod(jnp.concatenate([c_prompt, c_sample], axis=0), ada_w, ada_b)
    t0, t1, t2, s_far, s_last, s_new = _bias_tables(rel_bias, cfg.blk)

    n_phys = cache_k.shape[1]
    cache_kt = cache_k.transpose(0, 1, 3, 4, 2).reshape(depth, n_phys, D_ATT, PAGE_SIZE)
    cache_vt = cache_v.transpose(0, 1, 3, 4, 2).reshape(depth, n_phys, D_ATT, PAGE_SIZE)
    cache_kit = cache_kidx.transpose(0, 1, 3, 2)

    xp = x_prompt.reshape(rp, D_MODEL)
    xs = x_sample.reshape(rs, D_MODEL)
    outs_p = [[], [], [], []]
    outs_s = [[], [], [], []]
    for l in range(depth):
        w_in_b = jnp.pad(w_in[l], ((0, 0), (0, D_IN_PAD - D_IN))).astype(BF16)
        wo_b, w1_b, w2_b = w_out[l].astype(BF16), w_ff1[l].astype(BF16), w_ff2[l].astype(BF16)
        lnrow = lambda a: a[l].reshape(1, D_MODEL)
        mod_p = mod[l, :bp].reshape(bp, 1, 6 * D_MODEL)
        mod_s = jnp.repeat(mod[l, bp:], ts, axis=0)
        psc = pool_scale[l].reshape(1, C_POOL)

        u, kb, tailb, kt, vt, kit, wit, qt, qit, vbt = _in_proj(xp, mod_p, seq, lnrow(ln1_pre), w_in_b, cfg.tm)
        u3 = u.reshape(bp, seq, C_POOL)
        y_pool = _pool_mix(jnp.pad(u3, ((0, 0), (POOL_PAD, 0), (0, 0))), 0, pool_w[l], psc, 1,
                           min(256, seq)).reshape(rp, C_POOL)
        y_att = _attn_prompt(cfg, k_prompt, qt, qit, wit, kb, tailb, vbt, t0, t1, t2)
        xp = _out_ffn(xp, y_pool, y_att, mod_p, seq, lnrow(ln1_post), lnrow(ln2_pre), lnrow(ln2_post),
                      wo_b, w1_b, w2_b, cfg.tm)
        heads_last = lambda a: a.reshape(bp, N_HEADS, HEAD_DIM, seq).transpose(0, 3, 1, 2)
        for lst, a in zip(outs_p, (heads_last(kt), heads_last(vt), kit.transpose(0, 2, 1),
                                   u3[:, seq - POOL_HIST:])):
            lst.append(a)

        u, q, qi, tail, k, v = _in_proj(xs, mod_s, 0, lnrow(ln1_pre), w_in_b, tm_s)
        u3 = u.reshape(bs, ts, C_POOL)
        u_ext = jnp.concatenate([jnp.zeros((bs, POOL_PAD - POOL_HIST, C_POOL), F32), state_pool[l], u3], axis=1)
        y_pool = _pool_mix(u_ext, cfg.past_len, pool_w[l], psc, min(16, bs), ts).reshape(rs, C_POOL)
        qi_ht = qi.reshape(bs, ts, IDX_HEADS, IDX_DIM).transpose(0, 2, 1, 3).reshape(bs, IDX_HEADS * ts, IDX_DIM)
        wi = tail[:, OFF_WI - OFF_KI:OFF_WI - OFF_KI + IDX_HEADS].reshape(bs, ts, IDX_HEADS) * IDX_HEADS ** -0.5
        wrep = jnp.broadcast_to(wi.transpose(0, 2, 1).reshape(bs, IDX_HEADS * ts, 1), (bs, IDX_HEADS * ts, LANES))
        y_att = _attn_sample(cfg, l, k_sample, page_table, qi_ht, wrep,
                             q.astype(F32).reshape(bs, ts, D_ATT), tail.reshape(bs, ts, TAIL),
                             k.reshape(bs, ts, D_ATT), v.reshape(bs, ts, D_ATT),
                             s_far, s_last, s_new, cache_kit, cache_kt, cache_vt).reshape(rs, D_ATT)
        xs = _out_ffn(xs, y_pool, y_att, mod_s, 0, lnrow(ln1_post), lnrow(ln2_pre), lnrow(ln2_post),
                      wo_b, w1_b, w2_b, tm_s)
        for lst, a in zip(outs_s, (k.reshape(bs, ts, N_HEADS, HEAD_DIM), v.reshape(bs, ts, N_HEADS, HEAD_DIM),
                                   tail[:, :IDX_DIM].reshape(bs, ts, IDX_DIM), u_ext[:, -POOL_HIST:])):
            lst.append(a)

    stack = lambda lists: [jnp.stack(a) for a in lists]
    return (xp.reshape(bp, seq, D_MODEL), xs.reshape(bs, ts, D_MODEL), *stack(outs_p), *stack(outs_s))


def kernel(x_prompt, x_sample, cache_k, cache_v, cache_kidx, state_pool, page_table, c_prompt, c_sample,
           rel_bias, ada_w, ada_b, ln1_pre, ln1_post, ln2_pre, ln2_post, w_in, pool_w, pool_scale,
           w_out, w_ff1, w_ff2):
    cfg = Cfg(batch=x_prompt.shape[0], seq=x_prompt.shape[1], dec_batch=x_sample.shape[0],
              dec_seq=x_sample.shape[1], past_len=page_table.shape[1] * PAGE_SIZE, blk=256, tm=256)
    return _forward(cfg, x_prompt, x_sample, cache_k, cache_v, cache_kidx, state_pool, page_table, c_prompt,
                    c_sample, rel_bias, ada_w, ada_b, ln1_pre, ln1_post, ln2_pre, ln2_post, w_in, pool_w,
                    pool_scale, w_out, w_ff1, w_ff2)
```

```python
import functools
import math
from typing import NamedTuple

import jax
import jax.numpy as jnp
from jax import lax
from jax.experimental import pallas as pl
from jax.experimental.pallas import tpu as pltpu

F32 = jnp.float32
BF16 = jnp.bfloat16
I32 = jnp.int32

D_MODEL = 1024
C_POOL = 512
D_ATT = 512
HEAD_DIM = 64
N_HEADS = 8
IDX_HEADS = 8
IDX_DIM = 64
POOL_WINDOWS = (2, 4, 8, 16)
POOL_GC = C_POOL // len(POOL_WINDOWS)
POOL_HIST = max(POOL_WINDOWS) - 1
TOPK_MAX = 256
N_BUCKETS = 32
MAX_DISTANCE = 128
D_FF = 4 * D_MODEL
EPS = 1e-6
PAGE_SIZE = 128
OFF_Q = C_POOL
OFF_K = OFF_Q + D_ATT
OFF_V = OFF_K + D_ATT
OFF_QI = OFF_V + D_ATT
OFF_KI = OFF_QI + IDX_HEADS * IDX_DIM
OFF_WI = OFF_KI + IDX_DIM
D_IN = OFF_WI + IDX_HEADS

LANES = 128
SUBLANES = 8
BF16_ROWS = 2 * SUBLANES
D_IN_PAD = ((D_IN + LANES - 1) // LANES) * LANES
TAIL = D_IN_PAD - OFF_KI
POOL_PAD = 16
VMEM_LIMIT = 56 * 1024 * 1024

NEG_INF = float("-inf")
INT_MIN = -(2 ** 31)


class Cfg(NamedTuple):
    batch: int
    seq: int
    dec_batch: int
    dec_seq: int
    past_len: int
    blk: int
    tm: int


def _rms(x, g):
    ms = jnp.mean(x * x, axis=-1, keepdims=True)
    return x * lax.rsqrt(ms + EPS) * g


def _nt_dot(a, b):
    return lax.dot_general(a, b, (((1,), (1,)), ((), ())), preferred_element_type=F32)


def _dot(a, b):
    return jnp.dot(a, b, preferred_element_type=F32)


def _key_to_float(key):
    bits = key ^ ((key >> 31) & jnp.int32(0x7FFFFFFF))
    return lax.bitcast_convert_type(bits, F32)


def _lane_rep(col, width=LANES):
    return jnp.broadcast_to(col, (col.shape[0], width))


def _fold_tiles(x, op=jnp.add):
    acc = x[0:SUBLANES]
    for r in range(1, x.shape[0] // SUBLANES):
        acc = op(acc, x[r * SUBLANES:(r + 1) * SUBLANES])
    return acc


def _ada_kernel(c_ref, w_ref, b_ref, o_ref):
    o_ref[...] = _dot(c_ref[...].astype(BF16), w_ref[...].astype(BF16)) + b_ref[...]


def _ada_mod(c_all, ada_w, ada_b):
    depth, d, n = ada_w.shape
    rc = c_all.shape[0]
    tn = 1536
    return pl.pallas_call(
        _ada_kernel,
        grid=(depth, n // tn),
        in_specs=[pl.BlockSpec((rc, d), lambda l, j: (0, 0)),
                  pl.BlockSpec((None, d, tn), lambda l, j: (l, 0, j)),
                  pl.BlockSpec((None, 1, tn), lambda l, j: (l, 0, j))],
        out_specs=pl.BlockSpec((None, rc, tn), lambda l, j: (l, 0, j)),
        out_shape=jax.ShapeDtypeStruct((depth, rc, n), F32),
        name="ada_mod",
    )(c_all, ada_w, ada_b.reshape(depth, 1, n))


def _mod_spec(per_batch_rows, tm, chunk):
    if per_batch_rows:
        tiles = per_batch_rows // tm
        return pl.BlockSpec((None, 1, D_MODEL), lambda i: (i // tiles, 0, chunk))
    return pl.BlockSpec((tm, D_MODEL), lambda i: (i, chunk))


def _in_kernel(transposed, n_carried, x_ref, sh_ref, sc_ref, ln_ref, w_ref, *refs):
    u_ref, *refs = refs[n_carried:]
    h = _rms(x_ref[...], ln_ref[...]) * (1.0 + sc_ref[...]) + sh_ref[...]
    z = _dot(h.astype(BF16), w_ref[...])
    u_ref[...] = z[:, :OFF_Q]
    q = z[:, OFF_Q:OFF_K] * HEAD_DIM ** -0.5
    k = z[:, OFF_K:OFF_V]
    v = z[:, OFF_V:OFF_QI]
    qi = z[:, OFF_QI:OFF_KI]
    tail = z[:, OFF_KI:]
    if transposed:
        kb_ref, tailb_ref, kt_ref, vt_ref, kit_ref, wit_ref, qt_ref, qit_ref, vbt_ref = refs
        kb_ref[...] = k.astype(BF16)
        tailb_ref[...] = tail.astype(BF16)
        kt_ref[...] = k.T
        vt = v.T
        vt_ref[...] = vt
        vbt_ref[...] = vt.astype(BF16)
        tail_t = tail.T
        kit_ref[...] = tail_t[:IDX_DIM]
        wit_ref[...] = tail_t[OFF_WI - OFF_KI:OFF_WI - OFF_KI + IDX_HEADS]
        qt_ref[...] = q.T.astype(BF16)
        qit_ref[...] = qi.T.astype(BF16)
    else:
        q_ref, qi_ref, tail_ref, k_ref, v_ref = refs
        q_ref[...] = q.astype(BF16)
        qi_ref[...] = qi.astype(BF16)
        tail_ref[...] = tail
        k_ref[...] = k
        v_ref[...] = v


def _in_proj(x, mod, per_batch_rows, ln, w_in_b, tm, layer=0, depth=1, carried=()):
    r = x.shape[0]
    row = lambda width: pl.BlockSpec((tm, width), lambda i: (i, 0))
    const = lambda shape: pl.BlockSpec(shape, lambda i: (0, 0))
    in_specs = [row(D_MODEL), _mod_spec(per_batch_rows, tm, 0), _mod_spec(per_batch_rows, tm, 1),
                const((1, D_MODEL)), const((D_MODEL, D_IN_PAD))]
    aliases = {}
    if per_batch_rows:
        tiles = per_batch_rows // tm
        nb = r // per_batch_rows
        rows_out = [(C_POOL, F32), (D_ATT, BF16), (TAIL, BF16)]
        state_out = [D_ATT, D_ATT, IDX_DIM]
        cols_out = [(IDX_HEADS, F32), (D_ATT, BF16), (IDX_HEADS * IDX_DIM, BF16), (D_ATT, BF16)]
    else:
        rows_out = [(C_POOL, F32), (D_ATT, BF16), (IDX_HEADS * IDX_DIM, BF16), (TAIL, F32), (D_ATT, F32), (D_ATT, F32)]
        state_out, cols_out = [], []
    out_specs = [row(w) for w, _ in rows_out]
    out_shape = [jax.ShapeDtypeStruct((r, w), dt) for w, dt in rows_out]
    for n, width in enumerate(state_out):
        if carried:
            aliases[len(in_specs)] = len(out_specs)
            in_specs.append(pl.BlockSpec(memory_space=pl.ANY))
        out_specs.append(pl.BlockSpec((None, None, width, tm), lambda i: (layer, i // tiles, 0, i % tiles)))
        out_shape.append(jax.ShapeDtypeStruct((depth, nb, width, per_batch_rows), F32))
    for width, dt in cols_out:
        out_specs.append(pl.BlockSpec((None, width, tm), lambda i: (i // tiles, 0, i % tiles)))
        out_shape.append(jax.ShapeDtypeStruct((nb, width, per_batch_rows), dt))
    return pl.pallas_call(
        functools.partial(_in_kernel, bool(per_batch_rows), len(carried)),
        grid=(r // tm,),
        in_specs=in_specs,
        out_specs=out_specs,
        out_shape=out_shape,
        input_output_aliases=aliases,
        compiler_params=pltpu.CompilerParams(vmem_limit_bytes=VMEM_LIMIT),
        name="in_proj",
    )(x, mod, mod, ln, w_in_b, *carried)


def _out_kernel(x_ref, yp_ref, ya_ref, g1_ref, sh2_ref, sc2_ref, g2_ref,
                ln1_ref, ln2a_ref, ln2b_ref, wo_ref, w1_ref, w2_ref, o_ref):
    mixed = (_dot(yp_ref[...].astype(BF16), wo_ref[:C_POOL, :])
             + _dot(ya_ref[...].astype(BF16), wo_ref[C_POOL:, :]))
    x1 = x_ref[...] + g1_ref[...] * _rms(mixed, ln1_ref[...])
    h = (_rms(x1, ln2a_ref[...]) * (1.0 + sc2_ref[...]) + sh2_ref[...]).astype(BF16)
    f = jnp.zeros(x1.shape, F32)
    for c in range(D_FF // D_MODEL):
        sl = slice(c * D_MODEL, (c + 1) * D_MODEL)
        a = jnp.maximum(_dot(h, w1_ref[:, sl]), 0.0)
        f = f + _dot((a * a).astype(BF16), w2_ref[sl, :])
    o_ref[...] = x1 + g2_ref[...] * _rms(f, ln2b_ref[...])


def _out_ffn(x, yp, ya, mod, per_batch_rows, ln1_post, ln2_pre, ln2_post, wo_b, w1_b, w2_b, tm):
    r = x.shape[0]
    row = lambda width: pl.BlockSpec((tm, width), lambda i: (i, 0))
    const = lambda shape: pl.BlockSpec(shape, lambda i: (0, 0), pipeline_mode=pl.Buffered(1))
    ms = lambda chunk: _mod_spec(per_batch_rows, tm, chunk)
    return pl.pallas_call(
        _out_kernel,
        grid=(r // tm,),
        in_specs=[row(D_MODEL), row(C_POOL), row(D_ATT), ms(2), ms(3), ms(4), ms(5),
                  const((1, D_MODEL)), const((1, D_MODEL)), const((1, D_MODEL)),
                  const((D_MODEL, D_MODEL)), const((D_MODEL, D_FF)), const((D_FF, D_MODEL))],
        out_specs=row(D_MODEL),
        out_shape=jax.ShapeDtypeStruct((r, D_MODEL), F32),
        compiler_params=pltpu.CompilerParams(vmem_limit_bytes=VMEM_LIMIT),
        name="out_ffn",
    )(x, yp, ya, mod, mod, mod, mod, ln1_post, ln2_pre, ln2_post, wo_b, w1_b, w2_b)


def _pool_kernel(pos0, t_rows, chunk, has_hist, u_ref, *refs):
    if has_hist:
        hist_ref, pw_ref, ps_ref, o_ref, ue_ref = refs
        ue_ref[:, :POOL_PAD, :] = hist_ref[...]
    else:
        pw_ref, ps_ref, o_ref, ue_ref = refs
        ue_ref[:, :POOL_PAD, :] = jnp.zeros((ue_ref.shape[0], POOL_PAD, C_POOL), F32)
    ue_ref[:, POOL_PAD:, :] = u_ref[...]
    g_elems = ue_ref.shape[0]
    n_chunks = t_rows // chunk
    for ci in range(n_chunks):
        r0 = ci * chunk
        pos = pos0 + r0 + lax.broadcasted_iota(I32, (chunk, POOL_GC), 0)
        for g, w in enumerate(POOL_WINDOWS):
            lanes = pl.ds(g * POOL_GC, POOL_GC)
            cnt = jnp.minimum(pos + 1, w).astype(F32)
            pooled = []
            for e in range(g_elems):
                cur = ue_ref[e, pl.ds(POOL_PAD + r0, chunk), lanes]
                acc = cur
                for j in range(1, w):
                    acc = acc + ue_ref[e, pl.ds(POOL_PAD + r0 - j, chunk), lanes]
                pooled.append(acc / cnt - cur)
            pooled = pooled[0] if g_elems == 1 else jnp.concatenate(pooled, axis=0)
            y = _dot(pooled.astype(BF16), pw_ref[g].astype(BF16))
            y = y * ps_ref[:, lanes]
            for e in range(g_elems):
                o_ref[e, pl.ds(r0, chunk), lanes] = y[e * chunk:(e + 1) * chunk].astype(o_ref.dtype)


def _pool_mix(u, hist, pos0, pool_w_l, pool_scale_l, g_elems, chunk):
    b, t_rows, _ = u.shape
    per_elem = lambda rows: pl.BlockSpec((g_elems, rows, C_POOL), lambda i: (i, 0, 0))
    in_specs = [per_elem(t_rows)] + ([per_elem(POOL_PAD)] if hist is not None else []) + [
        pl.BlockSpec((len(POOL_WINDOWS), POOL_GC, POOL_GC), lambda i: (0, 0, 0)),
        pl.BlockSpec((1, C_POOL), lambda i: (0, 0))]
    args = (u,) + ((hist,) if hist is not None else ()) + (pool_w_l, pool_scale_l)
    return pl.pallas_call(
        functools.partial(_pool_kernel, pos0, t_rows, chunk, hist is not None),
        grid=(b // g_elems,),
        in_specs=in_specs,
        out_specs=per_elem(t_rows),
        out_shape=jax.ShapeDtypeStruct((b, t_rows, C_POOL), F32),
        scratch_shapes=[pltpu.VMEM((g_elems, POOL_PAD + t_rows, C_POOL), F32)],
        compiler_params=pltpu.CompilerParams(vmem_limit_bytes=VMEM_LIMIT),
        name="pool_mix",
    )(*args)


def _bias_of_dist(dist, rb_ref, h):
    n = jnp.maximum(dist, 0)
    max_exact = N_BUCKETS // 2
    large = max_exact + (jnp.log(jnp.maximum(n, 1).astype(F32) / max_exact)
                         / math.log(MAX_DISTANCE / max_exact) * (N_BUCKETS - max_exact)).astype(I32)
    large = jnp.minimum(large, N_BUCKETS - 1)
    bucket = jnp.where(n < max_exact, n, large)
    out = jnp.zeros(dist.shape, F32)
    for b in range(N_BUCKETS):
        out = jnp.where(bucket == b, rb_ref[b, h], out)
    return out


def _bias_kernel(blk, rb_ref, tb_ref, s_far_ref, s_last_ref, s_new_ref):
    key = lax.broadcasted_iota(I32, (blk, blk), 0)
    qry = lax.broadcasted_iota(I32, (blk, blk), 1)
    t = lax.broadcasted_iota(I32, (SUBLANES, LANES), 0)
    jj = lax.broadcasted_iota(I32, (SUBLANES, LANES), 1)
    for h in range(N_HEADS):
        tb_ref[0, h] = _bias_of_dist(jnp.full((blk, blk), 2 * blk, I32), rb_ref, h)
        tb_ref[1, h] = _bias_of_dist(blk + qry - key, rb_ref, h)
        tb_ref[2, h] = _bias_of_dist(qry - key, rb_ref, h)
        rows = pl.ds(h * SUBLANES, SUBLANES)
        s_far_ref[rows, :] = _bias_of_dist(jnp.full((SUBLANES, LANES), 2 * PAGE_SIZE, I32), rb_ref, h)
        s_last_ref[rows, :] = _bias_of_dist(PAGE_SIZE + t - jj, rb_ref, h)
        s_new_ref[rows, :] = _bias_of_dist(t - jj, rb_ref, h)


def _bias_tables(rel_bias, blk):
    sm = jax.ShapeDtypeStruct((N_HEADS * SUBLANES, LANES), F32)
    return pl.pallas_call(
        functools.partial(_bias_kernel, blk),
        in_specs=[pl.BlockSpec(memory_space=pltpu.SMEM)],
        out_shape=[jax.ShapeDtypeStruct((3, N_HEADS, blk, blk), F32), sm, sm, sm],
        compiler_params=pltpu.CompilerParams(vmem_limit_bytes=VMEM_LIMIT),
        name="bias_tables",
    )(rel_bias)


def _kth_start(shape):
    return jnp.full(shape, INT_MIN, I32), jnp.zeros(shape, F32)


def _kth_rounds(count_ge_key, k, carry, first_round, last_round, digit_bits=1):
    def body(r, carry):
        tkey, nge = carry
        shift = 32 - digit_bits * (r + 1)
        best_key, best_cnt = tkey, nge
        for j in range(1, 1 << digit_bits):
            ckey = tkey + lax.shift_left(jnp.int32(j), shift)
            cnt = count_ge_key(ckey)
            ok = cnt >= k
            best_key = jnp.where(ok, ckey, best_key)
            best_cnt = jnp.where(ok, cnt, best_cnt)
        return best_key, best_cnt

    return lax.fori_loop(first_round, last_round, body, carry)


def _kth_finish(carry):
    tkey, nge = carry
    return jnp.where(tkey == INT_MIN, NEG_INF, _key_to_float(tkey)), nge


def _kth_largest(count_ge, k, shape, digit_bits=1):
    carry = _kth_rounds(lambda ckey: count_ge(_key_to_float(ckey)), k, _kth_start(shape),
                        0, 32 // digit_bits, digit_bits)
    return _kth_finish(carry)


def _tie_limit(count_eq_below, need, shape, bits):
    def body(it, lim):
        cand = lim + lax.shift_left(jnp.int32(1), bits - 1 - it)
        return jnp.where(count_eq_below(cand) <= need, cand, lim)

    return lax.fori_loop(0, bits, body, jnp.zeros(shape, I32))


def _select_mask(s, idx, thr, lim):
    tie = jnp.where(s == thr, jnp.where(idx < lim, 0.0, NEG_INF), NEG_INF)
    m = jnp.where(s > thr, 0.0, tie)
    return jnp.where(s == NEG_INF, NEG_INF, m)


def _attn_prompt_kernel(k_sel, blk, seq,
                        qt_ref, qit_ref, wit_ref, kb_ref, kib_ref, vbt_ref, tb_ref,
                        o_ref,
                        score_ref, trunc_ref, lg_ref, mrun_ref, lsum_ref, acc_ref, thr_ref, lim_ref):
    qb = pl.program_id(1)
    n_chunks = qb + 1
    rep = (SUBLANES, blk)
    idx_bits = max(1, (seq - 1).bit_length()) + 1

    key_row = lax.broadcasted_iota(I32, (blk, blk), 0)
    qry_col = lax.broadcasted_iota(I32, (blk, blk), 1)

    def chunk_keys(c):
        return pl.ds(pl.multiple_of(c * blk, blk), blk)

    def all_sublanes(x, op):
        return jnp.broadcast_to(op(x, axis=0, keepdims=True), rep)

    w_idx = wit_ref[...] * IDX_HEADS ** -0.5 * IDX_DIM ** -0.5

    def score_chunk(c, carry):
        keys = chunk_keys(c)
        kic = kib_ref[keys, :][:, :IDX_DIM]
        s = jnp.zeros((blk, blk), F32)
        for h in range(IDX_HEADS):
            d = _dot(kic, qit_ref[h * IDX_DIM:(h + 1) * IDX_DIM, :])
            s = s + w_idx[h:h + 1, :] * jnp.maximum(d, 0.0)
        s = jnp.where(key_row + c * blk <= qry_col + qb * blk, s, NEG_INF)
        score_ref[keys, :] = s
        hi = lax.bitcast_convert_type(s, I32) & jnp.int32(-(1 << 16))
        trunc_ref[keys, :] = lax.bitcast_convert_type(hi, F32).astype(BF16)
        return carry

    lax.fori_loop(0, n_chunks, score_chunk, 0)

    def count_where(pred):
        def body(c, acc):
            return acc + _fold_tiles(pred(score_ref[chunk_keys(c), :], key_row + c * blk))
        acc = lax.fori_loop(0, n_chunks, body, jnp.zeros(rep, F32))
        return all_sublanes(acc, jnp.sum)

    def count_upper_half(ckey):
        bits = ckey ^ ((ckey >> 31) & jnp.int32(0x7FFFFFFF))
        cf = lax.bitcast_convert_type(bits & jnp.int32(-(1 << 16)), F32)
        cb = jnp.concatenate([cf, cf], axis=0).astype(BF16)
        one, zero = jnp.ones((BF16_ROWS, blk), BF16), jnp.zeros((BF16_ROWS, blk), BF16)

        def body(c, acc):
            t = trunc_ref[chunk_keys(c), :]
            for r in range(blk // BF16_ROWS):
                acc = acc + jnp.where(t[r * BF16_ROWS:(r + 1) * BF16_ROWS] >= cb, one, zero)
            return acc
        acc = lax.fori_loop(0, n_chunks, body, zero)
        return all_sublanes(acc.astype(F32), jnp.sum)

    carry = _kth_rounds(count_upper_half, k_sel, _kth_start(rep), 0, 16)
    carry = _kth_rounds(
        lambda ckey: count_where(lambda s, idx: jnp.where(s >= _key_to_float(ckey)[0:1], 1.0, 0.0)),
        k_sel, carry, 16, 32)
    thr, nge = _kth_finish(carry)
    thr_ref[...] = thr
    lim_ref[...] = jnp.full(rep, 1 << idx_bits, I32)

    @pl.when(jnp.max(nge) > k_sel)
    def _():
        n_gt = count_where(lambda s, idx: jnp.where(s > thr[0:1], 1.0, 0.0))
        need = k_sel - n_gt
        lim_ref[...] = _tie_limit(
            lambda cand: count_where(
                lambda s, idx: jnp.where(s == thr[0:1], jnp.where(idx < cand[0:1], 1.0, 0.0), 0.0)),
            need, rep, idx_bits)

    def mask_chunk(c, carry):
        keys = chunk_keys(c)
        score_ref[keys, :] = _select_mask(score_ref[keys, :], key_row + c * blk,
                                          thr_ref[0:1, :], lim_ref[0:1, :])
        return carry

    lax.fori_loop(0, n_chunks, mask_chunk, 0)

    mrun_ref[...] = jnp.full(mrun_ref.shape, NEG_INF, F32)

    def logits_chunk(c, carry):
        keys = chunk_keys(c)
        madd = score_ref[keys, :]
        table = jnp.clip(c - (qb - 2), 0, 2)
        for h in range(N_HEADS):
            hs = slice(h * HEAD_DIM, (h + 1) * HEAD_DIM)
            s = _dot(kb_ref[keys, hs], qt_ref[hs, :]) + tb_ref[table, h] + madd
            lg_ref[h, keys, :] = s
            mrun_ref[h] = jnp.maximum(mrun_ref[h], _fold_tiles(s, jnp.maximum))
        return carry

    lax.fori_loop(0, n_chunks, logits_chunk, 0)

    for h in range(N_HEADS):
        mrun_ref[h] = all_sublanes(mrun_ref[h], jnp.max)
    lsum_ref[...] = jnp.zeros(lsum_ref.shape, F32)
    acc_ref[...] = jnp.zeros(acc_ref.shape, F32)

    def pv_chunk(c, carry):
        keys = chunk_keys(c)
        for h in range(N_HEADS):
            hs = slice(h * HEAD_DIM, (h + 1) * HEAD_DIM)
            p = jnp.exp(lg_ref[h, keys, :] - mrun_ref[h, 0:1, :])
            lsum_ref[h] += _fold_tiles(p)
            acc_ref[h] += _dot(vbt_ref[hs, keys], p.astype(BF16))
        return carry

    lax.fori_loop(0, n_chunks, pv_chunk, 0)

    outs =[acc_ref[h] / jnp.sum(lsum_ref[h], axis=0, keepdims=True) for h in range(N_HEADS)]
    o_ref[...] = jnp.concatenate(outs, axis=0).T.astype(o_ref.dtype)


def _attn_prompt(cfg, k_sel, qt, qit, wit, kb, kib, vbt, tb):
    blk, seq = cfg.blk, cfg.seq
    nq = seq // blk
    qspec = lambda width: pl.BlockSpec((None, width, blk), lambda b, i: (b, 0, i))
    kspec = lambda width: pl.BlockSpec((seq, width), lambda b, i: (b, 0))
    return pl.pallas_call(
        functools.partial(_attn_prompt_kernel, k_sel, blk, seq),
        grid=(cfg.batch, nq),
        in_specs=[qspec(D_ATT), qspec(IDX_HEADS * IDX_DIM), qspec(IDX_HEADS),
                  kspec(D_ATT), kspec(TAIL), pl.BlockSpec((None, D_ATT, seq), lambda b, i: (b, 0, 0)),
                  pl.BlockSpec((3, N_HEADS, blk, blk), lambda b, i: (0, 0, 0, 0), pipeline_mode=pl.Buffered(1))],
        out_specs=pl.BlockSpec((blk, D_ATT), lambda b, i: (b * nq + i, 0)),
        out_shape=jax.ShapeDtypeStruct((cfg.batch * seq, D_ATT), BF16),
        scratch_shapes=[pltpu.VMEM((seq, blk), F32),
                        pltpu.VMEM((seq, blk), BF16),
                        pltpu.VMEM((N_HEADS, seq, blk), F32),
                        pltpu.VMEM((N_HEADS, SUBLANES, blk), F32),
                        pltpu.VMEM((N_HEADS, SUBLANES, blk), F32),
                        pltpu.VMEM((N_HEADS, HEAD_DIM, blk), F32),
                        pltpu.VMEM((SUBLANES, blk), F32),
                        pltpu.VMEM((SUBLANES, blk), I32)],
        compiler_params=pltpu.CompilerParams(vmem_limit_bytes=VMEM_LIMIT),
        name="attn_prompt",
    )(qt, qit, wit, kb, kib, vbt, tb)


def _pad_rows(x, rows):
    return jnp.concatenate([x, jnp.zeros((rows - x.shape[0], x.shape[1]), x.dtype)], axis=0)


def _score_sample_kernel(n_pages, pt_ref, qi_ref, wrep_ref, tailn_ref, *rest):
    ki_refs, o_ref = rest[:n_pages], rest[n_pages]
    shape = (SUBLANES, LANES)
    qi = qi_ref[...]
    w = wrep_ref[...]

    def score_block(dots):
        return _fold_tiles(w * jnp.maximum(dots * IDX_DIM ** -0.5, 0.0))

    for p in range(n_pages):
        o_ref[:, p * PAGE_SIZE:(p + 1) * PAGE_SIZE] = score_block(_dot(qi, ki_refs[p][...].astype(BF16)))
    ki_new = _pad_rows(tailn_ref[...][:, :IDX_DIM], PAGE_SIZE).astype(BF16)
    causal = lax.broadcasted_iota(I32, shape, 1) <= lax.broadcasted_iota(I32, shape, 0)
    o_ref[:, n_pages * PAGE_SIZE:] = jnp.where(causal, score_block(_nt_dot(qi, ki_new)), NEG_INF)


def _score_sample(cfg, layer, page_table, qi_ht, wrep, tail_new, cache_ki):
    b, t_rows = cfg.dec_batch, cfg.dec_seq
    n_pages = cfg.past_len // PAGE_SIZE
    rows = N_HEADS * t_rows
    per_b = lambda r, w: pl.BlockSpec((None, r, w), lambda i, pt: (i, 0, 0))
    pages = [pl.BlockSpec((None, None, IDX_DIM, PAGE_SIZE), lambda i, pt, p=p: (layer, pt[i, p], 0, 0))
             for p in range(n_pages)]
    n_keys = (n_pages + 1) * PAGE_SIZE
    return pl.pallas_call(
        functools.partial(_score_sample_kernel, n_pages),
        grid_spec=pltpu.PrefetchScalarGridSpec(
            num_scalar_prefetch=1, grid=(b,),
            in_specs=[per_b(rows, IDX_DIM), per_b(rows, LANES), per_b(t_rows, TAIL)] + pages,
            out_specs=per_b(t_rows, n_keys)),
        out_shape=jax.ShapeDtypeStruct((b, t_rows, n_keys), F32),
        name="score_sample",
    )(page_table, qi_ht, wrep, tail_new, *([cache_ki] * n_pages))


def _threshold_kernel(k_sel, s_ref, thr_ref, lim_ref):
    n_keys, cols = s_ref.shape
    rep = (SUBLANES, cols)
    idx_bits = (n_keys - 1).bit_length() + 1
    key_row = lax.broadcasted_iota(I32, (LANES, cols), 0)

    def count_where(pred):
        def body(c, acc):
            keys = pl.ds(pl.multiple_of(c * LANES, LANES), LANES)
            return acc + _fold_tiles(pred(s_ref[keys, :], key_row + c * LANES))
        acc = lax.fori_loop(0, n_keys // LANES, body, jnp.zeros(rep, F32))
        return jnp.broadcast_to(jnp.sum(acc, axis=0, keepdims=True), rep)

    thr, nge = _kth_largest(lambda cf: count_where(lambda s, idx: jnp.where(s >= cf[0:1], 1.0, 0.0)), k_sel, rep)
    thr_ref[...] = thr
    lim_ref[...] = jnp.full(rep, 1 << idx_bits, I32)

    @pl.when(jnp.max(nge) > k_sel)
    def _():
        n_gt = count_where(lambda s, idx: jnp.where(s > thr[0:1], 1.0, 0.0))
        need = k_sel - n_gt
        lim_ref[...] = _tie_limit(
            lambda cand: count_where(
                lambda s, idx: jnp.where(s == thr[0:1], jnp.where(idx < cand[0:1], 1.0, 0.0), 0.0)),
            need, rep, idx_bits)


def _threshold_cols(k_sel, scores_t, cols):
    n_keys, n_q = scores_t.shape
    spec = pl.BlockSpec((SUBLANES, cols), lambda i: (0, i))
    return pl.pallas_call(
        functools.partial(_threshold_kernel, k_sel),
        grid=(n_q // cols,),
        in_specs=[pl.BlockSpec((n_keys, cols), lambda i: (0, i))],
        out_specs=[spec, spec],
        out_shape=[jax.ShapeDtypeStruct((SUBLANES, n_q), F32), jax.ShapeDtypeStruct((SUBLANES, n_q), I32)],
        name="threshold_cols",
    )(scores_t)


def _attn_sample_kernel(n_pages, pt_ref,
                        sc_ref, thr_ref, lim_ref, q_ref, kn_ref, vn_ref, bfar_ref, blast_ref, bnew_ref,
                        *rest):
    k_refs = rest[:n_pages]
    v_refs = rest[n_pages:2 * n_pages]
    o_ref = rest[2 * n_pages]
    t_rows = SUBLANES
    n_blocks = n_pages + 1
    lane = lax.broadcasted_iota(I32, (t_rows, LANES), 1)
    thr, lim = thr_ref[...], lim_ref[...]
    masks = [_select_mask(sc_ref[:, b * PAGE_SIZE:(b + 1) * PAGE_SIZE], lane + b * PAGE_SIZE, thr, lim)
             for b in range(n_blocks)]

    def pad_new(x):
        return _pad_rows(x, PAGE_SIZE)

    rows = N_HEADS * t_rows
    same_head = (lax.broadcasted_iota(I32, (rows, D_ATT), 0) // t_rows
                 == lax.broadcasted_iota(I32, (rows, D_ATT), 1) // HEAD_DIM)
    q_bd = jnp.where(same_head, jnp.concatenate([q_ref[...]] * N_HEADS, axis=0), 0.0).astype(BF16)

    def tile_heads(m):
        return jnp.concatenate([m] * N_HEADS, axis=0)

    logits = []
    for p in range(n_pages):
        bias = blast_ref[...] if p == n_pages - 1 else bfar_ref[...]
        logits.append(_dot(q_bd, k_refs[p][...].astype(BF16)) + bias + tile_heads(masks[p]))
    logits.append(_nt_dot(q_bd, pad_new(kn_ref[...]).astype(BF16)) + bnew_ref[...] + tile_heads(masks[n_pages]))

    m = logits[0]
    for lg in logits[1:]:
        m = jnp.maximum(m, lg)
    m = _lane_rep(jnp.max(m, axis=1, keepdims=True))
    lsum = jnp.zeros((rows, LANES), F32)
    acc = jnp.zeros((rows, D_ATT), F32)
    for b in range(n_blocks):
        p = jnp.exp(logits[b] - m)
        lsum = lsum + p
        if b < n_pages:
            acc = acc + _nt_dot(p.astype(BF16), v_refs[b][...].astype(BF16))
        else:
            acc = acc + _dot(p.astype(BF16), pad_new(vn_ref[...]).astype(BF16))
    out = jnp.where(same_head, acc / jnp.sum(lsum, axis=1, keepdims=True), 0.0)
    o_ref[...] = _fold_tiles(out)


def _attn_sample(cfg, layer, page_table, scores, thr_rep, lim_rep, q, k_new, v_new,
                 s_far, s_last, s_new, cache_k, cache_v):
    b = cfg.dec_batch
    t_rows = cfg.dec_seq
    n_pages = cfg.past_len // PAGE_SIZE
    rows = N_HEADS * t_rows
    per_b = lambda r, w: pl.BlockSpec((None, r, w), lambda i, pt: (i, 0, 0))
    const = lambda r, w: pl.BlockSpec((r, w), lambda i, pt: (0, 0))
    pages = [pl.BlockSpec((None, None, D_ATT, PAGE_SIZE), lambda i, pt, p=p: (layer, pt[i, p], 0, 0))
             for p in range(n_pages)]
    in_specs = ([per_b(t_rows, scores.shape[2]), per_b(t_rows, LANES), per_b(t_rows, LANES),
                 per_b(t_rows, D_ATT), per_b(t_rows, D_ATT), per_b(t_rows, D_ATT),
                 const(rows, LANES), const(rows, LANES), const(rows, LANES)] + pages + pages)
    return pl.pallas_call(
        functools.partial(_attn_sample_kernel, n_pages),
        grid_spec=pltpu.PrefetchScalarGridSpec(
            num_scalar_prefetch=1, grid=(b,), in_specs=in_specs, out_specs=per_b(t_rows, D_ATT)),
        out_shape=jax.ShapeDtypeStruct((b, t_rows, D_ATT), F32),
        compiler_params=pltpu.CompilerParams(vmem_limit_bytes=VMEM_LIMIT),
        name="attn_sample",
    )(page_table, scores, thr_rep, lim_rep, q, k_new, v_new, s_far, s_last, s_new,
      *([cache_k] * n_pages), *([cache_v] * n_pages))


def _forward(cfg, x_prompt, x_sample, cache_k, cache_v, cache_kidx, state_pool, page_table, c_prompt,
             c_sample, rel_bias, ada_w, ada_b, ln1_pre, ln1_post, ln2_pre, ln2_post, w_in, pool_w,
             pool_scale, w_out, w_ff1, w_ff2):
    depth = ada_w.shape[0]
    bp, seq, bs, ts = cfg.batch, cfg.seq, cfg.dec_batch, cfg.dec_seq
    assert ts == SUBLANES and seq % cfg.blk == 0 and cfg.past_len % PAGE_SIZE == 0
    assert cfg.blk >= PAGE_SIZE and seq >= TOPK_MAX and seq // BF16_ROWS <= 256
    k_prompt = min(TOPK_MAX, seq // 4)
    k_sample = min(TOPK_MAX, (cfg.past_len + ts) // 4)
    rp, rs = bp * seq, bs * ts
    tm_s = min(cfg.tm, rs)

    mod = _ada_mod(jnp.concatenate([c_prompt, c_sample], axis=0), ada_w, ada_b)
    tb, s_far, s_last, s_new = _bias_tables(rel_bias, cfg.blk)

    n_phys = cache_k.shape[1]
    cache_kt = cache_k.transpose(0, 1, 3, 4, 2).reshape(depth, n_phys, D_ATT, PAGE_SIZE)
    cache_vt = cache_v.transpose(0, 1, 3, 4, 2).reshape(depth, n_phys, D_ATT, PAGE_SIZE)
    cache_kit = cache_kidx.transpose(0, 1, 3, 2)

    xp = x_prompt.reshape(rp, D_MODEL)
    xs = x_sample.reshape(rs, D_MODEL)
    state_p = ()
    pool_p = []
    outs_s = [[], [], [], []]
    for l in range(depth):
        w_in_b = jnp.pad(w_in[l], ((0, 0), (0, D_IN_PAD - D_IN))).astype(BF16)
        wo_b, w1_b, w2_b = w_out[l].astype(BF16), w_ff1[l].astype(BF16), w_ff2[l].astype(BF16)
        lnrow = lambda a: a[l].reshape(1, D_MODEL)
        mod_p = mod[l, :bp].reshape(bp, 1, 6 * D_MODEL)
        mod_s = jnp.repeat(mod[l, bp:], ts, axis=0)
        psc = pool_scale[l].reshape(1, C_POOL)

        u, kb, tailb, *state_p, wit, qt, qit, vbt = _in_proj(
            xp, mod_p, seq, lnrow(ln1_pre), w_in_b, cfg.tm, layer=l, depth=depth, carried=tuple(state_p))
        u3 = u.reshape(bp, seq, C_POOL)
        y_pool = _pool_mix(u3, None, 0, pool_w[l], psc, 1, min(256, seq)).reshape(rp, C_POOL)
        y_att = _attn_prompt(cfg, k_prompt, qt, qit, wit, kb, tailb, vbt, tb)
        xp = _out_ffn(xp, y_pool, y_att, mod_p, seq, lnrow(ln1_post), lnrow(ln2_pre), lnrow(ln2_post),
                      wo_b, w1_b, w2_b, cfg.tm)
        pool_p.append(u3[:, seq - POOL_HIST:])

        u, q, qi, tail, k, v = _in_proj(xs, mod_s, 0, lnrow(ln1_pre), w_in_b, tm_s)
        u3 = u.reshape(bs, ts, C_POOL)
        hist = jnp.concatenate([jnp.zeros((bs, POOL_PAD - POOL_HIST, C_POOL), F32), state_pool[l]], axis=1)
        y_pool = _pool_mix(u3, hist, cfg.past_len, pool_w[l], psc, min(16, bs), ts).reshape(rs, C_POOL)
        qi_ht = qi.reshape(bs, ts, IDX_HEADS, IDX_DIM).transpose(0, 2, 1, 3).reshape(bs, IDX_HEADS * ts, IDX_DIM)
        wi = tail[:, OFF_WI - OFF_KI:OFF_WI - OFF_KI + IDX_HEADS].reshape(bs, ts, IDX_HEADS) * IDX_HEADS ** -0.5
        wrep = jnp.broadcast_to(wi.transpose(0, 2, 1).reshape(bs, IDX_HEADS * ts, 1), (bs, IDX_HEADS * ts, LANES))
        scores = _score_sample(cfg, l, page_table, qi_ht, wrep, tail.reshape(bs, ts, TAIL), cache_kit)
        thr, lim = _threshold_cols(k_sample, scores.transpose(2, 0, 1).reshape(scores.shape[2], rs),
                                   min(256, rs))
        per_query = lambda a: jnp.broadcast_to(a[0].reshape(bs, ts, 1), (bs, ts, LANES))
        y_att = _attn_sample(cfg, l, page_table, scores, per_query(thr), per_query(lim),
                             q.astype(F32).reshape(bs, ts, D_ATT),
                             k.reshape(bs, ts, D_ATT), v.reshape(bs, ts, D_ATT),
                             s_far, s_last, s_new, cache_kt, cache_vt).reshape(rs, D_ATT)
        xs = _out_ffn(xs, y_pool, y_att, mod_s, 0, lnrow(ln1_post), lnrow(ln2_pre), lnrow(ln2_post),
                      wo_b, w1_b, w2_b, tm_s)
        for lst, a in zip(outs_s, (k.reshape(bs, ts, N_HEADS, HEAD_DIM), v.reshape(bs, ts, N_HEADS, HEAD_DIM),
                                   tail[:, :IDX_DIM].reshape(bs, ts, IDX_DIM),
                                   jnp.concatenate([hist, u3], axis=1)[:, -POOL_HIST:])):
            lst.append(a)

    kt, vt, kit = state_p
    heads_last = lambda a: a.reshape(depth, bp, N_HEADS, HEAD_DIM, seq).transpose(0, 1, 4, 2, 3)
    return (xp.reshape(bp, seq, D_MODEL), xs.reshape(bs, ts, D_MODEL),
            heads_last(kt), heads_last(vt), kit.transpose(0, 1, 3, 2), jnp.stack(pool_p),
            *[jnp.stack(a) for a in outs_s])


def kernel(x_prompt, x_sample, cache_k, cache_v, cache_kidx, state_pool, page_table, c_prompt, c_sample,
           rel_bias, ada_w, ada_b, ln1_pre, ln1_post, ln2_pre, ln2_post, w_in, pool_w, pool_scale,
           w_out, w_ff1, w_ff2):
    cfg = Cfg(batch=x_prompt.shape[0], seq=x_prompt.shape[1], dec_batch=x_sample.shape[0],
              dec_seq=x_sample.shape[1], past_len=page_table.shape[1] * PAGE_SIZE, blk=256, tm=256)
    return _forward(cfg, x_prompt, x_sample, cache_k, cache_v, cache_kidx, state_pool, page_table, c_prompt,
                    c_sample, rel_bias, ada_w, ada_b, ln1_pre, ln1_post, ln2_pre, ln2_post, w_in, pool_w,
                    pool_scale, w_out, w_ff1, w_ff2)
```

```python
import functools
import math
from typing import NamedTuple

import jax
import jax.numpy as jnp
from jax import lax
from jax.experimental import pallas as pl
from jax.experimental.pallas import tpu as pltpu

F32 = jnp.float32
BF16 = jnp.bfloat16
I32 = jnp.int32

D_MODEL = 1024
C_POOL = 512
D_ATT = 512
HEAD_DIM = 64
N_HEADS = 8
IDX_HEADS = 8
IDX_DIM = 64
POOL_WINDOWS = (2, 4, 8, 16)
POOL_GC = C_POOL // len(POOL_WINDOWS)
POOL_HIST = max(POOL_WINDOWS) - 1
TOPK_MAX = 256
N_BUCKETS = 32
MAX_DISTANCE = 128
D_FF = 4 * D_MODEL
EPS = 1e-6
PAGE_SIZE = 128
OFF_Q = C_POOL
OFF_K = OFF_Q + D_ATT
OFF_V = OFF_K + D_ATT
OFF_QI = OFF_V + D_ATT
OFF_KI = OFF_QI + IDX_HEADS * IDX_DIM
OFF_WI = OFF_KI + IDX_DIM
D_IN = OFF_WI + IDX_HEADS

LANES = 128
SUBLANES = 8
BF16_ROWS = 2 * SUBLANES
D_IN_PAD = ((D_IN + LANES - 1) // LANES) * LANES
TAIL = D_IN_PAD - OFF_KI
POOL_PAD = 16
OUT_SUB_ROWS = 256
IN_SUB_ROWS = 256
VMEM_LIMIT = 56 * 1024 * 1024

NEG_INF = float("-inf")
INT_MIN = -(2 ** 31)


class Cfg(NamedTuple):
    batch: int
    seq: int
    dec_batch: int
    dec_seq: int
    past_len: int
    blk: int
    tm: int
    tm_out: int


def _rms(x, g):
    ms = jnp.mean(x * x, axis=-1, keepdims=True)
    return x * lax.rsqrt(ms + EPS) * g


def _nt_dot(a, b):
    return lax.dot_general(a, b, (((1,), (1,)), ((), ())), preferred_element_type=F32)


def _dot(a, b):
    return jnp.dot(a, b, preferred_element_type=F32)


def _key_to_float(key):
    bits = key ^ ((key >> 31) & jnp.int32(0x7FFFFFFF))
    return lax.bitcast_convert_type(bits, F32)


def _lane_rep(col, width=LANES):
    return jnp.broadcast_to(col, (col.shape[0], width))


def _fold_tiles(x, op=jnp.add):
    acc = x[0:SUBLANES]
    for r in range(1, x.shape[0] // SUBLANES):
        acc = op(acc, x[r * SUBLANES:(r + 1) * SUBLANES])
    return acc


def _ada_kernel(c_ref, w_ref, b_ref, o_ref):
    o_ref[...] = _dot(c_ref[...].astype(BF16), w_ref[...].astype(BF16)) + b_ref[...]


def _ada_mod(c_all, ada_w, ada_b):
    depth, d, n = ada_w.shape
    rc = c_all.shape[0]
    tn = 1536
    return pl.pallas_call(
        _ada_kernel,
        grid=(depth, n // tn),
        in_specs=[pl.BlockSpec((rc, d), lambda l, j: (0, 0)),
                  pl.BlockSpec((None, d, tn), lambda l, j: (l, 0, j)),
                  pl.BlockSpec((None, 1, tn), lambda l, j: (l, 0, j))],
        out_specs=pl.BlockSpec((None, rc, tn), lambda l, j: (l, 0, j)),
        out_shape=jax.ShapeDtypeStruct((depth, rc, n), F32),
        name="ada_mod",
    )(c_all, ada_w, ada_b.reshape(depth, 1, n))


def _mod_spec(per_batch_rows, tm, chunk):
    if per_batch_rows:
        tiles = per_batch_rows // tm
        return pl.BlockSpec((None, 1, D_MODEL), lambda i: (i // tiles, 0, chunk))
    return pl.BlockSpec((tm, D_MODEL), lambda i: (i, chunk))


def _in_kernel(transposed, n_carried, n_sub, x_ref, sh_ref, sc_ref, ln_ref, w_ref, *refs):
    u_ref, *refs = refs[n_carried:]
    sub = x_ref.shape[0] // n_sub
    mod = lambda ref, rows: ref[...] if ref.shape[0] == 1 else ref[rows, :]

    def project(i):
        rows = pl.ds(i * sub, sub)
        h = _rms(x_ref[rows, :], ln_ref[...]) * (1.0 + mod(sc_ref, rows)) + mod(sh_ref, rows)
        return _dot(h.astype(BF16), w_ref[...])

    def emit(i, z):
        rows = pl.ds(i * sub, sub)
        u_ref[rows, :] = z[:, :OFF_Q]
        q = z[:, OFF_Q:OFF_K] * HEAD_DIM ** -0.5
        k = z[:, OFF_K:OFF_V]
        v = z[:, OFF_V:OFF_QI]
        qi = z[:, OFF_QI:OFF_KI]
        tail = z[:, OFF_KI:]
        if transposed:
            kb_ref, tailb_ref, kt_ref, vt_ref, kit_ref, wit_ref, qt_ref, qit_ref, vbt_ref = refs
            cols = pl.ds(i * sub, sub)
            kb_ref[rows, :] = k.astype(BF16)
            tailb_ref[rows, :] = tail.astype(BF16)
            kt_ref[:, cols] = k.T
            vt = v.T
            vt_ref[:, cols] = vt
            vbt_ref[:, cols] = vt.astype(BF16)
            tail_t = tail.T
            kit_ref[:, cols] = tail_t[:IDX_DIM]
            wit_ref[:, cols] = tail_t[OFF_WI - OFF_KI:OFF_WI - OFF_KI + IDX_HEADS]
            qt_ref[:, cols] = q.T.astype(BF16)
            qit_ref[:, cols] = qi.T.astype(BF16)
        else:
            q_ref, qi_ref, tail_ref, k_ref, v_ref = refs
            q_ref[rows, :] = q.astype(BF16)
            qi_ref[rows, :] = qi.astype(BF16)
            tail_ref[rows, :] = tail
            k_ref[rows, :] = k
            v_ref[rows, :] = v

    z = project(0)
    for i in range(n_sub):
        z_next = project(i + 1) if i + 1 < n_sub else None
        emit(i, z)
        z = z_next


def _in_proj(x, mod, per_batch_rows, ln, w_in_b, tm, layer=0, depth=1, carried=()):
    r = x.shape[0]
    row = lambda width: pl.BlockSpec((tm, width), lambda i: (i, 0))
    const = lambda shape: pl.BlockSpec(shape, lambda i: (0, 0))
    in_specs = [row(D_MODEL), _mod_spec(per_batch_rows, tm, 0), _mod_spec(per_batch_rows, tm, 1),
                const((1, D_MODEL)), const((D_MODEL, D_IN_PAD))]
    aliases = {}
    if per_batch_rows:
        tiles = per_batch_rows // tm
        nb = r // per_batch_rows
        rows_out = [(C_POOL, F32), (D_ATT, BF16), (TAIL, BF16)]
        state_out = [D_ATT, D_ATT, IDX_DIM]
        cols_out = [(IDX_HEADS, F32), (D_ATT, BF16), (IDX_HEADS * IDX_DIM, BF16), (D_ATT, BF16)]
    else:
        rows_out = [(C_POOL, F32), (D_ATT, BF16), (IDX_HEADS * IDX_DIM, BF16), (TAIL, F32), (D_ATT, F32), (D_ATT, F32)]
        state_out, cols_out = [], []
    out_specs = [row(w) for w, _ in rows_out]
    out_shape = [jax.ShapeDtypeStruct((r, w), dt) for w, dt in rows_out]
    for n, width in enumerate(state_out):
        if carried:
            aliases[len(in_specs)] = len(out_specs)
            in_specs.append(pl.BlockSpec(memory_space=pl.ANY))
        out_specs.append(pl.BlockSpec((None, None, width, tm), lambda i: (layer, i // tiles, 0, i % tiles)))
        out_shape.append(jax.ShapeDtypeStruct((depth, nb, width, per_batch_rows), F32))
    for width, dt in cols_out:
        out_specs.append(pl.BlockSpec((None, width, tm), lambda i: (i // tiles, 0, i % tiles)))
        out_shape.append(jax.ShapeDtypeStruct((nb, width, per_batch_rows), dt))
    return pl.pallas_call(
        functools.partial(_in_kernel, bool(per_batch_rows), len(carried), max(1, tm // IN_SUB_ROWS)),
        grid=(r // tm,),
        in_specs=in_specs,
        out_specs=out_specs,
        out_shape=out_shape,
        input_output_aliases=aliases,
        compiler_params=pltpu.CompilerParams(vmem_limit_bytes=VMEM_LIMIT),
        name="in_proj",
    )(x, mod, mod, ln, w_in_b, *carried)


def _out_kernel(n_sub, x_ref, yp_ref, ya_ref, g1_ref, sh2_ref, sc2_ref, g2_ref,
                ln1_ref, ln2a_ref, ln2b_ref, wo_ref, w1_ref, w2_ref, o_ref):
    sub = x_ref.shape[0] // n_sub
    rows = [pl.ds(i * sub, sub) for i in range(n_sub)]
    mod = lambda ref, i: ref[...] if ref.shape[0] == 1 else ref[rows[i], :]

    def out_proj(i):
        return (_dot(yp_ref[rows[i], :].astype(BF16), wo_ref[:C_POOL, :])
                + _dot(ya_ref[rows[i], :].astype(BF16), wo_ref[C_POOL:, :]))

    def residual_and_prenorm(i, mixed):
        x1 = x_ref[rows[i], :] + mod(g1_ref, i) * _rms(mixed, ln1_ref[...])
        h = (_rms(x1, ln2a_ref[...]) * (1.0 + mod(sc2_ref, i)) + mod(sh2_ref, i)).astype(BF16)
        return x1, h

    def mlp_chunk(h, f, c):
        sl = slice(c * D_MODEL, (c + 1) * D_MODEL)
        a = jnp.maximum(_dot(h, w1_ref[:, sl]), 0.0)
        return f + _dot((a * a).astype(BF16), w2_ref[sl, :])

    def finish(i, x1, f):
        o_ref[rows[i], :] = x1 + mod(g2_ref, i) * _rms(f, ln2b_ref[...])

    n_chunks = D_FF // D_MODEL
    mixed = [out_proj(i) for i in range(n_sub)]
    x1, h = residual_and_prenorm(0, mixed[0])
    for i in range(n_sub):
        f = mlp_chunk(h, jnp.zeros(x1.shape, F32), 0)
        if i + 1 < n_sub:
            nxt = residual_and_prenorm(i + 1, mixed[i + 1])
        if i > 0:
            finish(i - 1, *done)
        for c in range(1, n_chunks):
            f = mlp_chunk(h, f, c)
        done = (x1, f)
        if i + 1 < n_sub:
            x1, h = nxt
    finish(n_sub - 1, *done)


def _out_ffn(x, yp, ya, mod, per_batch_rows, ln1_post, ln2_pre, ln2_post, wo_b, w1_b, w2_b, tm):
    r = x.shape[0]
    row = lambda width: pl.BlockSpec((tm, width), lambda i: (i, 0))
    const = lambda shape: pl.BlockSpec(shape, lambda i: (0, 0), pipeline_mode=pl.Buffered(1))
    ms = lambda chunk: _mod_spec(per_batch_rows, tm, chunk)
    return pl.pallas_call(
        functools.partial(_out_kernel, max(1, tm // OUT_SUB_ROWS)),
        grid=(r // tm,),
        in_specs=[row(D_MODEL), row(C_POOL), row(D_ATT), ms(2), ms(3), ms(4), ms(5),
                  const((1, D_MODEL)), const((1, D_MODEL)), const((1, D_MODEL)),
                  const((D_MODEL, D_MODEL)), const((D_MODEL, D_FF)), const((D_FF, D_MODEL))],
        out_specs=row(D_MODEL),
        out_shape=jax.ShapeDtypeStruct((r, D_MODEL), F32),
        compiler_params=pltpu.CompilerParams(vmem_limit_bytes=VMEM_LIMIT),
        name="out_ffn",
    )(x, yp, ya, mod, mod, mod, mod, ln1_post, ln2_pre, ln2_post, wo_b, w1_b, w2_b)


def _pool_kernel(pos0, t_rows, chunk, has_hist, u_ref, *refs):
    if has_hist:
        hist_ref, pw_ref, ps_ref, o_ref, ue_ref = refs
        ue_ref[:, :POOL_PAD, :] = hist_ref[...]
    else:
        pw_ref, ps_ref, o_ref, ue_ref = refs
        ue_ref[:, :POOL_PAD, :] = jnp.zeros((ue_ref.shape[0], POOL_PAD, C_POOL), F32)
    ue_ref[:, POOL_PAD:, :] = u_ref[...]
    g_elems = ue_ref.shape[0]
    n_chunks = t_rows // chunk
    for ci in range(n_chunks):
        r0 = ci * chunk
        pos = pos0 + r0 + lax.broadcasted_iota(I32, (chunk, POOL_GC), 0)
        for g, w in enumerate(POOL_WINDOWS):
            lanes = pl.ds(g * POOL_GC, POOL_GC)
            cnt = jnp.minimum(pos + 1, w).astype(F32)
            pooled = []
            for e in range(g_elems):
                cur = ue_ref[e, pl.ds(POOL_PAD + r0, chunk), lanes]
                acc = cur
                for j in range(1, w):
                    acc = acc + ue_ref[e, pl.ds(POOL_PAD + r0 - j, chunk), lanes]
                pooled.append(acc / cnt - cur)
            pooled = pooled[0] if g_elems == 1 else jnp.concatenate(pooled, axis=0)
            y = _dot(pooled.astype(BF16), pw_ref[g].astype(BF16))
            y = y * ps_ref[:, lanes]
            for e in range(g_elems):
                o_ref[e, pl.ds(r0, chunk), lanes] = y[e * chunk:(e + 1) * chunk].astype(o_ref.dtype)


def _pool_mix(u, hist, pos0, pool_w_l, pool_scale_l, g_elems, chunk):
    b, t_rows, _ = u.shape
    per_elem = lambda rows: pl.BlockSpec((g_elems, rows, C_POOL), lambda i: (i, 0, 0))
    in_specs = [per_elem(t_rows)] + ([per_elem(POOL_PAD)] if hist is not None else []) + [
        pl.BlockSpec((len(POOL_WINDOWS), POOL_GC, POOL_GC), lambda i: (0, 0, 0)),
        pl.BlockSpec((1, C_POOL), lambda i: (0, 0))]
    args = (u,) + ((hist,) if hist is not None else ()) + (pool_w_l, pool_scale_l)
    return pl.pallas_call(
        functools.partial(_pool_kernel, pos0, t_rows, chunk, hist is not None),
        grid=(b // g_elems,),
        in_specs=in_specs,
        out_specs=per_elem(t_rows),
        out_shape=jax.ShapeDtypeStruct((b, t_rows, C_POOL), F32),
        scratch_shapes=[pltpu.VMEM((g_elems, POOL_PAD + t_rows, C_POOL), F32)],
        compiler_params=pltpu.CompilerParams(vmem_limit_bytes=VMEM_LIMIT),
        name="pool_mix",
    )(*args)


def _bias_of_dist(dist, rb_ref, h):
    n = jnp.maximum(dist, 0)
    max_exact = N_BUCKETS // 2
    large = max_exact + (jnp.log(jnp.maximum(n, 1).astype(F32) / max_exact)
                         / math.log(MAX_DISTANCE / max_exact) * (N_BUCKETS - max_exact)).astype(I32)
    large = jnp.minimum(large, N_BUCKETS - 1)
    bucket = jnp.where(n < max_exact, n, large)
    out = jnp.zeros(dist.shape, F32)
    for b in range(N_BUCKETS):
        out = jnp.where(bucket == b, rb_ref[b, h], out)
    return out


def _bias_kernel(blk, rb_ref, tb_ref, s_far_ref, s_last_ref, s_new_ref):
    key = lax.broadcasted_iota(I32, (blk, blk), 0)
    qry = lax.broadcasted_iota(I32, (blk, blk), 1)
    t = lax.broadcasted_iota(I32, (SUBLANES, LANES), 0)
    jj = lax.broadcasted_iota(I32, (SUBLANES, LANES), 1)
    for h in range(N_HEADS):
        tb_ref[0, h] = _bias_of_dist(jnp.full((blk, blk), 2 * blk, I32), rb_ref, h)
        tb_ref[1, h] = _bias_of_dist(blk + qry - key, rb_ref, h)
        tb_ref[2, h] = _bias_of_dist(qry - key, rb_ref, h)
        rows = pl.ds(h * SUBLANES, SUBLANES)
        s_far_ref[rows, :] = _bias_of_dist(jnp.full((SUBLANES, LANES), 2 * PAGE_SIZE, I32), rb_ref, h)
        s_last_ref[rows, :] = _bias_of_dist(PAGE_SIZE + t - jj, rb_ref, h)
        s_new_ref[rows, :] = _bias_of_dist(t - jj, rb_ref, h)


def _bias_tables(rel_bias, blk):
    sm = jax.ShapeDtypeStruct((N_HEADS * SUBLANES, LANES), F32)
    return pl.pallas_call(
        functools.partial(_bias_kernel, blk),
        in_specs=[pl.BlockSpec(memory_space=pltpu.SMEM)],
        out_shape=[jax.ShapeDtypeStruct((3, N_HEADS, blk, blk), F32), sm, sm, sm],
        compiler_params=pltpu.CompilerParams(vmem_limit_bytes=VMEM_LIMIT),
        name="bias_tables",
    )(rel_bias)


def _kth_start(shape):
    return jnp.full(shape, INT_MIN, I32), jnp.zeros(shape, F32)


def _kth_rounds(count_ge_key, k, carry, first_round, last_round, digit_bits=1):
    def body(r, carry):
        tkey, nge = carry
        shift = 32 - digit_bits * (r + 1)
        best_key, best_cnt = tkey, nge
        for j in range(1, 1 << digit_bits):
            ckey = tkey + lax.shift_left(jnp.int32(j), shift)
            cnt = count_ge_key(ckey)
            ok = cnt >= k
            best_key = jnp.where(ok, ckey, best_key)
            best_cnt = jnp.where(ok, cnt, best_cnt)
        return best_key, best_cnt

    return lax.fori_loop(first_round, last_round, body, carry)


def _kth_finish(carry):
    tkey, nge = carry
    return jnp.where(tkey == INT_MIN, NEG_INF, _key_to_float(tkey)), nge


def _kth_largest(count_ge, k, shape, digit_bits=1):
    carry = _kth_rounds(lambda ckey: count_ge(_key_to_float(ckey)), k, _kth_start(shape),
                        0, 32 // digit_bits, digit_bits)
    return _kth_finish(carry)


def _tie_limit(count_eq_below, need, shape, bits):
    def body(it, lim):
        cand = lim + lax.shift_left(jnp.int32(1), bits - 1 - it)
        return jnp.where(count_eq_below(cand) <= need, cand, lim)

    return lax.fori_loop(0, bits, body, jnp.zeros(shape, I32))


def _select_mask(s, idx, thr, lim):
    tie = jnp.where(s == thr, jnp.where(idx < lim, 0.0, NEG_INF), NEG_INF)
    m = jnp.where(s > thr, 0.0, tie)
    return jnp.where(s == NEG_INF, NEG_INF, m)


def _attn_prompt_kernel(k_sel, blk, seq,
                        qt_ref, qit_ref, wit_ref, kb_ref, kib_ref, vbt_ref, tb_ref,
                        o_ref,
                        score_ref, trunc_ref, lg_ref, mrun_ref, lsum_ref, acc_ref, thr_ref, lim_ref):
    qb = pl.program_id(1)
    n_chunks = qb + 1
    rep = (SUBLANES, blk)
    idx_bits = max(1, (seq - 1).bit_length()) + 1

    key_row = lax.broadcasted_iota(I32, (blk, blk), 0)
    qry_col = lax.broadcasted_iota(I32, (blk, blk), 1)

    def chunk_keys(c):
        return pl.ds(pl.multiple_of(c * blk, blk), blk)

    def all_sublanes(x, op):
        return jnp.broadcast_to(op(x, axis=0, keepdims=True), rep)

    w_idx = wit_ref[...] * IDX_HEADS ** -0.5 * IDX_DIM ** -0.5

    def score_chunk(c, carry):
        keys = chunk_keys(c)
        kic = kib_ref[keys, :][:, :IDX_DIM]
        s = jnp.zeros((blk, blk), F32)
        for h in range(IDX_HEADS):
            d = _dot(kic, qit_ref[h * IDX_DIM:(h + 1) * IDX_DIM, :])
            s = s + w_idx[h:h + 1, :] * jnp.maximum(d, 0.0)
        s = jnp.where(key_row + c * blk <= qry_col + qb * blk, s, NEG_INF)
        score_ref[keys, :] = s
        hi = lax.bitcast_convert_type(s, I32) & jnp.int32(-(1 << 16))
        trunc_ref[keys, :] = lax.bitcast_convert_type(hi, F32).astype(BF16)
        return carry

    lax.fori_loop(0, n_chunks, score_chunk, 0)

    def count_where(pred):
        def body(c, acc):
            return acc + _fold_tiles(pred(score_ref[chunk_keys(c), :], key_row + c * blk))
        acc = lax.fori_loop(0, n_chunks, body, jnp.zeros(rep, F32))
        return all_sublanes(acc, jnp.sum)

    def count_upper_half(ckey):
        bits = ckey ^ ((ckey >> 31) & jnp.int32(0x7FFFFFFF))
        cf = lax.bitcast_convert_type(bits & jnp.int32(-(1 << 16)), F32)
        cb = jnp.concatenate([cf, cf], axis=0).astype(BF16)
        one, zero = jnp.ones((BF16_ROWS, blk), BF16), jnp.zeros((BF16_ROWS, blk), BF16)

        def body(c, acc):
            t = trunc_ref[chunk_keys(c), :]
            for r in range(blk // BF16_ROWS):
                acc = acc + jnp.where(t[r * BF16_ROWS:(r + 1) * BF16_ROWS] >= cb, one, zero)
            return acc
        acc = lax.fori_loop(0, n_chunks, body, zero)
        return all_sublanes(acc.astype(F32), jnp.sum)

    carry = _kth_rounds(count_upper_half, k_sel, _kth_start(rep), 0, 16)
    carry = _kth_rounds(
        lambda ckey: count_where(lambda s, idx: jnp.where(s >= _key_to_float(ckey)[0:1], 1.0, 0.0)),
        k_sel, carry, 16, 32)
    thr, nge = _kth_finish(carry)
    thr_ref[...] = thr
    lim_ref[...] = jnp.full(rep, 1 << idx_bits, I32)

    @pl.when(jnp.max(nge) > k_sel)
    def _():
        n_gt = count_where(lambda s, idx: jnp.where(s > thr[0:1], 1.0, 0.0))
        need = k_sel - n_gt
        lim_ref[...] = _tie_limit(
            lambda cand: count_where(
                lambda s, idx: jnp.where(s == thr[0:1], jnp.where(idx < cand[0:1], 1.0, 0.0), 0.0)),
            need, rep, idx_bits)

    def mask_chunk(c, carry):
        keys = chunk_keys(c)
        score_ref[keys, :] = _select_mask(score_ref[keys, :], key_row + c * blk,
                                          thr_ref[0:1, :], lim_ref[0:1, :])
        return carry

    lax.fori_loop(0, n_chunks, mask_chunk, 0)

    mrun_ref[...] = jnp.full(mrun_ref.shape, NEG_INF, F32)

    def logits_chunk(c, carry):
        keys = chunk_keys(c)
        madd = score_ref[keys, :]
        table = jnp.clip(c - (qb - 2), 0, 2)
        for h in range(N_HEADS):
            hs = slice(h * HEAD_DIM, (h + 1) * HEAD_DIM)
            s = _dot(kb_ref[keys, hs], qt_ref[hs, :]) + tb_ref[table, h] + madd
            lg_ref[h, keys, :] = s
            mrun_ref[h] = jnp.maximum(mrun_ref[h], _fold_tiles(s, jnp.maximum))
        return carry

    lax.fori_loop(0, n_chunks, logits_chunk, 0)

    for h in range(N_HEADS):
        mrun_ref[h] = all_sublanes(mrun_ref[h], jnp.max)
    lsum_ref[...] = jnp.zeros(lsum_ref.shape, F32)
    acc_ref[...] = jnp.zeros(acc_ref.shape, F32)

    def pv_chunk(c, carry):
        keys = chunk_keys(c)
        for h in range(N_HEADS):
            hs = slice(h * HEAD_DIM, (h + 1) * HEAD_DIM)
            p = jnp.exp(lg_ref[h, keys, :] - mrun_ref[h, 0:1, :])
            lsum_ref[h] += _fold_tiles(p)
            acc_ref[h] += _dot(vbt_ref[hs, keys], p.astype(BF16))
        return carry

    lax.fori_loop(0, n_chunks, pv_chunk, 0)

    outs =[acc_ref[h] / jnp.sum(lsum_ref[h], axis=0, keepdims=True) for h in range(N_HEADS)]
    o_ref[...] = jnp.concatenate(outs, axis=0).T.astype(o_ref.dtype)


def _attn_prompt(cfg, k_sel, qt, qit, wit, kb, kib, vbt, tb):
    blk, seq = cfg.blk, cfg.seq
    nq = seq // blk
    qspec = lambda width: pl.BlockSpec((None, width, blk), lambda b, i: (b, 0, i))
    kspec = lambda width: pl.BlockSpec((seq, width), lambda b, i: (b, 0))
    return pl.pallas_call(
        functools.partial(_attn_prompt_kernel, k_sel, blk, seq),
        grid=(cfg.batch, nq),
        in_specs=[qspec(D_ATT), qspec(IDX_HEADS * IDX_DIM), qspec(IDX_HEADS),
                  kspec(D_ATT), kspec(TAIL), pl.BlockSpec((None, D_ATT, seq), lambda b, i: (b, 0, 0)),
                  pl.BlockSpec((3, N_HEADS, blk, blk), lambda b, i: (0, 0, 0, 0), pipeline_mode=pl.Buffered(1))],
        out_specs=pl.BlockSpec((blk, D_ATT), lambda b, i: (b * nq + i, 0)),
        out_shape=jax.ShapeDtypeStruct((cfg.batch * seq, D_ATT), BF16),
        scratch_shapes=[pltpu.VMEM((seq, blk), F32),
                        pltpu.VMEM((seq, blk), BF16),
                        pltpu.VMEM((N_HEADS, seq, blk), F32),
                        pltpu.VMEM((N_HEADS, SUBLANES, blk), F32),
                        pltpu.VMEM((N_HEADS, SUBLANES, blk), F32),
                        pltpu.VMEM((N_HEADS, HEAD_DIM, blk), F32),
                        pltpu.VMEM((SUBLANES, blk), F32),
                        pltpu.VMEM((SUBLANES, blk), I32)],
        compiler_params=pltpu.CompilerParams(vmem_limit_bytes=VMEM_LIMIT),
        name="attn_prompt",
    )(qt, qit, wit, kb, kib, vbt, tb)


def _pad_rows(x, rows):
    return jnp.concatenate([x, jnp.zeros((rows - x.shape[0], x.shape[1]), x.dtype)], axis=0)


def _score_sample_kernel(n_pages, pt_ref, qi_ref, wrep_ref, tailn_ref, *rest):
    ki_refs, o_ref = rest[:n_pages], rest[n_pages]
    shape = (SUBLANES, LANES)
    qi = qi_ref[...]
    w = wrep_ref[...]

    def score_block(dots):
        return _fold_tiles(w * jnp.maximum(dots * IDX_DIM ** -0.5, 0.0))

    for p in range(n_pages):
        o_ref[:, p * PAGE_SIZE:(p + 1) * PAGE_SIZE] = score_block(_dot(qi, ki_refs[p][...].astype(BF16)))
    ki_new = _pad_rows(tailn_ref[...][:, :IDX_DIM], PAGE_SIZE).astype(BF16)
    causal = lax.broadcasted_iota(I32, shape, 1) <= lax.broadcasted_iota(I32, shape, 0)
    o_ref[:, n_pages * PAGE_SIZE:] = jnp.where(causal, score_block(_nt_dot(qi, ki_new)), NEG_INF)


def _score_sample(cfg, layer, page_table, qi_ht, wrep, tail_new, cache_ki):
    b, t_rows = cfg.dec_batch, cfg.dec_seq
    n_pages = cfg.past_len // PAGE_SIZE
    rows = N_HEADS * t_rows
    per_b = lambda r, w: pl.BlockSpec((None, r, w), lambda i, pt: (i, 0, 0))
    pages = [pl.BlockSpec((None, None, IDX_DIM, PAGE_SIZE), lambda i, pt, p=p: (layer, pt[i, p], 0, 0))
             for p in range(n_pages)]
    n_keys = (n_pages + 1) * PAGE_SIZE
    return pl.pallas_call(
        functools.partial(_score_sample_kernel, n_pages),
        grid_spec=pltpu.PrefetchScalarGridSpec(
            num_scalar_prefetch=1, grid=(b,),
            in_specs=[per_b(rows, IDX_DIM), per_b(rows, LANES), per_b(t_rows, TAIL)] + pages,
            out_specs=per_b(t_rows, n_keys)),
        out_shape=jax.ShapeDtypeStruct((b, t_rows, n_keys), F32),
        name="score_sample",
    )(page_table, qi_ht, wrep, tail_new, *([cache_ki] * n_pages))


def _threshold_kernel(k_sel, s_ref, thr_ref, lim_ref):
    n_keys, cols = s_ref.shape
    rep = (SUBLANES, cols)
    idx_bits = (n_keys - 1).bit_length() + 1
    key_row = lax.broadcasted_iota(I32, (LANES, cols), 0)

    def count_where(pred):
        acc = jnp.zeros(rep, F32)
        for c in range(n_keys // LANES):
            acc = acc + _fold_tiles(pred(s_ref[c * LANES:(c + 1) * LANES, :], key_row + c * LANES))
        return jnp.broadcast_to(jnp.sum(acc, axis=0, keepdims=True), rep)

    thr, nge = _kth_largest(lambda cf: count_where(lambda s, idx: jnp.where(s >= cf[0:1], 1.0, 0.0)), k_sel, rep)
    thr_ref[...] = thr
    lim_ref[...] = jnp.full(rep, 1 << idx_bits, I32)

    @pl.when(jnp.max(nge) > k_sel)
    def _():
        n_gt = count_where(lambda s, idx: jnp.where(s > thr[0:1], 1.0, 0.0))
        need = k_sel - n_gt
        lim_ref[...] = _tie_limit(
            lambda cand: count_where(
                lambda s, idx: jnp.where(s == thr[0:1], jnp.where(idx < cand[0:1], 1.0, 0.0), 0.0)),
            need, rep, idx_bits)


def _threshold_cols(k_sel, scores_t, cols):
    n_keys, n_q = scores_t.shape
    spec = pl.BlockSpec((SUBLANES, cols), lambda i: (0, i))
    return pl.pallas_call(
        functools.partial(_threshold_kernel, k_sel),
        grid=(n_q // cols,),
        in_specs=[pl.BlockSpec((n_keys, cols), lambda i: (0, i))],
        out_specs=[spec, spec],
        out_shape=[jax.ShapeDtypeStruct((SUBLANES, n_q), F32), jax.ShapeDtypeStruct((SUBLANES, n_q), I32)],
        name="threshold_cols",
    )(scores_t)


def _attn_sample_kernel(n_pages, pt_ref,
                        sc_ref, thr_ref, lim_ref, q_ref, kn_ref, vn_ref, bfar_ref, blast_ref, bnew_ref,
                        *rest):
    k_refs = rest[:n_pages]
    v_refs = rest[n_pages:2 * n_pages]
    o_ref = rest[2 * n_pages]
    t_rows = SUBLANES
    n_blocks = n_pages + 1
    lane = lax.broadcasted_iota(I32, (t_rows, LANES), 1)
    thr, lim = thr_ref[...], lim_ref[...]
    masks = [_select_mask(sc_ref[:, b * PAGE_SIZE:(b + 1) * PAGE_SIZE], lane + b * PAGE_SIZE, thr, lim)
             for b in range(n_blocks)]

    def pad_new(x):
        return _pad_rows(x, PAGE_SIZE)

    rows = N_HEADS * t_rows
    same_head = (lax.broadcasted_iota(I32, (rows, D_ATT), 0) // t_rows
                 == lax.broadcasted_iota(I32, (rows, D_ATT), 1) // HEAD_DIM)
    q_bd = jnp.where(same_head, jnp.concatenate([q_ref[...]] * N_HEADS, axis=0), 0.0).astype(BF16)

    def tile_heads(m):
        return jnp.concatenate([m] * N_HEADS, axis=0)

    logits = []
    for p in range(n_pages):
        bias = blast_ref[...] if p == n_pages - 1 else bfar_ref[...]
        logits.append(_dot(q_bd, k_refs[p][...].astype(BF16)) + bias + tile_heads(masks[p]))
    logits.append(_nt_dot(q_bd, pad_new(kn_ref[...]).astype(BF16)) + bnew_ref[...] + tile_heads(masks[n_pages]))

    m = logits[0]
    for lg in logits[1:]:
        m = jnp.maximum(m, lg)
    m = _lane_rep(jnp.max(m, axis=1, keepdims=True))
    lsum = jnp.zeros((rows, LANES), F32)
    acc = jnp.zeros((rows, D_ATT), F32)
    for b in range(n_blocks):
        p = jnp.exp(logits[b] - m)
        lsum = lsum + p
        if b < n_pages:
            acc = acc + _nt_dot(p.astype(BF16), v_refs[b][...].astype(BF16))
        else:
            acc = acc + _dot(p.astype(BF16), pad_new(vn_ref[...]).astype(BF16))
    out = jnp.where(same_head, acc / jnp.sum(lsum, axis=1, keepdims=True), 0.0)
    o_ref[...] = _fold_tiles(out)


def _attn_sample(cfg, layer, page_table, scores, thr_rep, lim_rep, q, k_new, v_new,
                 s_far, s_last, s_new, cache_k, cache_v):
    b = cfg.dec_batch
    t_rows = cfg.dec_seq
    n_pages = cfg.past_len // PAGE_SIZE
    rows = N_HEADS * t_rows
    per_b = lambda r, w: pl.BlockSpec((None, r, w), lambda i, pt: (i, 0, 0))
    const = lambda r, w: pl.BlockSpec((r, w), lambda i, pt: (0, 0))
    pages = [pl.BlockSpec((None, None, D_ATT, PAGE_SIZE), lambda i, pt, p=p: (layer, pt[i, p], 0, 0))
             for p in range(n_pages)]
    in_specs = ([per_b(t_rows, scores.shape[2]), per_b(t_rows, LANES), per_b(t_rows, LANES),
                 per_b(t_rows, D_ATT), per_b(t_rows, D_ATT), per_b(t_rows, D_ATT),
                 const(rows, LANES), const(rows, LANES), const(rows, LANES)] + pages + pages)
    return pl.pallas_call(
        functools.partial(_attn_sample_kernel, n_pages),
        grid_spec=pltpu.PrefetchScalarGridSpec(
            num_scalar_prefetch=1, grid=(b,), in_specs=in_specs, out_specs=per_b(t_rows, D_ATT)),
        out_shape=jax.ShapeDtypeStruct((b, t_rows, D_ATT), F32),
        compiler_params=pltpu.CompilerParams(vmem_limit_bytes=VMEM_LIMIT),
        name="attn_sample",
    )(page_table, scores, thr_rep, lim_rep, q, k_new, v_new, s_far, s_last, s_new,
      *([cache_k] * n_pages), *([cache_v] * n_pages))


def _forward(cfg, x_prompt, x_sample, cache_k, cache_v, cache_kidx, state_pool, page_table, c_prompt,
             c_sample, rel_bias, ada_w, ada_b, ln1_pre, ln1_post, ln2_pre, ln2_post, w_in, pool_w,
             pool_scale, w_out, w_ff1, w_ff2):
    depth = ada_w.shape[0]
    bp, seq, bs, ts = cfg.batch, cfg.seq, cfg.dec_batch, cfg.dec_seq
    assert ts == SUBLANES and seq % cfg.blk == 0 and cfg.past_len % PAGE_SIZE == 0
    assert cfg.blk >= PAGE_SIZE and seq >= TOPK_MAX and seq // BF16_ROWS <= 256
    k_prompt = min(TOPK_MAX, seq // 4)
    k_sample = min(TOPK_MAX, (cfg.past_len + ts) // 4)
    rp, rs = bp * seq, bs * ts
    tm_s = min(cfg.tm, rs)

    mod = _ada_mod(jnp.concatenate([c_prompt, c_sample], axis=0), ada_w, ada_b)
    tb, s_far, s_last, s_new = _bias_tables(rel_bias, cfg.blk)

    n_phys = cache_k.shape[1]
    cache_kt = cache_k.transpose(0, 1, 3, 4, 2).reshape(depth, n_phys, D_ATT, PAGE_SIZE)
    cache_vt = cache_v.transpose(0, 1, 3, 4, 2).reshape(depth, n_phys, D_ATT, PAGE_SIZE)
    cache_kit = cache_kidx.transpose(0, 1, 3, 2)

    xp = x_prompt.reshape(rp, D_MODEL)
    xs = x_sample.reshape(rs, D_MODEL)
    state_p = ()
    pool_p = []
    outs_s = [[], [], [], []]
    for l in range(depth):
        w_in_b = jnp.pad(w_in[l], ((0, 0), (0, D_IN_PAD - D_IN))).astype(BF16)
        wo_b, w1_b, w2_b = w_out[l].astype(BF16), w_ff1[l].astype(BF16), w_ff2[l].astype(BF16)
        lnrow = lambda a: a[l].reshape(1, D_MODEL)
        mod_p = mod[l, :bp].reshape(bp, 1, 6 * D_MODEL)
        mod_s = jnp.repeat(mod[l, bp:], ts, axis=0)
        psc = pool_scale[l].reshape(1, C_POOL)

        u, kb, tailb, *state_p, wit, qt, qit, vbt = _in_proj(
            xp, mod_p, seq, lnrow(ln1_pre), w_in_b, cfg.tm, layer=l, depth=depth, carried=tuple(state_p))
        u3 = u.reshape(bp, seq, C_POOL)
        y_pool = _pool_mix(u3, None, 0, pool_w[l], psc, 1, min(256, seq)).reshape(rp, C_POOL)
        y_att = _attn_prompt(cfg, k_prompt, qt, qit, wit, kb, tailb, vbt, tb)
        xp = _out_ffn(xp, y_pool, y_att, mod_p, seq, lnrow(ln1_post), lnrow(ln2_pre), lnrow(ln2_post),
                      wo_b, w1_b, w2_b, min(cfg.tm_out, seq))
        pool_p.append(u3[:, seq - POOL_HIST:])

        u, q, qi, tail, k, v = _in_proj(xs, mod_s, 0, lnrow(ln1_pre), w_in_b, tm_s)
        u3 = u.reshape(bs, ts, C_POOL)
        hist = jnp.concatenate([jnp.zeros((bs, POOL_PAD - POOL_HIST, C_POOL), F32), state_pool[l]], axis=1)
        y_pool = _pool_mix(u3, hist, cfg.past_len, pool_w[l], psc, min(16, bs), ts).reshape(rs, C_POOL)
        qi_ht = qi.reshape(bs, ts, IDX_HEADS, IDX_DIM).transpose(0, 2, 1, 3).reshape(bs, IDX_HEADS * ts, IDX_DIM)
        wi = tail[:, OFF_WI - OFF_KI:OFF_WI - OFF_KI + IDX_HEADS].reshape(bs, ts, IDX_HEADS) * IDX_HEADS ** -0.5
        wrep = jnp.broadcast_to(wi.transpose(0, 2, 1).reshape(bs, IDX_HEADS * ts, 1), (bs, IDX_HEADS * ts, LANES))
        scores = _score_sample(cfg, l, page_table, qi_ht, wrep, tail.reshape(bs, ts, TAIL), cache_kit)
        thr, lim = _threshold_cols(k_sample, scores.transpose(2, 0, 1).reshape(scores.shape[2], rs),
                                   min(256, rs))
        per_query = lambda a: jnp.broadcast_to(a[0].reshape(bs, ts, 1), (bs, ts, LANES))
        y_att = _attn_sample(cfg, l, page_table, scores, per_query(thr), per_query(lim),
                             q.astype(F32).reshape(bs, ts, D_ATT),
                             k.reshape(bs, ts, D_ATT), v.reshape(bs, ts, D_ATT),
                             s_far, s_last, s_new, cache_kt, cache_vt).reshape(rs, D_ATT)
        xs = _out_ffn(xs, y_pool, y_att, mod_s, 0, lnrow(ln1_post), lnrow(ln2_pre), lnrow(ln2_post),
                      wo_b, w1_b, w2_b, min(cfg.tm_out, rs))
        for lst, a in zip(outs_s, (k.reshape(bs, ts, N_HEADS, HEAD_DIM), v.reshape(bs, ts, N_HEADS, HEAD_DIM),
                                   tail[:, :IDX_DIM].reshape(bs, ts, IDX_DIM),
                                   jnp.concatenate([hist, u3], axis=1)[:, -POOL_HIST:])):
            lst.append(a)

    kt, vt, kit = state_p
    heads_last = lambda a: a.reshape(depth, bp, N_HEADS, HEAD_DIM, seq).transpose(0, 1, 4, 2, 3)
    return (xp.reshape(bp, seq, D_MODEL), xs.reshape(bs, ts, D_MODEL),
            heads_last(kt), heads_last(vt), kit.transpose(0, 1, 3, 2), jnp.stack(pool_p),
            *[jnp.stack(a) for a in outs_s])


def kernel(x_prompt, x_sample, cache_k, cache_v, cache_kidx, state_pool, page_table, c_prompt, c_sample,
           rel_bias, ada_w, ada_b, ln1_pre, ln1_post, ln2_pre, ln2_post, w_in, pool_w, pool_scale,
           w_out, w_ff1, w_ff2):
    cfg = Cfg(batch=x_prompt.shape[0], seq=x_prompt.shape[1], dec_batch=x_sample.shape[0],
              dec_seq=x_sample.shape[1], past_len=page_table.shape[1] * PAGE_SIZE, blk=256, tm=1024,
              tm_out=1024)
    return _forward(cfg, x_prompt, x_sample, cache_k, cache_v, cache_kidx, state_pool, page_table, c_prompt,
                    c_sample, rel_bias, ada_w, ada_b, ln1_pre, ln1_post, ln2_pre, ln2_post, w_in, pool_w,
                    pool_scale, w_out, w_ff1, w_ff2)
```

```python
import functools
import math
from typing import NamedTuple

import jax
import jax.numpy as jnp
from jax import lax
from jax.experimental import pallas as pl
from jax.experimental.pallas import tpu as pltpu

F32 = jnp.float32
BF16 = jnp.bfloat16
I32 = jnp.int32

D_MODEL = 1024
C_POOL = 512
D_ATT = 512
HEAD_DIM = 64
N_HEADS = 8
IDX_HEADS = 8
IDX_DIM = 64
POOL_WINDOWS = (2, 4, 8, 16)
POOL_GC = C_POOL // len(POOL_WINDOWS)
POOL_HIST = max(POOL_WINDOWS) - 1
TOPK_MAX = 256
N_BUCKETS = 32
MAX_DISTANCE = 128
D_FF = 4 * D_MODEL
EPS = 1e-6
PAGE_SIZE = 128
OFF_Q = C_POOL
OFF_K = OFF_Q + D_ATT
OFF_V = OFF_K + D_ATT
OFF_QI = OFF_V + D_ATT
OFF_KI = OFF_QI + IDX_HEADS * IDX_DIM
OFF_WI = OFF_KI + IDX_DIM
D_IN = OFF_WI + IDX_HEADS

LANES = 128
SUBLANES = 8
BF16_ROWS = 2 * SUBLANES
D_IN_PAD = ((D_IN + LANES - 1) // LANES) * LANES
TAIL = D_IN_PAD - OFF_KI
POOL_PAD = 16
OUT_SUB_ROWS = 256
IN_SUB_ROWS = 256
VMEM_LIMIT = 56 * 1024 * 1024

NEG_INF = float("-inf")
INT_MIN = -(2 ** 31)


class Cfg(NamedTuple):
    batch: int
    seq: int
    dec_batch: int
    dec_seq: int
    past_len: int
    blk: int
    tm: int
    tm_out: int


def _rms(x, g):
    ms = jnp.mean(x * x, axis=-1, keepdims=True)
    return x * lax.rsqrt(ms + EPS) * g


def _nt_dot(a, b):
    return lax.dot_general(a, b, (((1,), (1,)), ((), ())), preferred_element_type=F32)


def _dot(a, b):
    return jnp.dot(a, b, preferred_element_type=F32)


def _key_to_float(key):
    bits = key ^ ((key >> 31) & jnp.int32(0x7FFFFFFF))
    return lax.bitcast_convert_type(bits, F32)


def _lane_rep(col, width=LANES):
    return jnp.broadcast_to(col, (col.shape[0], width))


def _fold_tiles(x, op=jnp.add):
    return _tree([x[r * SUBLANES:(r + 1) * SUBLANES] for r in range(x.shape[0] // SUBLANES)], op)


def _tree(parts, op=jnp.add):
    while len(parts) > 1:
        parts = [op(parts[i], parts[i + 1]) if i + 1 < len(parts) else parts[i] for i in range(0, len(parts), 2)]
    return parts[0]


def _ada_kernel(c_ref, w_ref, b_ref, o_ref):
    o_ref[...] = _dot(c_ref[...].astype(BF16), w_ref[...].astype(BF16)) + b_ref[...]


def _ada_mod(c_all, ada_w, ada_b):
    depth, d, n = ada_w.shape
    rc = c_all.shape[0]
    tn = 1536
    return pl.pallas_call(
        _ada_kernel,
        grid=(depth, n // tn),
        in_specs=[pl.BlockSpec((rc, d), lambda l, j: (0, 0)),
                  pl.BlockSpec((None, d, tn), lambda l, j: (l, 0, j)),
                  pl.BlockSpec((None, 1, tn), lambda l, j: (l, 0, j))],
        out_specs=pl.BlockSpec((None, rc, tn), lambda l, j: (l, 0, j)),
        out_shape=jax.ShapeDtypeStruct((depth, rc, n), F32),
        name="ada_mod",
    )(c_all, ada_w, ada_b.reshape(depth, 1, n))


def _mod_spec(per_batch_rows, tm, chunk):
    if per_batch_rows:
        tiles = per_batch_rows // tm
        return pl.BlockSpec((None, 1, D_MODEL), lambda i: (i // tiles, 0, chunk))
    return pl.BlockSpec((tm, D_MODEL), lambda i: (i, chunk))


def _in_kernel(transposed, n_carried, n_sub, layer, x_ref, sh_ref, sc_ref, ln_ref, w_ref, *refs):
    u_ref, *refs = refs[n_carried:]
    sub = x_ref.shape[0] // n_sub
    mod = lambda ref, rows: ref[...] if ref.shape[0] == 1 else ref[rows, :]

    def put_state(ref, cols, val):
        if len(ref.shape) == 2:
            ref[:, cols] = val
        else:
            for l in range(ref.shape[0]):
                ref[l, :, cols] = val if l == layer else jnp.zeros(val.shape, val.dtype)

    def project(i):
        rows = pl.ds(i * sub, sub)
        h = _rms(x_ref[rows, :], ln_ref[...]) * (1.0 + mod(sc_ref, rows)) + mod(sh_ref, rows)
        return _dot(h.astype(BF16), w_ref[...])

    def emit(i, z):
        rows = pl.ds(i * sub, sub)
        u_ref[rows, :] = z[:, :OFF_Q]
        q = z[:, OFF_Q:OFF_K] * HEAD_DIM ** -0.5
        k = z[:, OFF_K:OFF_V]
        v = z[:, OFF_V:OFF_QI]
        qi = z[:, OFF_QI:OFF_KI]
        tail = z[:, OFF_KI:]
        if transposed:
            kb_ref, tailb_ref, kt_ref, vt_ref, kit_ref, wit_ref, qt_ref, qit_ref, vbt_ref = refs
            cols = pl.ds(i * sub, sub)
            kb_ref[rows, :] = k.astype(BF16)
            tailb_ref[rows, :] = tail.astype(BF16)
            put_state(kt_ref, cols, k.T)
            vt = v.T
            put_state(vt_ref, cols, vt)
            vbt_ref[:, cols] = vt.astype(BF16)
            tail_t = tail.T
            put_state(kit_ref, cols, tail_t[:IDX_DIM])
            wit_ref[:, cols] = tail_t[OFF_WI - OFF_KI:OFF_WI - OFF_KI + IDX_HEADS]
            qt_ref[:, cols] = q.T.astype(BF16)
            qit_ref[:, cols] = qi.T.astype(BF16)
        else:
            q_ref, qi_ref, tail_ref, k_ref, v_ref = refs
            q_ref[rows, :] = q.astype(BF16)
            qi_ref[rows, :] = qi.astype(BF16)
            tail_ref[rows, :] = tail
            k_ref[rows, :] = k
            v_ref[rows, :] = v

    z = project(0)
    for i in range(n_sub):
        z_next = project(i + 1) if i + 1 < n_sub else None
        emit(i, z)
        z = z_next


def _in_proj(x, mod, per_batch_rows, ln, w_in_b, tm, layer=0, depth=1, carried=()):
    r = x.shape[0]
    row = lambda width: pl.BlockSpec((tm, width), lambda i: (i, 0))
    const = lambda shape: pl.BlockSpec(shape, lambda i: (0, 0))
    in_specs = [row(D_MODEL), _mod_spec(per_batch_rows, tm, 0), _mod_spec(per_batch_rows, tm, 1),
                const((1, D_MODEL)), const((D_MODEL, D_IN_PAD))]
    aliases = {}
    if per_batch_rows:
        tiles = per_batch_rows // tm
        nb = r // per_batch_rows
        rows_out = [(C_POOL, F32), (D_ATT, BF16), (TAIL, BF16)]
        state_out = [D_ATT, D_ATT, IDX_DIM]
        cols_out = [(IDX_HEADS, F32), (D_ATT, BF16), (IDX_HEADS * IDX_DIM, BF16), (D_ATT, BF16)]
    else:
        rows_out = [(C_POOL, F32), (D_ATT, BF16), (IDX_HEADS * IDX_DIM, BF16), (TAIL, F32), (D_ATT, F32), (D_ATT, F32)]
        state_out, cols_out = [], []
    out_specs = [row(w) for w, _ in rows_out]
    out_shape = [jax.ShapeDtypeStruct((r, w), dt) for w, dt in rows_out]
    for n, width in enumerate(state_out):
        if carried:
            aliases[len(in_specs)] = len(out_specs)
            in_specs.append(pl.BlockSpec(memory_space=pl.ANY))
            out_specs.append(pl.BlockSpec((None, None, width, tm), lambda i: (layer, i // tiles, 0, i % tiles)))
        else:
            out_specs.append(pl.BlockSpec((depth, None, width, tm), lambda i: (0, i // tiles, 0, i % tiles)))
        out_shape.append(jax.ShapeDtypeStruct((depth, nb, width, per_batch_rows), F32))
    for width, dt in cols_out:
        out_specs.append(pl.BlockSpec((None, width, tm), lambda i: (i // tiles, 0, i % tiles)))
        out_shape.append(jax.ShapeDtypeStruct((nb, width, per_batch_rows), dt))
    return pl.pallas_call(
        functools.partial(_in_kernel, bool(per_batch_rows), len(carried), max(1, tm // IN_SUB_ROWS), layer),
        grid=(r // tm,),
        in_specs=in_specs,
        out_specs=out_specs,
        out_shape=out_shape,
        input_output_aliases=aliases,
        compiler_params=pltpu.CompilerParams(vmem_limit_bytes=VMEM_LIMIT),
        name="in_proj",
    )(x, mod, mod, ln, w_in_b, *carried)


def _out_kernel(n_sub, x_ref, yp_ref, ya_ref, g1_ref, sh2_ref, sc2_ref, g2_ref,
                ln1_ref, ln2a_ref, ln2b_ref, wo_ref, w1_ref, w2_ref, o_ref):
    sub = x_ref.shape[0] // n_sub
    rows = [pl.ds(i * sub, sub) for i in range(n_sub)]
    mod = lambda ref, i: ref[...] if ref.shape[0] == 1 else ref[rows[i], :]

    def out_proj(i):
        return (_dot(yp_ref[rows[i], :].astype(BF16), wo_ref[:C_POOL, :])
                + _dot(ya_ref[rows[i], :].astype(BF16), wo_ref[C_POOL:, :]))

    def residual_and_prenorm(i, mixed):
        x1 = x_ref[rows[i], :] + mod(g1_ref, i) * _rms(mixed, ln1_ref[...])
        h = (_rms(x1, ln2a_ref[...]) * (1.0 + mod(sc2_ref, i)) + mod(sh2_ref, i)).astype(BF16)
        return x1, h

    def mlp_chunk(h, f, c):
        sl = slice(c * D_MODEL, (c + 1) * D_MODEL)
        a = jnp.maximum(_dot(h, w1_ref[:, sl]), 0.0)
        return f + _dot((a * a).astype(BF16), w2_ref[sl, :])

    def finish(i, x1, f):
        o_ref[rows[i], :] = x1 + mod(g2_ref, i) * _rms(f, ln2b_ref[...])

    n_chunks = D_FF // D_MODEL
    mixed = [out_proj(i) for i in range(n_sub)]
    x1, h = residual_and_prenorm(0, mixed[0])
    for i in range(n_sub):
        f = mlp_chunk(h, jnp.zeros(x1.shape, F32), 0)
        if i + 1 < n_sub:
            nxt = residual_and_prenorm(i + 1, mixed[i + 1])
        if i > 0:
            finish(i - 1, *done)
        for c in range(1, n_chunks):
            f = mlp_chunk(h, f, c)
        done = (x1, f)
        if i + 1 < n_sub:
            x1, h = nxt
    finish(n_sub - 1, *done)


def _out_ffn(x, yp, ya, mod, per_batch_rows, ln1_post, ln2_pre, ln2_post, wo_b, w1_b, w2_b, tm):
    r = x.shape[0]
    row = lambda width: pl.BlockSpec((tm, width), lambda i: (i, 0))
    const = lambda shape: pl.BlockSpec(shape, lambda i: (0, 0), pipeline_mode=pl.Buffered(1))
    ms = lambda chunk: _mod_spec(per_batch_rows, tm, chunk)
    return pl.pallas_call(
        functools.partial(_out_kernel, max(1, tm // OUT_SUB_ROWS)),
        grid=(r // tm,),
        in_specs=[row(D_MODEL), row(C_POOL), row(D_ATT), ms(2), ms(3), ms(4), ms(5),
                  const((1, D_MODEL)), const((1, D_MODEL)), const((1, D_MODEL)),
                  const((D_MODEL, D_MODEL)), const((D_MODEL, D_FF)), const((D_FF, D_MODEL))],
        out_specs=row(D_MODEL),
        out_shape=jax.ShapeDtypeStruct((r, D_MODEL), F32),
        compiler_params=pltpu.CompilerParams(vmem_limit_bytes=VMEM_LIMIT),
        name="out_ffn",
    )(x, yp, ya, mod, mod, mod, mod, ln1_post, ln2_pre, ln2_post, wo_b, w1_b, w2_b)


def _pool_kernel(pos0, t_rows, chunk, has_hist, u_ref, *refs):
    if has_hist:
        hist_ref, pw_ref, ps_ref, o_ref, ue_ref = refs
        ue_ref[:, :POOL_PAD, :] = hist_ref[...]
    else:
        pw_ref, ps_ref, o_ref, ue_ref = refs
        ue_ref[:, :POOL_PAD, :] = jnp.zeros((ue_ref.shape[0], POOL_PAD, C_POOL), F32)
    ue_ref[:, POOL_PAD:, :] = u_ref[...]
    g_elems = ue_ref.shape[0]
    n_chunks = t_rows // chunk
    for ci in range(n_chunks):
        r0 = ci * chunk
        pos = pos0 + r0 + lax.broadcasted_iota(I32, (chunk, POOL_GC), 0)
        for g, w in enumerate(POOL_WINDOWS):
            lanes = pl.ds(g * POOL_GC, POOL_GC)
            cnt = jnp.minimum(pos + 1, w).astype(F32)
            pooled = []
            for e in range(g_elems):
                cur = ue_ref[e, pl.ds(POOL_PAD + r0, chunk), lanes]
                acc = cur
                for j in range(1, w):
                    acc = acc + ue_ref[e, pl.ds(POOL_PAD + r0 - j, chunk), lanes]
                pooled.append(acc / cnt - cur)
            pooled = pooled[0] if g_elems == 1 else jnp.concatenate(pooled, axis=0)
            y = _dot(pooled.astype(BF16), pw_ref[g].astype(BF16))
            y = y * ps_ref[:, lanes]
            for e in range(g_elems):
                o_ref[e, pl.ds(r0, chunk), lanes] = y[e * chunk:(e + 1) * chunk].astype(o_ref.dtype)


def _pool_mix(u, hist, pos0, pool_w_l, pool_scale_l, g_elems, chunk):
    b, t_rows, _ = u.shape
    per_elem = lambda rows: pl.BlockSpec((g_elems, rows, C_POOL), lambda i: (i, 0, 0))
    in_specs = [per_elem(t_rows)] + ([per_elem(POOL_PAD)] if hist is not None else []) + [
        pl.BlockSpec((len(POOL_WINDOWS), POOL_GC, POOL_GC), lambda i: (0, 0, 0)),
        pl.BlockSpec((1, C_POOL), lambda i: (0, 0))]
    args = (u,) + ((hist,) if hist is not None else ()) + (pool_w_l, pool_scale_l)
    return pl.pallas_call(
        functools.partial(_pool_kernel, pos0, t_rows, chunk, hist is not None),
        grid=(b // g_elems,),
        in_specs=in_specs,
        out_specs=per_elem(t_rows),
        out_shape=jax.ShapeDtypeStruct((b, t_rows, C_POOL), F32),
        scratch_shapes=[pltpu.VMEM((g_elems, POOL_PAD + t_rows, C_POOL), F32)],
        compiler_params=pltpu.CompilerParams(vmem_limit_bytes=VMEM_LIMIT),
        name="pool_mix",
    )(*args)


def _bias_of_dist(dist, rb_ref, h):
    n = jnp.maximum(dist, 0)
    max_exact = N_BUCKETS // 2
    large = max_exact + (jnp.log(jnp.maximum(n, 1).astype(F32) / max_exact)
                         / math.log(MAX_DISTANCE / max_exact) * (N_BUCKETS - max_exact)).astype(I32)
    large = jnp.minimum(large, N_BUCKETS - 1)
    bucket = jnp.where(n < max_exact, n, large)
    out = jnp.zeros(dist.shape, F32)
    for b in range(N_BUCKETS):
        out = jnp.where(bucket == b, rb_ref[b, h], out)
    return out


def _bias_kernel(blk, rb_ref, tb_ref, s_far_ref, s_last_ref, s_new_ref):
    key = lax.broadcasted_iota(I32, (blk, blk), 0)
    qry = lax.broadcasted_iota(I32, (blk, blk), 1)
    t = lax.broadcasted_iota(I32, (SUBLANES, LANES), 0)
    jj = lax.broadcasted_iota(I32, (SUBLANES, LANES), 1)
    for h in range(N_HEADS):
        tb_ref[0, h] = _bias_of_dist(jnp.full((blk, blk), 2 * blk, I32), rb_ref, h)
        tb_ref[1, h] = _bias_of_dist(blk + qry - key, rb_ref, h)
        tb_ref[2, h] = _bias_of_dist(qry - key, rb_ref, h)
        rows = pl.ds(h * SUBLANES, SUBLANES)
        s_far_ref[rows, :] = _bias_of_dist(jnp.full((SUBLANES, LANES), 2 * PAGE_SIZE, I32), rb_ref, h)
        s_last_ref[rows, :] = _bias_of_dist(PAGE_SIZE + t - jj, rb_ref, h)
        s_new_ref[rows, :] = _bias_of_dist(t - jj, rb_ref, h)


def _bias_tables(rel_bias, blk):
    sm = jax.ShapeDtypeStruct((N_HEADS * SUBLANES, LANES), F32)
    return pl.pallas_call(
        functools.partial(_bias_kernel, blk),
        in_specs=[pl.BlockSpec(memory_space=pltpu.SMEM)],
        out_shape=[jax.ShapeDtypeStruct((3, N_HEADS, blk, blk), F32), sm, sm, sm],
        compiler_params=pltpu.CompilerParams(vmem_limit_bytes=VMEM_LIMIT),
        name="bias_tables",
    )(rel_bias)


def _kth_start(shape):
    return jnp.full(shape, INT_MIN, I32), jnp.zeros(shape, F32)


def _kth_rounds(count_ge_key, k, carry, first_round, last_round, digit_bits=1):
    def body(r, carry):
        tkey, nge = carry
        shift = 32 - digit_bits * (r + 1)
        best_key, best_cnt = tkey, nge
        for j in range(1, 1 << digit_bits):
            ckey = tkey + lax.shift_left(jnp.int32(j), shift)
            cnt = count_ge_key(ckey)
            ok = cnt >= k
            best_key = jnp.where(ok, ckey, best_key)
            best_cnt = jnp.where(ok, cnt, best_cnt)
        return best_key, best_cnt

    return lax.fori_loop(first_round, last_round, body, carry)


def _kth_finish(carry):
    tkey, nge = carry
    return jnp.where(tkey == INT_MIN, NEG_INF, _key_to_float(tkey)), nge


def _kth_largest(count_ge, k, shape, digit_bits=1):
    carry = _kth_rounds(lambda ckey: count_ge(_key_to_float(ckey)), k, _kth_start(shape),
                        0, 32 // digit_bits, digit_bits)
    return _kth_finish(carry)


def _tie_limit(count_eq_below, need, shape, bits):
    def body(it, lim):
        cand = lim + lax.shift_left(jnp.int32(1), bits - 1 - it)
        return jnp.where(count_eq_below(cand) <= need, cand, lim)

    return lax.fori_loop(0, bits, body, jnp.zeros(shape, I32))


def _select_mask(s, idx, thr, lim):
    tie = jnp.where(s == thr, jnp.where(idx < lim, 0.0, NEG_INF), NEG_INF)
    m = jnp.where(s > thr, 0.0, tie)
    return jnp.where(s == NEG_INF, NEG_INF, m)


def _attn_prompt_kernel(k_sel, blk, seq,
                        qt_ref, qit_ref, wit_ref, kb_ref, kib_ref, vbt_ref, tb_ref,
                        o_ref,
                        score_ref, trunc_ref, lg_ref, mrun_ref, lsum_ref, acc_ref, thr_ref, lim_ref):
    qb = pl.program_id(1)
    n_chunks = qb + 1
    rep = (SUBLANES, blk)
    idx_bits = max(1, (seq - 1).bit_length()) + 1

    key_row = lax.broadcasted_iota(I32, (blk, blk), 0)
    qry_col = lax.broadcasted_iota(I32, (blk, blk), 1)

    def chunk_keys(c):
        return pl.ds(pl.multiple_of(c * blk, blk), blk)

    def all_sublanes(x, op):
        return jnp.broadcast_to(op(x, axis=0, keepdims=True), rep)

    w_idx = wit_ref[...] * IDX_HEADS ** -0.5 * IDX_DIM ** -0.5

    def score_chunk(c, carry):
        keys = chunk_keys(c)
        kic = kib_ref[keys, :][:, :IDX_DIM]
        s = jnp.zeros((blk, blk), F32)
        for h in range(IDX_HEADS):
            d = _dot(kic, qit_ref[h * IDX_DIM:(h + 1) * IDX_DIM, :])
            s = s + w_idx[h:h + 1, :] * jnp.maximum(d, 0.0)
        s = jnp.where(key_row + c * blk <= qry_col + qb * blk, s, NEG_INF)
        score_ref[keys, :] = s
        hi = lax.bitcast_convert_type(s, I32) & jnp.int32(-(1 << 16))
        trunc_ref[keys, :] = lax.bitcast_convert_type(hi, F32).astype(BF16)
        return carry

    lax.fori_loop(0, n_chunks, score_chunk, 0)

    def count_where(pred):
        def body(c, acc):
            return acc + _fold_tiles(pred(score_ref[chunk_keys(c), :], key_row + c * blk))
        acc = lax.fori_loop(0, n_chunks, body, jnp.zeros(rep, F32))
        return all_sublanes(acc, jnp.sum)

    def count_upper_half(ckey):
        bits = ckey ^ ((ckey >> 31) & jnp.int32(0x7FFFFFFF))
        cf = lax.bitcast_convert_type(bits & jnp.int32(-(1 << 16)), F32)
        cb = jnp.concatenate([cf, cf], axis=0).astype(BF16)
        one, zero = jnp.ones((BF16_ROWS, blk), BF16), jnp.zeros((BF16_ROWS, blk), BF16)

        def body(c, acc):
            t = trunc_ref[chunk_keys(c), :]
            return acc + _tree([jnp.where(t[r * BF16_ROWS:(r + 1) * BF16_ROWS] >= cb, one, zero)
                                for r in range(blk // BF16_ROWS)])
        acc = lax.fori_loop(0, n_chunks, body, zero)
        return all_sublanes(acc.astype(F32), jnp.sum)

    carry = _kth_rounds(count_upper_half, k_sel, _kth_start(rep), 0, 16)
    carry = _kth_rounds(
        lambda ckey: count_where(lambda s, idx: jnp.where(s >= _key_to_float(ckey)[0:1], 1.0, 0.0)),
        k_sel, carry, 16, 32)
    thr, nge = _kth_finish(carry)
    thr_ref[...] = thr
    lim_ref[...] = jnp.full(rep, 1 << idx_bits, I32)

    @pl.when(jnp.max(nge) > k_sel)
    def _():
        n_gt = count_where(lambda s, idx: jnp.where(s > thr[0:1], 1.0, 0.0))
        need = k_sel - n_gt
        lim_ref[...] = _tie_limit(
            lambda cand: count_where(
                lambda s, idx: jnp.where(s == thr[0:1], jnp.where(idx < cand[0:1], 1.0, 0.0), 0.0)),
            need, rep, idx_bits)

    def mask_chunk(c, carry):
        keys = chunk_keys(c)
        score_ref[keys, :] = _select_mask(score_ref[keys, :], key_row + c * blk,
                                          thr_ref[0:1, :], lim_ref[0:1, :])
        return carry

    lax.fori_loop(0, n_chunks, mask_chunk, 0)

    mrun_ref[...] = jnp.full(mrun_ref.shape, NEG_INF, F32)

    def logits_chunk(c, carry):
        keys = chunk_keys(c)
        madd = score_ref[keys, :]
        table = jnp.clip(c - (qb - 2), 0, 2)
        for h in range(N_HEADS):
            hs = slice(h * HEAD_DIM, (h + 1) * HEAD_DIM)
            s = _dot(kb_ref[keys, hs], qt_ref[hs, :]) + tb_ref[table, h] + madd
            lg_ref[h, keys, :] = s
            mrun_ref[h] = jnp.maximum(mrun_ref[h], _fold_tiles(s, jnp.maximum))
        return carry

    lax.fori_loop(0, n_chunks, logits_chunk, 0)

    for h in range(N_HEADS):
        mrun_ref[h] = all_sublanes(mrun_ref[h], jnp.max)
    lsum_ref[...] = jnp.zeros(lsum_ref.shape, F32)
    acc_ref[...] = jnp.zeros(acc_ref.shape, F32)

    def pv_chunk(c, carry):
        keys = chunk_keys(c)
        for h in range(N_HEADS):
            hs = slice(h * HEAD_DIM, (h + 1) * HEAD_DIM)
            p = jnp.exp(lg_ref[h, keys, :] - mrun_ref[h, 0:1, :])
            lsum_ref[h] += _fold_tiles(p)
            acc_ref[h] += _dot(vbt_ref[hs, keys], p.astype(BF16))
        return carry

    lax.fori_loop(0, n_chunks, pv_chunk, 0)

    outs =[acc_ref[h] / jnp.sum(lsum_ref[h], axis=0, keepdims=True) for h in range(N_HEADS)]
    o_ref[...] = jnp.concatenate(outs, axis=0).T.astype(o_ref.dtype)


def _attn_prompt(cfg, k_sel, qt, qit, wit, kb, kib, vbt, tb):
    blk, seq = cfg.blk, cfg.seq
    nq = seq // blk
    qspec = lambda width: pl.BlockSpec((None, width, blk), lambda b, i: (b, 0, i))
    kspec = lambda width: pl.BlockSpec((seq, width), lambda b, i: (b, 0))
    return pl.pallas_call(
        functools.partial(_attn_prompt_kernel, k_sel, blk, seq),
        grid=(cfg.batch, nq),
        in_specs=[qspec(D_ATT), qspec(IDX_HEADS * IDX_DIM), qspec(IDX_HEADS),
                  kspec(D_ATT), kspec(TAIL), pl.BlockSpec((None, D_ATT, seq), lambda b, i: (b, 0, 0)),
                  pl.BlockSpec((3, N_HEADS, blk, blk), lambda b, i: (0, 0, 0, 0), pipeline_mode=pl.Buffered(1))],
        out_specs=pl.BlockSpec((blk, D_ATT), lambda b, i: (b * nq + i, 0)),
        out_shape=jax.ShapeDtypeStruct((cfg.batch * seq, D_ATT), BF16),
        scratch_shapes=[pltpu.VMEM((seq, blk), F32),
                        pltpu.VMEM((seq, blk), BF16),
                        pltpu.VMEM((N_HEADS, seq, blk), F32),
                        pltpu.VMEM((N_HEADS, SUBLANES, blk), F32),
                        pltpu.VMEM((N_HEADS, SUBLANES, blk), F32),
                        pltpu.VMEM((N_HEADS, HEAD_DIM, blk), F32),
                        pltpu.VMEM((SUBLANES, blk), F32),
                        pltpu.VMEM((SUBLANES, blk), I32)],
        compiler_params=pltpu.CompilerParams(vmem_limit_bytes=VMEM_LIMIT),
        name="attn_prompt",
    )(qt, qit, wit, kb, kib, vbt, tb)


def _pad_rows(x, rows):
    return jnp.concatenate([x, jnp.zeros((rows - x.shape[0], x.shape[1]), x.dtype)], axis=0)


def _fetch_pages(pt_ref, layer, n_pages, caches):
    i = pl.program_id(0)
    slot = i % 2

    def copy(cache, b, s, p):
        hbm, buf, sem = cache
        return pltpu.make_async_copy(hbm.at[layer, pt_ref[b, p]], buf.at[s, p], sem.at[s])

    def start(b, s):
        for p in range(n_pages):
            for cache in caches:
                copy(cache, b, s, p).start()

    @pl.when(i == 0)
    def _():
        start(0, 0)

    @pl.when(i + 1 < pl.num_programs(0))
    def _():
        start(i + 1, 1 - slot)

    for p in range(n_pages):
        for cache in caches:
            copy(cache, i, slot, p).wait()
    return slot


def _page_scratch(n_pages, rows):
    return [pltpu.VMEM((2, n_pages, rows, PAGE_SIZE), F32), pltpu.SemaphoreType.DMA((2,))]


def _score_sample_kernel(layer, n_pages, pt_ref, qi_ref, wrep_ref, tailn_ref, ki_hbm, o_ref, ki_buf, ki_sem):
    slot = _fetch_pages(pt_ref, layer, n_pages, [(ki_hbm, ki_buf, ki_sem)])
    shape = (SUBLANES, LANES)
    qi = qi_ref[...]
    w = wrep_ref[...]

    def score_block(dots):
        return _fold_tiles(w * jnp.maximum(dots * IDX_DIM ** -0.5, 0.0))

    for p in range(n_pages):
        o_ref[:, p * PAGE_SIZE:(p + 1) * PAGE_SIZE] = score_block(_dot(qi, ki_buf[slot, p].astype(BF16)))
    ki_new = _pad_rows(tailn_ref[...][:, :IDX_DIM], PAGE_SIZE).astype(BF16)
    causal = lax.broadcasted_iota(I32, shape, 1) <= lax.broadcasted_iota(I32, shape, 0)
    o_ref[:, n_pages * PAGE_SIZE:] = jnp.where(causal, score_block(_nt_dot(qi, ki_new)), NEG_INF)


def _score_sample(cfg, layer, page_table, qi_ht, wrep, tail_new, cache_ki):
    b, t_rows = cfg.dec_batch, cfg.dec_seq
    n_pages = cfg.past_len // PAGE_SIZE
    rows = N_HEADS * t_rows
    per_b = lambda r, w: pl.BlockSpec((None, r, w), lambda i, pt: (i, 0, 0))
    n_keys = (n_pages + 1) * PAGE_SIZE
    return pl.pallas_call(
        functools.partial(_score_sample_kernel, layer, n_pages),
        grid_spec=pltpu.PrefetchScalarGridSpec(
            num_scalar_prefetch=1, grid=(b,),
            in_specs=[per_b(rows, IDX_DIM), per_b(rows, LANES), per_b(t_rows, TAIL),
                      pl.BlockSpec(memory_space=pl.ANY)],
            out_specs=per_b(t_rows, n_keys),
            scratch_shapes=_page_scratch(n_pages, IDX_DIM)),
        out_shape=jax.ShapeDtypeStruct((b, t_rows, n_keys), F32),
        name="score_sample",
    )(page_table, qi_ht, wrep, tail_new, cache_ki)


def _threshold_kernel(k_sel, s_ref, thr_ref, lim_ref):
    n_keys, cols = s_ref.shape
    rep = (SUBLANES, cols)
    idx_bits = (n_keys - 1).bit_length() + 1
    key_row = lax.broadcasted_iota(I32, (LANES, cols), 0)

    def count_where(pred):
        acc = _tree([_fold_tiles(pred(s_ref[c * LANES:(c + 1) * LANES, :], key_row + c * LANES))
                     for c in range(n_keys // LANES)])
        return jnp.broadcast_to(jnp.sum(acc, axis=0, keepdims=True), rep)

    thr, nge = _kth_largest(lambda cf: count_where(lambda s, idx: jnp.where(s >= cf[0:1], 1.0, 0.0)), k_sel, rep)
    thr_ref[...] = thr
    lim_ref[...] = jnp.full(rep, 1 << idx_bits, I32)

    @pl.when(jnp.max(nge) > k_sel)
    def _():
        n_gt = count_where(lambda s, idx: jnp.where(s > thr[0:1], 1.0, 0.0))
        need = k_sel - n_gt
        lim_ref[...] = _tie_limit(
            lambda cand: count_where(
                lambda s, idx: jnp.where(s == thr[0:1], jnp.where(idx < cand[0:1], 1.0, 0.0), 0.0)),
            need, rep, idx_bits)


def _threshold_cols(k_sel, scores_t, cols):
    n_keys, n_q = scores_t.shape
    spec = pl.BlockSpec((SUBLANES, cols), lambda i: (0, i))
    return pl.pallas_call(
        functools.partial(_threshold_kernel, k_sel),
        grid=(n_q // cols,),
        in_specs=[pl.BlockSpec((n_keys, cols), lambda i: (0, i))],
        out_specs=[spec, spec],
        out_shape=[jax.ShapeDtypeStruct((SUBLANES, n_q), F32), jax.ShapeDtypeStruct((SUBLANES, n_q), I32)],
        name="threshold_cols",
    )(scores_t)


def _attn_sample_kernel(layer, n_pages, pt_ref,
                        sc_ref, thr_ref, lim_ref, q_ref, kn_ref, vn_ref, bfar_ref, blast_ref, bnew_ref,
                        k_hbm, v_hbm, o_ref, k_buf, k_sem, v_buf, v_sem):
    slot = _fetch_pages(pt_ref, layer, n_pages, [(k_hbm, k_buf, k_sem), (v_hbm, v_buf, v_sem)])
    t_rows = SUBLANES
    n_blocks = n_pages + 1
    lane = lax.broadcasted_iota(I32, (t_rows, LANES), 1)
    thr, lim = thr_ref[...], lim_ref[...]
    masks = [_select_mask(sc_ref[:, b * PAGE_SIZE:(b + 1) * PAGE_SIZE], lane + b * PAGE_SIZE, thr, lim)
             for b in range(n_blocks)]

    def pad_new(x):
        return _pad_rows(x, PAGE_SIZE)

    rows = N_HEADS * t_rows
    same_head = (lax.broadcasted_iota(I32, (rows, D_ATT), 0) // t_rows
                 == lax.broadcasted_iota(I32, (rows, D_ATT), 1) // HEAD_DIM)
    q_bd = jnp.where(same_head, jnp.concatenate([q_ref[...]] * N_HEADS, axis=0), 0.0).astype(BF16)

    def tile_heads(m):
        return jnp.concatenate([m] * N_HEADS, axis=0)

    logits = []
    for p in range(n_pages):
        bias = blast_ref[...] if p == n_pages - 1 else bfar_ref[...]
        logits.append(_dot(q_bd, k_buf[slot, p].astype(BF16)) + bias + tile_heads(masks[p]))
    logits.append(_nt_dot(q_bd, pad_new(kn_ref[...]).astype(BF16)) + bnew_ref[...] + tile_heads(masks[n_pages]))

    m = logits[0]
    for lg in logits[1:]:
        m = jnp.maximum(m, lg)
    m = _lane_rep(jnp.max(m, axis=1, keepdims=True))
    lsum = jnp.zeros((rows, LANES), F32)
    acc = jnp.zeros((rows, D_ATT), F32)
    for b in range(n_blocks):
        p = jnp.exp(logits[b] - m)
        lsum = lsum + p
        if b < n_pages:
            acc = acc + _nt_dot(p.astype(BF16), v_buf[slot, b].astype(BF16))
        else:
            acc = acc + _dot(p.astype(BF16), pad_new(vn_ref[...]).astype(BF16))
    out = jnp.where(same_head, acc / jnp.sum(lsum, axis=1, keepdims=True), 0.0)
    o_ref[...] = _fold_tiles(out)


def _attn_sample(cfg, layer, page_table, scores, thr_rep, lim_rep, q, k_new, v_new,
                 s_far, s_last, s_new, cache_k, cache_v):
    b = cfg.dec_batch
    t_rows = cfg.dec_seq
    n_pages = cfg.past_len // PAGE_SIZE
    rows = N_HEADS * t_rows
    per_b = lambda r, w: pl.BlockSpec((None, r, w), lambda i, pt: (i, 0, 0))
    const = lambda r, w: pl.BlockSpec((r, w), lambda i, pt: (0, 0))
    in_hbm = pl.BlockSpec(memory_space=pl.ANY)
    in_specs = [per_b(t_rows, scores.shape[2]), per_b(t_rows, LANES), per_b(t_rows, LANES),
                per_b(t_rows, D_ATT), per_b(t_rows, D_ATT), per_b(t_rows, D_ATT),
                const(rows, LANES), const(rows, LANES), const(rows, LANES), in_hbm, in_hbm]
    return pl.pallas_call(
        functools.partial(_attn_sample_kernel, layer, n_pages),
        grid_spec=pltpu.PrefetchScalarGridSpec(
            num_scalar_prefetch=1, grid=(b,), in_specs=in_specs, out_specs=per_b(t_rows, D_ATT),
            scratch_shapes=_page_scratch(n_pages, D_ATT) + _page_scratch(n_pages, D_ATT)),
        out_shape=jax.ShapeDtypeStruct((b, t_rows, D_ATT), F32),
        compiler_params=pltpu.CompilerParams(vmem_limit_bytes=VMEM_LIMIT),
        name="attn_sample",
    )(page_table, scores, thr_rep, lim_rep, q, k_new, v_new, s_far, s_last, s_new, cache_k, cache_v)


def _forward(cfg, x_prompt, x_sample, cache_k, cache_v, cache_kidx, state_pool, page_table, c_prompt,
             c_sample, rel_bias, ada_w, ada_b, ln1_pre, ln1_post, ln2_pre, ln2_post, w_in, pool_w,
             pool_scale, w_out, w_ff1, w_ff2):
    depth = ada_w.shape[0]
    bp, seq, bs, ts = cfg.batch, cfg.seq, cfg.dec_batch, cfg.dec_seq
    assert ts == SUBLANES and seq % cfg.blk == 0 and cfg.past_len % PAGE_SIZE == 0
    assert cfg.blk >= PAGE_SIZE and seq >= TOPK_MAX and seq // BF16_ROWS <= 256
    k_prompt = min(TOPK_MAX, seq // 4)
    k_sample = min(TOPK_MAX, (cfg.past_len + ts) // 4)
    rp, rs = bp * seq, bs * ts
    tm_s = min(IN_SUB_ROWS, rs)

    mod = _ada_mod(jnp.concatenate([c_prompt, c_sample], axis=0), ada_w, ada_b)
    tb, s_far, s_last, s_new = _bias_tables(rel_bias, cfg.blk)

    n_phys = cache_k.shape[1]
    cache_kt = cache_k.transpose(0, 1, 3, 4, 2).reshape(depth, n_phys, D_ATT, PAGE_SIZE)
    cache_vt = cache_v.transpose(0, 1, 3, 4, 2).reshape(depth, n_phys, D_ATT, PAGE_SIZE)
    cache_kit = cache_kidx.transpose(0, 1, 3, 2)

    xp = x_prompt.reshape(rp, D_MODEL)
    xs = x_sample.reshape(rs, D_MODEL)
    state_p = ()
    pool_p = []
    outs_s = [[], [], [], []]
    for l in range(depth):
        w_in_b = jnp.pad(w_in[l], ((0, 0), (0, D_IN_PAD - D_IN))).astype(BF16)
        wo_b, w1_b, w2_b = w_out[l].astype(BF16), w_ff1[l].astype(BF16), w_ff2[l].astype(BF16)
        lnrow = lambda a: a[l].reshape(1, D_MODEL)
        mod_p = mod[l, :bp].reshape(bp, 1, 6 * D_MODEL)
        mod_s = jnp.repeat(mod[l, bp:], ts, axis=0)
        psc = pool_scale[l].reshape(1, C_POOL)

        tm_in = min(seq, cfg.tm if state_p else max(IN_SUB_ROWS, cfg.tm // depth))
        u, kb, tailb, *state_p, wit, qt, qit, vbt = _in_proj(
            xp, mod_p, seq, lnrow(ln1_pre), w_in_b, tm_in, layer=l, depth=depth, carried=tuple(state_p))
        u3 = u.reshape(bp, seq, C_POOL)
        y_pool = _pool_mix(u3, None, 0, pool_w[l], psc, 1, min(256, seq)).reshape(rp, C_POOL)
        y_att = _attn_prompt(cfg, k_prompt, qt, qit, wit, kb, tailb, vbt, tb)
        xp = _out_ffn(xp, y_pool, y_att, mod_p, seq, lnrow(ln1_post), lnrow(ln2_pre), lnrow(ln2_post),
                      wo_b, w1_b, w2_b, min(cfg.tm_out, seq))
        pool_p.append(u3[:, seq - POOL_HIST:])

        u, q, qi, tail, k, v = _in_proj(xs, mod_s, 0, lnrow(ln1_pre), w_in_b, tm_s)
        u3 = u.reshape(bs, ts, C_POOL)
        hist = jnp.concatenate([jnp.zeros((bs, POOL_PAD - POOL_HIST, C_POOL), F32), state_pool[l]], axis=1)
        y_pool = _pool_mix(u3, hist, cfg.past_len, pool_w[l], psc, min(16, bs), ts).reshape(rs, C_POOL)
        qi_ht = qi.reshape(bs, ts, IDX_HEADS, IDX_DIM).transpose(0, 2, 1, 3).reshape(bs, IDX_HEADS * ts, IDX_DIM)
        wi = tail[:, OFF_WI - OFF_KI:OFF_WI - OFF_KI + IDX_HEADS].reshape(bs, ts, IDX_HEADS) * IDX_HEADS ** -0.5
        wrep = jnp.broadcast_to(wi.transpose(0, 2, 1).reshape(bs, IDX_HEADS * ts, 1), (bs, IDX_HEADS * ts, LANES))
        scores = _score_sample(cfg, l, page_table, qi_ht, wrep, tail.reshape(bs, ts, TAIL), cache_kit)
        thr, lim = _threshold_cols(k_sample, scores.transpose(2, 0, 1).reshape(scores.shape[2], rs),
                                   min(256, rs))
        per_query = lambda a: jnp.broadcast_to(a[0].reshape(bs, ts, 1), (bs, ts, LANES))
        y_att = _attn_sample(cfg, l, page_table, scores, per_query(thr), per_query(lim),
                             q.astype(F32).reshape(bs, ts, D_ATT),
                             k.reshape(bs, ts, D_ATT), v.reshape(bs, ts, D_ATT),
                             s_far, s_last, s_new, cache_kt, cache_vt).reshape(rs, D_ATT)
        xs = _out_ffn(xs, y_pool, y_att, mod_s, 0, lnrow(ln1_post), lnrow(ln2_pre), lnrow(ln2_post),
                      wo_b, w1_b, w2_b, min(OUT_SUB_ROWS, rs))
        for lst, a in zip(outs_s, (k.reshape(bs, ts, N_HEADS, HEAD_DIM), v.reshape(bs, ts, N_HEADS, HEAD_DIM),
                                   tail[:, :IDX_DIM].reshape(bs, ts, IDX_DIM),
                                   jnp.concatenate([hist, u3], axis=1)[:, -POOL_HIST:])):
            lst.append(a)

    kt, vt, kit = state_p
    heads_last = lambda a: a.reshape(depth, bp, N_HEADS, HEAD_DIM, seq).transpose(0, 1, 4, 2, 3)
    return (xp.reshape(bp, seq, D_MODEL), xs.reshape(bs, ts, D_MODEL),
            heads_last(kt), heads_last(vt), kit.transpose(0, 1, 3, 2), jnp.stack(pool_p),
            *[jnp.stack(a) for a in outs_s])


def kernel(x_prompt, x_sample, cache_k, cache_v, cache_kidx, state_pool, page_table, c_prompt, c_sample,
           rel_bias, ada_w, ada_b, ln1_pre, ln1_post, ln2_pre, ln2_post, w_in, pool_w, pool_scale,
           w_out, w_ff1, w_ff2):
    cfg = Cfg(batch=x_prompt.shape[0], seq=x_prompt.shape[1], dec_batch=x_sample.shape[0],
              dec_seq=x_sample.shape[1], past_len=page_table.shape[1] * PAGE_SIZE, blk=256, tm=1024,
              tm_out=1024)
    return _forward(cfg, x_prompt, x_sample, cache_k, cache_v, cache_kidx, state_pool, page_table, c_prompt,
                    c_sample, rel_bias, ada_w, ada_b, ln1_pre, ln1_post, ln2_pre, ln2_post, w_in, pool_w,
                    pool_scale, w_out, w_ff1, w_ff2)
```

```python
import functools
import math
from typing import NamedTuple

import jax
import jax.numpy as jnp
from jax import lax
from jax.experimental import pallas as pl
from jax.experimental.pallas import tpu as pltpu

F32 = jnp.float32
BF16 = jnp.bfloat16
I32 = jnp.int32

D_MODEL = 1024
C_POOL = 512
D_ATT = 512
HEAD_DIM = 64
N_HEADS = 8
IDX_HEADS = 8
IDX_DIM = 64
POOL_WINDOWS = (2, 4, 8, 16)
POOL_GC = C_POOL // len(POOL_WINDOWS)
POOL_HIST = max(POOL_WINDOWS) - 1
TOPK_MAX = 256
N_BUCKETS = 32
MAX_DISTANCE = 128
D_FF = 4 * D_MODEL
EPS = 1e-6
PAGE_SIZE = 128
OFF_Q = C_POOL
OFF_K = OFF_Q + D_ATT
OFF_V = OFF_K + D_ATT
OFF_QI = OFF_V + D_ATT
OFF_KI = OFF_QI + IDX_HEADS * IDX_DIM
OFF_WI = OFF_KI + IDX_DIM
D_IN = OFF_WI + IDX_HEADS

LANES = 128
SUBLANES = 8
BF16_ROWS = 2 * SUBLANES
D_IN_PAD = ((D_IN + LANES - 1) // LANES) * LANES
TAIL = D_IN_PAD - OFF_KI
POOL_PAD = 16
OUT_SUB_ROWS = 256
IN_SUB_ROWS = 256
SCORE_PAGE_SLOTS = 4
ATTN_PAGE_SLOTS = 3
VMEM_LIMIT = 56 * 1024 * 1024

NEG_INF = float("-inf")
INT_MIN = -(2 ** 31)


class Cfg(NamedTuple):
    batch: int
    seq: int
    dec_batch: int
    dec_seq: int
    past_len: int
    blk: int
    tm: int
    tm_out: int


def _rms(x, g):
    ms = jnp.mean(x * x, axis=-1, keepdims=True)
    return x * lax.rsqrt(ms + EPS) * g


def _nt_dot(a, b):
    return lax.dot_general(a, b, (((1,), (1,)), ((), ())), preferred_element_type=F32)


def _dot(a, b):
    return jnp.dot(a, b, preferred_element_type=F32)


def _key_to_float(key):
    bits = key ^ ((key >> 31) & jnp.int32(0x7FFFFFFF))
    return lax.bitcast_convert_type(bits, F32)


def _lane_rep(col, width=LANES):
    return jnp.broadcast_to(col, (col.shape[0], width))


def _fold_tiles(x, op=jnp.add):
    return _tree([x[r * SUBLANES:(r + 1) * SUBLANES] for r in range(x.shape[0] // SUBLANES)], op)


def _tree(parts, op=jnp.add):
    while len(parts) > 1:
        parts = [op(parts[i], parts[i + 1]) if i + 1 < len(parts) else parts[i] for i in range(0, len(parts), 2)]
    return parts[0]


def _ada_kernel(c_ref, w_ref, b_ref, o_ref):
    o_ref[...] = _dot(c_ref[...].astype(BF16), w_ref[...].astype(BF16)) + b_ref[...]


def _ada_mod(c_all, ada_w, ada_b):
    depth, d, n = ada_w.shape
    rc = c_all.shape[0]
    tn = 1536
    return pl.pallas_call(
        _ada_kernel,
        grid=(depth, n // tn),
        in_specs=[pl.BlockSpec((rc, d), lambda l, j: (0, 0)),
                  pl.BlockSpec((None, d, tn), lambda l, j: (l, 0, j)),
                  pl.BlockSpec((None, 1, tn), lambda l, j: (l, 0, j))],
        out_specs=pl.BlockSpec((None, rc, tn), lambda l, j: (l, 0, j)),
        out_shape=jax.ShapeDtypeStruct((depth, rc, n), F32),
        name="ada_mod",
    )(c_all, ada_w, ada_b.reshape(depth, 1, n))


def _mod_spec(per_batch_rows, tm, chunk):
    if per_batch_rows:
        tiles = per_batch_rows // tm
        return pl.BlockSpec((None, 1, D_MODEL), lambda i: (i // tiles, 0, chunk))
    return pl.BlockSpec((tm, D_MODEL), lambda i: (i, chunk))


def _in_kernel(transposed, n_carried, n_sub, layer, x_ref, sh_ref, sc_ref, ln_ref, w_ref, *refs):
    u_ref, *refs = refs[n_carried:]
    sub = x_ref.shape[0] // n_sub
    mod = lambda ref, rows: ref[...] if ref.shape[0] == 1 else ref[rows, :]

    def put_state(ref, cols, val):
        if len(ref.shape) == 2:
            ref[:, cols] = val
        else:
            for l in range(ref.shape[0]):
                ref[l, :, cols] = val if l == layer else jnp.zeros(val.shape, val.dtype)

    def project(i):
        rows = pl.ds(i * sub, sub)
        h = _rms(x_ref[rows, :], ln_ref[...]) * (1.0 + mod(sc_ref, rows)) + mod(sh_ref, rows)
        return _dot(h.astype(BF16), w_ref[...])

    def emit(i, z):
        rows = pl.ds(i * sub, sub)
        u_ref[rows, :] = z[:, :OFF_Q]
        q = z[:, OFF_Q:OFF_K] * HEAD_DIM ** -0.5
        k = z[:, OFF_K:OFF_V]
        v = z[:, OFF_V:OFF_QI]
        qi = z[:, OFF_QI:OFF_KI]
        tail = z[:, OFF_KI:]
        if transposed:
            kb_ref, tailb_ref, kt_ref, vt_ref, kit_ref, wit_ref, qt_ref, qit_ref, vbt_ref = refs
            cols = pl.ds(i * sub, sub)
            kb_ref[rows, :] = k.astype(BF16)
            tailb_ref[rows, :] = tail.astype(BF16)
            put_state(kt_ref, cols, k.T)
            vt = v.T
            put_state(vt_ref, cols, vt)
            vbt_ref[:, cols] = vt.astype(BF16)
            tail_t = tail.T
            put_state(kit_ref, cols, tail_t[:IDX_DIM])
            wit_ref[:, cols] = tail_t[OFF_WI - OFF_KI:OFF_WI - OFF_KI + IDX_HEADS]
            qt_ref[:, cols] = q.T.astype(BF16)
            qit_ref[:, cols] = qi.T.astype(BF16)
        else:
            q_ref, qi_ref, tail_ref, k_ref, v_ref = refs
            q_ref[rows, :] = q.astype(BF16)
            qi_ref[rows, :] = qi.astype(BF16)
            tail_ref[rows, :] = tail
            k_ref[rows, :] = k
            v_ref[rows, :] = v

    z = project(0)
    for i in range(n_sub):
        z_next = project(i + 1) if i + 1 < n_sub else None
        emit(i, z)
        z = z_next


def _in_proj(x, mod, per_batch_rows, ln, w_in_b, tm, layer=0, depth=1, carried=()):
    r = x.shape[0]
    row = lambda width: pl.BlockSpec((tm, width), lambda i: (i, 0))
    const = lambda shape: pl.BlockSpec(shape, lambda i: (0, 0))
    in_specs = [row(D_MODEL), _mod_spec(per_batch_rows, tm, 0), _mod_spec(per_batch_rows, tm, 1),
                const((1, D_MODEL)), const((D_MODEL, D_IN_PAD))]
    aliases = {}
    if per_batch_rows:
        tiles = per_batch_rows // tm
        nb = r // per_batch_rows
        rows_out = [(C_POOL, F32), (D_ATT, BF16), (TAIL, BF16)]
        state_out = [D_ATT, D_ATT, IDX_DIM]
        cols_out = [(IDX_HEADS, F32), (D_ATT, BF16), (IDX_HEADS * IDX_DIM, BF16), (D_ATT, BF16)]
    else:
        rows_out = [(C_POOL, F32), (D_ATT, BF16), (IDX_HEADS * IDX_DIM, BF16), (TAIL, F32), (D_ATT, F32), (D_ATT, F32)]
        state_out, cols_out = [], []
    out_specs = [row(w) for w, _ in rows_out]
    out_shape = [jax.ShapeDtypeStruct((r, w), dt) for w, dt in rows_out]
    for n, width in enumerate(state_out):
        if carried:
            aliases[len(in_specs)] = len(out_specs)
            in_specs.append(pl.BlockSpec(memory_space=pl.ANY))
            out_specs.append(pl.BlockSpec((None, None, width, tm), lambda i: (layer, i // tiles, 0, i % tiles)))
        else:
            out_specs.append(pl.BlockSpec((depth, None, width, tm), lambda i: (0, i // tiles, 0, i % tiles)))
        out_shape.append(jax.ShapeDtypeStruct((depth, nb, width, per_batch_rows), F32))
    for width, dt in cols_out:
        out_specs.append(pl.BlockSpec((None, width, tm), lambda i: (i // tiles, 0, i % tiles)))
        out_shape.append(jax.ShapeDtypeStruct((nb, width, per_batch_rows), dt))
    return pl.pallas_call(
        functools.partial(_in_kernel, bool(per_batch_rows), len(carried), max(1, tm // IN_SUB_ROWS), layer),
        grid=(r // tm,),
        in_specs=in_specs,
        out_specs=out_specs,
        out_shape=out_shape,
        input_output_aliases=aliases,
        compiler_params=pltpu.CompilerParams(vmem_limit_bytes=VMEM_LIMIT),
        name="in_proj",
    )(x, mod, mod, ln, w_in_b, *carried)


def _out_kernel(n_sub, x_ref, yp_ref, ya_ref, g1_ref, sh2_ref, sc2_ref, g2_ref,
                ln1_ref, ln2a_ref, ln2b_ref, wo_ref, w1_ref, w2_ref, o_ref):
    sub = x_ref.shape[0] // n_sub
    rows = [pl.ds(i * sub, sub) for i in range(n_sub)]
    mod = lambda ref, i: ref[...] if ref.shape[0] == 1 else ref[rows[i], :]

    def out_proj(i):
        return (_dot(yp_ref[rows[i], :].astype(BF16), wo_ref[:C_POOL, :])
                + _dot(ya_ref[rows[i], :].astype(BF16), wo_ref[C_POOL:, :]))

    def residual_and_prenorm(i, mixed):
        x1 = x_ref[rows[i], :] + mod(g1_ref, i) * _rms(mixed, ln1_ref[...])
        h = (_rms(x1, ln2a_ref[...]) * (1.0 + mod(sc2_ref, i)) + mod(sh2_ref, i)).astype(BF16)
        return x1, h

    def mlp_chunk(h, f, c):
        sl = slice(c * D_MODEL, (c + 1) * D_MODEL)
        a = jnp.maximum(_dot(h, w1_ref[:, sl]), 0.0)
        return f + _dot((a * a).astype(BF16), w2_ref[sl, :])

    def finish(i, x1, f):
        o_ref[rows[i], :] = x1 + mod(g2_ref, i) * _rms(f, ln2b_ref[...])

    n_chunks = D_FF // D_MODEL
    mixed = [out_proj(i) for i in range(n_sub)]
    x1, h = residual_and_prenorm(0, mixed[0])
    for i in range(n_sub):
        f = mlp_chunk(h, jnp.zeros(x1.shape, F32), 0)
        if i + 1 < n_sub:
            nxt = residual_and_prenorm(i + 1, mixed[i + 1])
        if i > 0:
            finish(i - 1, *done)
        for c in range(1, n_chunks):
            f = mlp_chunk(h, f, c)
        done = (x1, f)
        if i + 1 < n_sub:
            x1, h = nxt
    finish(n_sub - 1, *done)


def _out_ffn(x, yp, ya, mod, per_batch_rows, ln1_post, ln2_pre, ln2_post, wo_b, w1_b, w2_b, tm):
    r = x.shape[0]
    row = lambda width: pl.BlockSpec((tm, width), lambda i: (i, 0))
    const = lambda shape: pl.BlockSpec(shape, lambda i: (0, 0), pipeline_mode=pl.Buffered(1))
    ms = lambda chunk: _mod_spec(per_batch_rows, tm, chunk)
    return pl.pallas_call(
        functools.partial(_out_kernel, max(1, tm // OUT_SUB_ROWS)),
        grid=(r // tm,),
        in_specs=[row(D_MODEL), row(C_POOL), row(D_ATT), ms(2), ms(3), ms(4), ms(5),
                  const((1, D_MODEL)), const((1, D_MODEL)), const((1, D_MODEL)),
                  const((D_MODEL, D_MODEL)), const((D_MODEL, D_FF)), const((D_FF, D_MODEL))],
        out_specs=row(D_MODEL),
        out_shape=jax.ShapeDtypeStruct((r, D_MODEL), F32),
        compiler_params=pltpu.CompilerParams(vmem_limit_bytes=VMEM_LIMIT),
        name="out_ffn",
    )(x, yp, ya, mod, mod, mod, mod, ln1_post, ln2_pre, ln2_post, wo_b, w1_b, w2_b)


def _pool_kernel(pos0, t_rows, chunk, has_hist, u_ref, *refs):
    if has_hist:
        hist_ref, pw_ref, ps_ref, o_ref, ue_ref = refs
        ue_ref[:, :POOL_PAD, :] = hist_ref[...]
    else:
        pw_ref, ps_ref, o_ref, ue_ref = refs
        ue_ref[:, :POOL_PAD, :] = jnp.zeros((ue_ref.shape[0], POOL_PAD, C_POOL), F32)
    ue_ref[:, POOL_PAD:, :] = u_ref[...]
    g_elems = ue_ref.shape[0]
    n_chunks = t_rows // chunk
    for ci in range(n_chunks):
        r0 = ci * chunk
        pos = pos0 + r0 + lax.broadcasted_iota(I32, (chunk, POOL_GC), 0)
        for g, w in enumerate(POOL_WINDOWS):
            lanes = pl.ds(g * POOL_GC, POOL_GC)
            cnt = jnp.minimum(pos + 1, w).astype(F32)
            pooled = []
            for e in range(g_elems):
                cur = ue_ref[e, pl.ds(POOL_PAD + r0, chunk), lanes]
                acc = cur
                for j in range(1, w):
                    acc = acc + ue_ref[e, pl.ds(POOL_PAD + r0 - j, chunk), lanes]
                pooled.append(acc / cnt - cur)
            pooled = pooled[0] if g_elems == 1 else jnp.concatenate(pooled, axis=0)
            y = _dot(pooled.astype(BF16), pw_ref[g].astype(BF16))
            y = y * ps_ref[:, lanes]
            for e in range(g_elems):
                o_ref[e, pl.ds(r0, chunk), lanes] = y[e * chunk:(e + 1) * chunk].astype(o_ref.dtype)


def _pool_mix(u, hist, pos0, pool_w_l, pool_scale_l, g_elems, chunk):
    b, t_rows, _ = u.shape
    per_elem = lambda rows: pl.BlockSpec((g_elems, rows, C_POOL), lambda i: (i, 0, 0))
    in_specs = [per_elem(t_rows)] + ([per_elem(POOL_PAD)] if hist is not None else []) + [
        pl.BlockSpec((len(POOL_WINDOWS), POOL_GC, POOL_GC), lambda i: (0, 0, 0)),
        pl.BlockSpec((1, C_POOL), lambda i: (0, 0))]
    args = (u,) + ((hist,) if hist is not None else ()) + (pool_w_l, pool_scale_l)
    return pl.pallas_call(
        functools.partial(_pool_kernel, pos0, t_rows, chunk, hist is not None),
        grid=(b // g_elems,),
        in_specs=in_specs,
        out_specs=per_elem(t_rows),
        out_shape=jax.ShapeDtypeStruct((b, t_rows, C_POOL), F32),
        scratch_shapes=[pltpu.VMEM((g_elems, POOL_PAD + t_rows, C_POOL), F32)],
        compiler_params=pltpu.CompilerParams(vmem_limit_bytes=VMEM_LIMIT),
        name="pool_mix",
    )(*args)


def _bias_of_dist(dist, rb_ref, h):
    n = jnp.maximum(dist, 0)
    max_exact = N_BUCKETS // 2
    large = max_exact + (jnp.log(jnp.maximum(n, 1).astype(F32) / max_exact)
                         / math.log(MAX_DISTANCE / max_exact) * (N_BUCKETS - max_exact)).astype(I32)
    large = jnp.minimum(large, N_BUCKETS - 1)
    bucket = jnp.where(n < max_exact, n, large)
    out = jnp.zeros(dist.shape, F32)
    for b in range(N_BUCKETS):
        out = jnp.where(bucket == b, rb_ref[b, h], out)
    return out


def _bias_kernel(blk, rb_ref, tb_ref, s_far_ref, s_last_ref, s_new_ref):
    key = lax.broadcasted_iota(I32, (blk, blk), 0)
    qry = lax.broadcasted_iota(I32, (blk, blk), 1)
    t = lax.broadcasted_iota(I32, (SUBLANES, LANES), 0)
    jj = lax.broadcasted_iota(I32, (SUBLANES, LANES), 1)
    for h in range(N_HEADS):
        tb_ref[0, h] = _bias_of_dist(jnp.full((blk, blk), 2 * blk, I32), rb_ref, h)
        tb_ref[1, h] = _bias_of_dist(blk + qry - key, rb_ref, h)
        tb_ref[2, h] = _bias_of_dist(qry - key, rb_ref, h)
        rows = pl.ds(h * SUBLANES, SUBLANES)
        s_far_ref[rows, :] = _bias_of_dist(jnp.full((SUBLANES, LANES), 2 * PAGE_SIZE, I32), rb_ref, h)
        s_last_ref[rows, :] = _bias_of_dist(PAGE_SIZE + t - jj, rb_ref, h)
        s_new_ref[rows, :] = _bias_of_dist(t - jj, rb_ref, h)


def _bias_tables(rel_bias, blk):
    sm = jax.ShapeDtypeStruct((N_HEADS * SUBLANES, LANES), F32)
    return pl.pallas_call(
        functools.partial(_bias_kernel, blk),
        in_specs=[pl.BlockSpec(memory_space=pltpu.SMEM)],
        out_shape=[jax.ShapeDtypeStruct((3, N_HEADS, blk, blk), F32), sm, sm, sm],
        compiler_params=pltpu.CompilerParams(vmem_limit_bytes=VMEM_LIMIT),
        name="bias_tables",
    )(rel_bias)


def _kth_start(shape):
    return jnp.full(shape, INT_MIN, I32), jnp.zeros(shape, F32)


def _kth_rounds(count_ge_key, k, carry, first_round, last_round, digit_bits=1):
    def body(r, carry):
        tkey, nge = carry
        shift = 32 - digit_bits * (r + 1)
        best_key, best_cnt = tkey, nge
        for j in range(1, 1 << digit_bits):
            ckey = tkey + lax.shift_left(jnp.int32(j), shift)
            cnt = count_ge_key(ckey)
            ok = cnt >= k
            best_key = jnp.where(ok, ckey, best_key)
            best_cnt = jnp.where(ok, cnt, best_cnt)
        return best_key, best_cnt

    return lax.fori_loop(first_round, last_round, body, carry)


def _kth_finish(carry):
    tkey, nge = carry
    return jnp.where(tkey == INT_MIN, NEG_INF, _key_to_float(tkey)), nge


def _kth_largest(count_ge, k, shape, digit_bits=1):
    carry = _kth_rounds(lambda ckey: count_ge(_key_to_float(ckey)), k, _kth_start(shape),
                        0, 32 // digit_bits, digit_bits)
    return _kth_finish(carry)


def _tie_limit(count_eq_below, need, shape, bits):
    def body(it, lim):
        cand = lim + lax.shift_left(jnp.int32(1), bits - 1 - it)
        return jnp.where(count_eq_below(cand) <= need, cand, lim)

    return lax.fori_loop(0, bits, body, jnp.zeros(shape, I32))


def _count_tiles(s, first_key, pred):
    row = lax.broadcasted_iota(I32, (SUBLANES, s.shape[1]), 0)
    return _tree([pred(s[r * SUBLANES:(r + 1) * SUBLANES], row + (first_key + r * SUBLANES))
                  for r in range(s.shape[0] // SUBLANES)])


def _resolve_ties(k_sel, thr, nge, count_where, idx_bits, lim_ref):
    lim_ref[...] = jnp.full(lim_ref.shape, 1 << idx_bits, I32)

    @pl.when(jnp.max(nge) > k_sel)
    def _():
        n_gt = count_where(lambda s, idx: jnp.where(s > thr, 1.0, 0.0))
        need = k_sel - n_gt
        lim_ref[...] = _tie_limit(
            lambda cand: count_where(lambda s, idx: jnp.where(s == thr, jnp.where(idx < cand, 1.0, 0.0), 0.0)),
            need, lim_ref.shape, idx_bits)


def _select_mask(s, idx, thr, lim):
    tie = jnp.where(s == thr, jnp.where(idx < lim, 0.0, NEG_INF), NEG_INF)
    m = jnp.where(s > thr, 0.0, tie)
    return jnp.where(s == NEG_INF, NEG_INF, m)


def _attn_prompt_kernel(k_sel, blk, seq,
                        qt_ref, qit_ref, wit_ref, kb_ref, kib_ref, vbt_ref, tb_ref,
                        o_ref,
                        score_ref, trunc_ref, lg_ref, mrun_ref, lsum_ref, acc_ref, thr_ref, lim_ref):
    qb = pl.program_id(1)
    n_chunks = qb + 1
    rep = (SUBLANES, blk)
    idx_bits = max(1, (seq - 1).bit_length()) + 1

    key_row = lax.broadcasted_iota(I32, (blk, blk), 0)
    qry_col = lax.broadcasted_iota(I32, (blk, blk), 1)

    def chunk_keys(c):
        return pl.ds(pl.multiple_of(c * blk, blk), blk)

    def all_sublanes(x, op):
        return jnp.broadcast_to(op(x, axis=0, keepdims=True), rep)

    w_idx = wit_ref[...] * IDX_HEADS ** -0.5 * IDX_DIM ** -0.5

    def score_chunk(c, carry):
        keys = chunk_keys(c)
        kic = kib_ref[keys, :][:, :IDX_DIM]
        s = jnp.zeros((blk, blk), F32)
        for h in range(IDX_HEADS):
            d = _dot(kic, qit_ref[h * IDX_DIM:(h + 1) * IDX_DIM, :])
            s = s + w_idx[h:h + 1, :] * jnp.maximum(d, 0.0)
        s = jnp.where(key_row + c * blk <= qry_col + qb * blk, s, NEG_INF)
        score_ref[keys, :] = s
        hi = lax.bitcast_convert_type(s, I32) & jnp.int32(-(1 << 16))
        trunc_ref[keys, :] = lax.bitcast_convert_type(hi, F32).astype(BF16)
        return carry

    lax.fori_loop(0, n_chunks, score_chunk, 0)

    def count_where(pred):
        def body(c, acc):
            return acc + _count_tiles(score_ref[chunk_keys(c), :], c * blk, pred)
        acc = lax.fori_loop(0, n_chunks, body, jnp.zeros(rep, F32))
        return all_sublanes(acc, jnp.sum)

    def count_upper_half(ckey):
        bits = ckey ^ ((ckey >> 31) & jnp.int32(0x7FFFFFFF))
        cf = lax.bitcast_convert_type(bits & jnp.int32(-(1 << 16)), F32)
        cb = jnp.concatenate([cf, cf], axis=0).astype(BF16)
        one, zero = jnp.ones((BF16_ROWS, blk), BF16), jnp.zeros((BF16_ROWS, blk), BF16)

        def body(c, acc):
            t = trunc_ref[chunk_keys(c), :]
            return acc + _tree([jnp.where(t[r * BF16_ROWS:(r + 1) * BF16_ROWS] >= cb, one, zero)
                                for r in range(blk // BF16_ROWS)])
        acc = lax.fori_loop(0, n_chunks, body, zero)
        return all_sublanes(acc.astype(F32), jnp.sum)

    carry = _kth_rounds(count_upper_half, k_sel, _kth_start(rep), 0, 16)
    def count_ge_key(ckey):
        cf = _key_to_float(ckey)
        return count_where(lambda s, idx: jnp.where(s >= cf, 1.0, 0.0))

    carry = _kth_rounds(count_ge_key, k_sel, carry, 16, 32)
    thr, nge = _kth_finish(carry)
    thr_ref[...] = thr
    _resolve_ties(k_sel, thr, nge, count_where, idx_bits, lim_ref)

    def mask_chunk(c, carry):
        keys = chunk_keys(c)
        score_ref[keys, :] = _select_mask(score_ref[keys, :], key_row + c * blk,
                                          thr_ref[0:1, :], lim_ref[0:1, :])
        return carry

    lax.fori_loop(0, n_chunks, mask_chunk, 0)

    mrun_ref[...] = jnp.full(mrun_ref.shape, NEG_INF, F32)

    def logits_chunk(c, carry):
        keys = chunk_keys(c)
        madd = score_ref[keys, :]
        table = jnp.clip(c - (qb - 2), 0, 2)
        for h in range(N_HEADS):
            hs = slice(h * HEAD_DIM, (h + 1) * HEAD_DIM)
            s = _dot(kb_ref[keys, hs], qt_ref[hs, :]) + tb_ref[table, h] + madd
            lg_ref[h, keys, :] = s
            mrun_ref[h] = jnp.maximum(mrun_ref[h], _fold_tiles(s, jnp.maximum))
        return carry

    lax.fori_loop(0, n_chunks, logits_chunk, 0)

    for h in range(N_HEADS):
        mrun_ref[h] = all_sublanes(mrun_ref[h], jnp.max)
    lsum_ref[...] = jnp.zeros(lsum_ref.shape, F32)
    acc_ref[...] = jnp.zeros(acc_ref.shape, F32)

    def pv_chunk(c, carry):
        keys = chunk_keys(c)
        for h in range(N_HEADS):
            hs = slice(h * HEAD_DIM, (h + 1) * HEAD_DIM)
            p = jnp.exp(lg_ref[h, keys, :] - mrun_ref[h, 0:1, :])
            lsum_ref[h] += _fold_tiles(p)
            acc_ref[h] += _dot(vbt_ref[hs, keys], p.astype(BF16))
        return carry

    lax.fori_loop(0, n_chunks, pv_chunk, 0)

    outs =[acc_ref[h] / jnp.sum(lsum_ref[h], axis=0, keepdims=True) for h in range(N_HEADS)]
    o_ref[...] = jnp.concatenate(outs, axis=0).T.astype(o_ref.dtype)


def _attn_prompt(cfg, k_sel, qt, qit, wit, kb, kib, vbt, tb):
    blk, seq = cfg.blk, cfg.seq
    nq = seq // blk
    qspec = lambda width: pl.BlockSpec((None, width, blk), lambda b, i: (b, 0, i))
    kspec = lambda width: pl.BlockSpec((seq, width), lambda b, i: (b, 0))
    return pl.pallas_call(
        functools.partial(_attn_prompt_kernel, k_sel, blk, seq),
        grid=(cfg.batch, nq),
        in_specs=[qspec(D_ATT), qspec(IDX_HEADS * IDX_DIM), qspec(IDX_HEADS),
                  kspec(D_ATT), kspec(TAIL), pl.BlockSpec((None, D_ATT, seq), lambda b, i: (b, 0, 0)),
                  pl.BlockSpec((3, N_HEADS, blk, blk), lambda b, i: (0, 0, 0, 0), pipeline_mode=pl.Buffered(1))],
        out_specs=pl.BlockSpec((blk, D_ATT), lambda b, i: (b * nq + i, 0)),
        out_shape=jax.ShapeDtypeStruct((cfg.batch * seq, D_ATT), BF16),
        scratch_shapes=[pltpu.VMEM((seq, blk), F32),
                        pltpu.VMEM((seq, blk), BF16),
                        pltpu.VMEM((N_HEADS, seq, blk), F32),
                        pltpu.VMEM((N_HEADS, SUBLANES, blk), F32),
                        pltpu.VMEM((N_HEADS, SUBLANES, blk), F32),
                        pltpu.VMEM((N_HEADS, HEAD_DIM, blk), F32),
                        pltpu.VMEM((SUBLANES, blk), F32),
                        pltpu.VMEM((SUBLANES, blk), I32)],
        compiler_params=pltpu.CompilerParams(vmem_limit_bytes=VMEM_LIMIT),
        name="attn_prompt",
    )(qt, qit, wit, kb, kib, vbt, tb)


def _pad_rows(x, rows):
    return jnp.concatenate([x, jnp.zeros((rows - x.shape[0], x.shape[1]), x.dtype)], axis=0)


def _fetch_pages(pt_ref, layer, n_steps, n_pages, caches):
    n_slots = caches[0][1].shape[0]
    ahead = n_slots - 1
    i = pl.program_id(0)
    slot = i % n_slots

    def copy(cache, b, s, p):
        hbm, buf, sem = cache
        return pltpu.make_async_copy(hbm.at[layer, pt_ref[b, p]], buf.at[s, p], sem.at[s])

    def start(b, s):
        for p in range(n_pages):
            for cache in caches:
                copy(cache, b, s, p).start()

    @pl.when(i == 0)
    def _():
        for b in range(min(ahead, n_steps)):
            start(b, b)

    @pl.when(i + ahead < n_steps)
    def _():
        start(i + ahead, (i + ahead) % n_slots)

    for p in range(n_pages):
        for cache in caches:
            copy(cache, i, slot, p).wait()
    return slot


def _page_scratch(n_slots, n_pages, rows):
    return [pltpu.VMEM((n_slots, n_pages, rows, PAGE_SIZE), F32), pltpu.SemaphoreType.DMA((n_slots,))]


def _score_sample_kernel(layer, n_steps, n_pages,
                         pt_ref, qi_ref, wrep_ref, tailn_ref, ki_hbm, o_ref, ki_buf, ki_sem):
    slot = _fetch_pages(pt_ref, layer, n_steps, n_pages, [(ki_hbm, ki_buf, ki_sem)])
    shape = (SUBLANES, LANES)
    qi = qi_ref[...]
    w = wrep_ref[...]

    def score_block(dots):
        return _fold_tiles(w * jnp.maximum(dots * IDX_DIM ** -0.5, 0.0))

    for p in range(n_pages):
        o_ref[:, p * PAGE_SIZE:(p + 1) * PAGE_SIZE] = score_block(_dot(qi, ki_buf[slot, p].astype(BF16)))
    ki_new = _pad_rows(tailn_ref[...][:, :IDX_DIM], PAGE_SIZE).astype(BF16)
    causal = lax.broadcasted_iota(I32, shape, 1) <= lax.broadcasted_iota(I32, shape, 0)
    o_ref[:, n_pages * PAGE_SIZE:] = jnp.where(causal, score_block(_nt_dot(qi, ki_new)), NEG_INF)


def _score_sample(cfg, layer, page_table, qi_ht, wrep, tail_new, cache_ki):
    b, t_rows = cfg.dec_batch, cfg.dec_seq
    n_pages = cfg.past_len // PAGE_SIZE
    rows = N_HEADS * t_rows
    per_b = lambda r, w: pl.BlockSpec((None, r, w), lambda i, pt: (i, 0, 0))
    n_keys = (n_pages + 1) * PAGE_SIZE
    return pl.pallas_call(
        functools.partial(_score_sample_kernel, layer, b, n_pages),
        grid_spec=pltpu.PrefetchScalarGridSpec(
            num_scalar_prefetch=1, grid=(b,),
            in_specs=[per_b(rows, IDX_DIM), per_b(rows, LANES), per_b(t_rows, TAIL),
                      pl.BlockSpec(memory_space=pl.ANY)],
            out_specs=per_b(t_rows, n_keys),
            scratch_shapes=_page_scratch(SCORE_PAGE_SLOTS, n_pages, IDX_DIM)),
        out_shape=jax.ShapeDtypeStruct((b, t_rows, n_keys), F32),
        name="score_sample",
    )(page_table, qi_ht, wrep, tail_new, cache_ki)


def _threshold_kernel(k_sel, s_ref, thr_ref, lim_ref):
    n_keys, cols = s_ref.shape
    rep = (SUBLANES, cols)
    idx_bits = (n_keys - 1).bit_length() + 1

    def count_where(pred):
        acc = _tree([_count_tiles(s_ref[c * LANES:(c + 1) * LANES, :], c * LANES, pred)
                     for c in range(n_keys // LANES)])
        return jnp.broadcast_to(jnp.sum(acc, axis=0, keepdims=True), rep)

    thr, nge = _kth_largest(lambda cf: count_where(lambda s, idx: jnp.where(s >= cf, 1.0, 0.0)), k_sel, rep)
    thr_ref[...] = thr
    _resolve_ties(k_sel, thr, nge, count_where, idx_bits, lim_ref)


def _threshold_cols(k_sel, scores_t, cols):
    n_keys, n_q = scores_t.shape
    spec = pl.BlockSpec((SUBLANES, cols), lambda i: (0, i))
    return pl.pallas_call(
        functools.partial(_threshold_kernel, k_sel),
        grid=(n_q // cols,),
        in_specs=[pl.BlockSpec((n_keys, cols), lambda i: (0, i))],
        out_specs=[spec, spec],
        out_shape=[jax.ShapeDtypeStruct((SUBLANES, n_q), F32), jax.ShapeDtypeStruct((SUBLANES, n_q), I32)],
        name="threshold_cols",
    )(scores_t)


def _attn_sample_kernel(layer, n_steps, n_pages, pt_ref,
                        sc_ref, thr_ref, lim_ref, q_ref, kn_ref, vn_ref, bfar_ref, blast_ref, bnew_ref,
                        k_hbm, v_hbm, o_ref, k_buf, k_sem, v_buf, v_sem):
    slot = _fetch_pages(pt_ref, layer, n_steps, n_pages, [(k_hbm, k_buf, k_sem), (v_hbm, v_buf, v_sem)])
    t_rows = SUBLANES
    n_blocks = n_pages + 1
    lane = lax.broadcasted_iota(I32, (t_rows, LANES), 1)
    thr, lim = thr_ref[...], lim_ref[...]
    masks = [_select_mask(sc_ref[:, b * PAGE_SIZE:(b + 1) * PAGE_SIZE], lane + b * PAGE_SIZE, thr, lim)
             for b in range(n_blocks)]

    def pad_new(x):
        return _pad_rows(x, PAGE_SIZE)

    rows = N_HEADS * t_rows
    same_head = (lax.broadcasted_iota(I32, (rows, D_ATT), 0) // t_rows
                 == lax.broadcasted_iota(I32, (rows, D_ATT), 1) // HEAD_DIM)
    q_bd = jnp.where(same_head, jnp.concatenate([q_ref[...]] * N_HEADS, axis=0), 0.0).astype(BF16)

    def tile_heads(m):
        return jnp.concatenate([m] * N_HEADS, axis=0)

    logits = []
    for p in range(n_pages):
        bias = blast_ref[...] if p == n_pages - 1 else bfar_ref[...]
        logits.append(_dot(q_bd, k_buf[slot, p].astype(BF16)) + bias + tile_heads(masks[p]))
    logits.append(_nt_dot(q_bd, pad_new(kn_ref[...]).astype(BF16)) + bnew_ref[...] + tile_heads(masks[n_pages]))

    m = logits[0]
    for lg in logits[1:]:
        m = jnp.maximum(m, lg)
    m = _lane_rep(jnp.max(m, axis=1, keepdims=True))
    lsum = jnp.zeros((rows, LANES), F32)
    acc = jnp.zeros((rows, D_ATT), F32)
    for b in range(n_blocks):
        p = jnp.exp(logits[b] - m)
        lsum = lsum + p
        if b < n_pages:
            acc = acc + _nt_dot(p.astype(BF16), v_buf[slot, b].astype(BF16))
        else:
            acc = acc + _dot(p.astype(BF16), pad_new(vn_ref[...]).astype(BF16))
    out = jnp.where(same_head, acc / jnp.sum(lsum, axis=1, keepdims=True), 0.0)
    o_ref[...] = _fold_tiles(out)


def _attn_sample(cfg, layer, page_table, scores, thr_rep, lim_rep, q, k_new, v_new,
                 s_far, s_last, s_new, cache_k, cache_v):
    b = cfg.dec_batch
    t_rows = cfg.dec_seq
    n_pages = cfg.past_len // PAGE_SIZE
    rows = N_HEADS * t_rows
    per_b = lambda r, w: pl.BlockSpec((None, r, w), lambda i, pt: (i, 0, 0))
    const = lambda r, w: pl.BlockSpec((r, w), lambda i, pt: (0, 0))
    in_hbm = pl.BlockSpec(memory_space=pl.ANY)
    in_specs = [per_b(t_rows, scores.shape[2]), per_b(t_rows, LANES), per_b(t_rows, LANES),
                per_b(t_rows, D_ATT), per_b(t_rows, D_ATT), per_b(t_rows, D_ATT),
                const(rows, LANES), const(rows, LANES), const(rows, LANES), in_hbm, in_hbm]
    return pl.pallas_call(
        functools.partial(_attn_sample_kernel, layer, b, n_pages),
        grid_spec=pltpu.PrefetchScalarGridSpec(
            num_scalar_prefetch=1, grid=(b,), in_specs=in_specs, out_specs=per_b(t_rows, D_ATT),
            scratch_shapes=(_page_scratch(ATTN_PAGE_SLOTS, n_pages, D_ATT)
                            + _page_scratch(ATTN_PAGE_SLOTS, n_pages, D_ATT))),
        out_shape=jax.ShapeDtypeStruct((b, t_rows, D_ATT), F32),
        compiler_params=pltpu.CompilerParams(vmem_limit_bytes=VMEM_LIMIT),
        name="attn_sample",
    )(page_table, scores, thr_rep, lim_rep, q, k_new, v_new, s_far, s_last, s_new, cache_k, cache_v)


def _forward(cfg, x_prompt, x_sample, cache_k, cache_v, cache_kidx, state_pool, page_table, c_prompt,
             c_sample, rel_bias, ada_w, ada_b, ln1_pre, ln1_post, ln2_pre, ln2_post, w_in, pool_w,
             pool_scale, w_out, w_ff1, w_ff2):
    depth = ada_w.shape[0]
    bp, seq, bs, ts = cfg.batch, cfg.seq, cfg.dec_batch, cfg.dec_seq
    assert ts == SUBLANES and seq % cfg.blk == 0 and cfg.past_len % PAGE_SIZE == 0
    assert cfg.blk >= PAGE_SIZE and seq >= TOPK_MAX and seq // BF16_ROWS <= 256
    k_prompt = min(TOPK_MAX, seq // 4)
    k_sample = min(TOPK_MAX, (cfg.past_len + ts) // 4)
    rp, rs = bp * seq, bs * ts
    tm_s = min(IN_SUB_ROWS, rs)

    mod = _ada_mod(jnp.concatenate([c_prompt, c_sample], axis=0), ada_w, ada_b)
    tb, s_far, s_last, s_new = _bias_tables(rel_bias, cfg.blk)

    n_phys = cache_k.shape[1]
    cache_kt = cache_k.transpose(0, 1, 3, 4, 2).reshape(depth, n_phys, D_ATT, PAGE_SIZE)
    cache_vt = cache_v.transpose(0, 1, 3, 4, 2).reshape(depth, n_phys, D_ATT, PAGE_SIZE)
    cache_kit = cache_kidx.transpose(0, 1, 3, 2)

    xp = x_prompt.reshape(rp, D_MODEL)
    xs = x_sample.reshape(rs, D_MODEL)
    state_p = ()
    pool_p = []
    outs_s = [[], [], [], []]
    for l in range(depth):
        w_in_b = jnp.pad(w_in[l], ((0, 0), (0, D_IN_PAD - D_IN))).astype(BF16)
        wo_b, w1_b, w2_b = w_out[l].astype(BF16), w_ff1[l].astype(BF16), w_ff2[l].astype(BF16)
        lnrow = lambda a: a[l].reshape(1, D_MODEL)
        mod_p = mod[l, :bp].reshape(bp, 1, 6 * D_MODEL)
        mod_s = jnp.repeat(mod[l, bp:], ts, axis=0)
        psc = pool_scale[l].reshape(1, C_POOL)

        tm_in = min(seq, cfg.tm if state_p else max(IN_SUB_ROWS, cfg.tm // depth))
        u, kb, tailb, *state_p, wit, qt, qit, vbt = _in_proj(
            xp, mod_p, seq, lnrow(ln1_pre), w_in_b, tm_in, layer=l, depth=depth, carried=tuple(state_p))
        u3 = u.reshape(bp, seq, C_POOL)
        y_pool = _pool_mix(u3, None, 0, pool_w[l], psc, 1, min(256, seq)).reshape(rp, C_POOL)
        y_att = _attn_prompt(cfg, k_prompt, qt, qit, wit, kb, tailb, vbt, tb)
        xp = _out_ffn(xp, y_pool, y_att, mod_p, seq, lnrow(ln1_post), lnrow(ln2_pre), lnrow(ln2_post),
                      wo_b, w1_b, w2_b, min(cfg.tm_out, seq))
        pool_p.append(u3[:, seq - POOL_HIST:])

        u, q, qi, tail, k, v = _in_proj(xs, mod_s, 0, lnrow(ln1_pre), w_in_b, tm_s)
        u3 = u.reshape(bs, ts, C_POOL)
        hist = jnp.concatenate([jnp.zeros((bs, POOL_PAD - POOL_HIST, C_POOL), F32), state_pool[l]], axis=1)
        y_pool = _pool_mix(u3, hist, cfg.past_len, pool_w[l], psc, min(16, bs), ts).reshape(rs, C_POOL)
        qi_ht = qi.reshape(bs, ts, IDX_HEADS, IDX_DIM).transpose(0, 2, 1, 3).reshape(bs, IDX_HEADS * ts, IDX_DIM)
        wi = tail[:, OFF_WI - OFF_KI:OFF_WI - OFF_KI + IDX_HEADS].reshape(bs, ts, IDX_HEADS) * IDX_HEADS ** -0.5
        wrep = jnp.broadcast_to(wi.transpose(0, 2, 1).reshape(bs, IDX_HEADS * ts, 1), (bs, IDX_HEADS * ts, LANES))
        scores = _score_sample(cfg, l, page_table, qi_ht, wrep, tail.reshape(bs, ts, TAIL), cache_kit)
        thr, lim = _threshold_cols(k_sample, scores.transpose(2, 0, 1).reshape(scores.shape[2], rs),
                                   min(256, rs))
        per_query = lambda a: jnp.broadcast_to(a[0].reshape(bs, ts, 1), (bs, ts, LANES))
        y_att = _attn_sample(cfg, l, page_table, scores, per_query(thr), per_query(lim),
                             q.astype(F32).reshape(bs, ts, D_ATT),
                             k.reshape(bs, ts, D_ATT), v.reshape(bs, ts, D_ATT),
                             s_far, s_last, s_new, cache_kt, cache_vt).reshape(rs, D_ATT)
        xs = _out_ffn(xs, y_pool, y_att, mod_s, 0, lnrow(ln1_post), lnrow(ln2_pre), lnrow(ln2_post),
                      wo_b, w1_b, w2_b, min(OUT_SUB_ROWS, rs))
        for lst, a in zip(outs_s, (k.reshape(bs, ts, N_HEADS, HEAD_DIM), v.reshape(bs, ts, N_HEADS, HEAD_DIM),
                                   tail[:, :IDX_DIM].reshape(bs, ts, IDX_DIM),
                                   jnp.concatenate([hist, u3], axis=1)[:, -POOL_HIST:])):
            lst.append(a)

    kt, vt, kit = state_p
    heads_last = lambda a: a.reshape(depth, bp, N_HEADS, HEAD_DIM, seq).transpose(0, 1, 4, 2, 3)
    return (xp.reshape(bp, seq, D_MODEL), xs.reshape(bs, ts, D_MODEL),
            heads_last(kt), heads_last(vt), kit.transpose(0, 1, 3, 2), jnp.stack(pool_p),
            *[jnp.stack(a) for a in outs_s])


def kernel(x_prompt, x_sample, cache_k, cache_v, cache_kidx, state_pool, page_table, c_prompt, c_sample,
           rel_bias, ada_w, ada_b, ln1_pre, ln1_post, ln2_pre, ln2_post, w_in, pool_w, pool_scale,
           w_out, w_ff1, w_ff2):
    cfg = Cfg(batch=x_prompt.shape[0], seq=x_prompt.shape[1], dec_batch=x_sample.shape[0],
              dec_seq=x_sample.shape[1], past_len=page_table.shape[1] * PAGE_SIZE, blk=256, tm=1024,
              tm_out=1024)
    return _forward(cfg, x_prompt, x_sample, cache_k, cache_v, cache_kidx, state_pool, page_table, c_prompt,
                    c_sample, rel_bias, ada_w, ada_b, ln1_pre, ln1_post, ln2_pre, ln2_post, w_in, pool_w,
                    pool_scale, w_out, w_ff1, w_ff2)
```

```python
import functools
import math
from typing import NamedTuple

import jax
import jax.numpy as jnp
from jax import lax
from jax.experimental import pallas as pl
from jax.experimental.pallas import tpu as pltpu

F32 = jnp.float32
BF16 = jnp.bfloat16
I32 = jnp.int32

D_MODEL = 1024
C_POOL = 512
D_ATT = 512
HEAD_DIM = 64
N_HEADS = 8
IDX_HEADS = 8
IDX_DIM = 64
POOL_WINDOWS = (2, 4, 8, 16)
POOL_GC = C_POOL // len(POOL_WINDOWS)
POOL_HIST = max(POOL_WINDOWS) - 1
TOPK_MAX = 256
N_BUCKETS = 32
MAX_DISTANCE = 128
D_FF = 4 * D_MODEL
EPS = 1e-6
PAGE_SIZE = 128
OFF_Q = C_POOL
OFF_K = OFF_Q + D_ATT
OFF_V = OFF_K + D_ATT
OFF_QI = OFF_V + D_ATT
OFF_KI = OFF_QI + IDX_HEADS * IDX_DIM
OFF_WI = OFF_KI + IDX_DIM
D_IN = OFF_WI + IDX_HEADS

LANES = 128
SUBLANES = 8
BF16_ROWS = 2 * SUBLANES
D_IN_PAD = ((D_IN + LANES - 1) // LANES) * LANES
TAIL = D_IN_PAD - OFF_KI
POOL_PAD = 16
OUT_SUB_ROWS = 256
IN_SUB_ROWS = 256
SCORE_PAGE_SLOTS = 4
ATTN_PAGE_SLOTS = 3
VMEM_LIMIT = 56 * 1024 * 1024

NEG_INF = float("-inf")
LOG2_E = math.log2(math.e)
INT_MIN = -(2 ** 31)


class Cfg(NamedTuple):
    batch: int
    seq: int
    dec_batch: int
    dec_seq: int
    past_len: int
    blk: int
    tm: int
    tm_out: int


def _rms(x, g):
    ms = jnp.mean(x * x, axis=-1, keepdims=True)
    return x * lax.rsqrt(ms + EPS) * g


def _nt_dot(a, b):
    return lax.dot_general(a, b, (((1,), (1,)), ((), ())), preferred_element_type=F32)


def _dot(a, b):
    return jnp.dot(a, b, preferred_element_type=F32)


def _key_to_float(key):
    bits = key ^ ((key >> 31) & jnp.int32(0x7FFFFFFF))
    return lax.bitcast_convert_type(bits, F32)


def _lane_rep(col, width=LANES):
    return jnp.broadcast_to(col, (col.shape[0], width))


def _fold_tiles(x, op=jnp.add):
    return _tree([x[r * SUBLANES:(r + 1) * SUBLANES] for r in range(x.shape[0] // SUBLANES)], op)


def _tree(parts, op=jnp.add):
    while len(parts) > 1:
        parts = [op(parts[i], parts[i + 1]) if i + 1 < len(parts) else parts[i] for i in range(0, len(parts), 2)]
    return parts[0]


def _ada_kernel(c_ref, w_ref, b_ref, o_ref):
    o_ref[...] = _dot(c_ref[...].astype(BF16), w_ref[...].astype(BF16)) + b_ref[...]


def _ada_mod(c_all, ada_w, ada_b):
    depth, d, n = ada_w.shape
    rc = c_all.shape[0]
    tn = 1536
    return pl.pallas_call(
        _ada_kernel,
        grid=(depth, n // tn),
        in_specs=[pl.BlockSpec((rc, d), lambda l, j: (0, 0)),
                  pl.BlockSpec((None, d, tn), lambda l, j: (l, 0, j)),
                  pl.BlockSpec((None, 1, tn), lambda l, j: (l, 0, j))],
        out_specs=pl.BlockSpec((None, rc, tn), lambda l, j: (l, 0, j)),
        out_shape=jax.ShapeDtypeStruct((depth, rc, n), F32),
        name="ada_mod",
    )(c_all, ada_w, ada_b.reshape(depth, 1, n))


def _mod_spec(per_batch_rows, tm, chunk):
    if per_batch_rows:
        tiles = per_batch_rows // tm
        return pl.BlockSpec((None, 1, D_MODEL), lambda i: (i // tiles, 0, chunk))
    return pl.BlockSpec((tm, D_MODEL), lambda i: (i, chunk))


def _in_kernel(transposed, n_carried, n_sub, layer, x_ref, sh_ref, sc_ref, ln_ref, w_ref, *refs):
    u_ref, *refs = refs[n_carried:]
    sub = x_ref.shape[0] // n_sub
    mod = lambda ref, rows: ref[...] if ref.shape[0] == 1 else ref[rows, :]

    def put_state(ref, cols, val):
        if len(ref.shape) == 2:
            ref[:, cols] = val
        else:
            for l in range(ref.shape[0]):
                ref[l, :, cols] = val if l == layer else jnp.zeros(val.shape, val.dtype)

    def project(i):
        rows = pl.ds(i * sub, sub)
        h = _rms(x_ref[rows, :], ln_ref[...]) * (1.0 + mod(sc_ref, rows)) + mod(sh_ref, rows)
        return _dot(h.astype(BF16), w_ref[...])

    def emit(i, z):
        rows = pl.ds(i * sub, sub)
        u_ref[rows, :] = z[:, :OFF_Q]
        q = z[:, OFF_Q:OFF_K] * HEAD_DIM ** -0.5
        k = z[:, OFF_K:OFF_V]
        v = z[:, OFF_V:OFF_QI]
        qi = z[:, OFF_QI:OFF_KI]
        tail = z[:, OFF_KI:]
        if transposed:
            kb_ref, tailb_ref, kt_ref, vt_ref, kit_ref, wit_ref, qt_ref, qit_ref, vbt_ref = refs
            cols = pl.ds(i * sub, sub)
            kb_ref[rows, :] = k.astype(BF16)
            tailb_ref[rows, :] = tail.astype(BF16)
            put_state(kt_ref, cols, k.T)
            vt = v.T
            put_state(vt_ref, cols, vt)
            vbt_ref[:, cols] = vt.astype(BF16)
            tail_t = tail.T
            put_state(kit_ref, cols, tail_t[:IDX_DIM])
            wit_ref[:, cols] = tail_t[OFF_WI - OFF_KI:OFF_WI - OFF_KI + IDX_HEADS]
            qt_ref[:, cols] = q.T.astype(BF16)
            qit_ref[:, cols] = qi.T.astype(BF16)
        else:
            q_ref, qi_ref, tail_ref, k_ref, v_ref = refs
            q_ref[rows, :] = q.astype(BF16)
            qi_ref[rows, :] = qi.astype(BF16)
            tail_ref[rows, :] = tail
            k_ref[rows, :] = k
            v_ref[rows, :] = v

    z = project(0)
    for i in range(n_sub):
        z_next = project(i + 1) if i + 1 < n_sub else None
        emit(i, z)
        z = z_next


def _in_proj(x, mod, per_batch_rows, ln, w_in_b, tm, layer=0, depth=1, carried=()):
    r = x.shape[0]
    row = lambda width: pl.BlockSpec((tm, width), lambda i: (i, 0))
    const = lambda shape: pl.BlockSpec(shape, lambda i: (0, 0))
    in_specs = [row(D_MODEL), _mod_spec(per_batch_rows, tm, 0), _mod_spec(per_batch_rows, tm, 1),
                const((1, D_MODEL)), pl.BlockSpec((None, D_MODEL, D_IN_PAD), lambda i: (layer, 0, 0))]
    aliases = {}
    if per_batch_rows:
        tiles = per_batch_rows // tm
        nb = r // per_batch_rows
        rows_out = [(C_POOL, F32), (D_ATT, BF16), (TAIL, BF16)]
        state_out = [D_ATT, D_ATT, IDX_DIM]
        cols_out = [(IDX_HEADS, F32), (D_ATT, BF16), (IDX_HEADS * IDX_DIM, BF16), (D_ATT, BF16)]
    else:
        rows_out = [(C_POOL, F32), (D_ATT, BF16), (IDX_HEADS * IDX_DIM, BF16), (TAIL, F32), (D_ATT, F32), (D_ATT, F32)]
        state_out, cols_out = [], []
    out_specs = [row(w) for w, _ in rows_out]
    out_shape = [jax.ShapeDtypeStruct((r, w), dt) for w, dt in rows_out]
    for n, width in enumerate(state_out):
        if carried:
            aliases[len(in_specs)] = len(out_specs)
            in_specs.append(pl.BlockSpec(memory_space=pl.ANY))
            out_specs.append(pl.BlockSpec((None, None, width, tm), lambda i: (layer, i // tiles, 0, i % tiles)))
        else:
            out_specs.append(pl.BlockSpec((depth, None, width, tm), lambda i: (0, i // tiles, 0, i % tiles)))
        out_shape.append(jax.ShapeDtypeStruct((depth, nb, width, per_batch_rows), F32))
    for width, dt in cols_out:
        out_specs.append(pl.BlockSpec((None, width, tm), lambda i: (i // tiles, 0, i % tiles)))
        out_shape.append(jax.ShapeDtypeStruct((nb, width, per_batch_rows), dt))
    return pl.pallas_call(
        functools.partial(_in_kernel, bool(per_batch_rows), len(carried), max(1, tm // IN_SUB_ROWS), layer),
        grid=(r // tm,),
        in_specs=in_specs,
        out_specs=out_specs,
        out_shape=out_shape,
        input_output_aliases=aliases,
        compiler_params=pltpu.CompilerParams(vmem_limit_bytes=VMEM_LIMIT),
        name="in_proj",
    )(x, mod, mod, ln, w_in_b, *carried)


def _out_kernel(n_sub, x_ref, yp_ref, ya_ref, g1_ref, sh2_ref, sc2_ref, g2_ref,
                ln1_ref, ln2a_ref, ln2b_ref, wo_ref, w1_ref, w2_ref, o_ref):
    sub = x_ref.shape[0] // n_sub
    rows = [pl.ds(i * sub, sub) for i in range(n_sub)]
    mod = lambda ref, i: ref[...] if ref.shape[0] == 1 else ref[rows[i], :]

    def out_proj(i):
        return (_dot(yp_ref[rows[i], :].astype(BF16), wo_ref[:C_POOL, :])
                + _dot(ya_ref[rows[i], :].astype(BF16), wo_ref[C_POOL:, :]))

    def residual_and_prenorm(i, mixed):
        x1 = x_ref[rows[i], :] + mod(g1_ref, i) * _rms(mixed, ln1_ref[...])
        h = (_rms(x1, ln2a_ref[...]) * (1.0 + mod(sc2_ref, i)) + mod(sh2_ref, i)).astype(BF16)
        return x1, h

    def mlp_chunk(h, f, c):
        sl = slice(c * D_MODEL, (c + 1) * D_MODEL)
        a = jnp.maximum(_dot(h, w1_ref[:, sl]), 0.0)
        return f + _dot((a * a).astype(BF16), w2_ref[sl, :])

    def finish(i, x1, f):
        o_ref[rows[i], :] = x1 + mod(g2_ref, i) * _rms(f, ln2b_ref[...])

    n_chunks = D_FF // D_MODEL
    mixed = [out_proj(i) for i in range(n_sub)]
    x1, h = residual_and_prenorm(0, mixed[0])
    for i in range(n_sub):
        f = mlp_chunk(h, jnp.zeros(x1.shape, F32), 0)
        if i + 1 < n_sub:
            nxt = residual_and_prenorm(i + 1, mixed[i + 1])
        if i > 0:
            finish(i - 1, *done)
        for c in range(1, n_chunks):
            f = mlp_chunk(h, f, c)
        done = (x1, f)
        if i + 1 < n_sub:
            x1, h = nxt
    finish(n_sub - 1, *done)


def _out_ffn(x, yp, ya, mod, per_batch_rows, ln1_post, ln2_pre, ln2_post, wo_b, w1_b, w2_b, layer, tm):
    r = x.shape[0]
    row = lambda width: pl.BlockSpec((tm, width), lambda i: (i, 0))
    const = lambda shape: pl.BlockSpec(shape, lambda i: (0, 0), pipeline_mode=pl.Buffered(1))
    weight = lambda shape: pl.BlockSpec((None,) + shape, lambda i: (layer, 0, 0), pipeline_mode=pl.Buffered(1))
    ms = lambda chunk: _mod_spec(per_batch_rows, tm, chunk)
    return pl.pallas_call(
        functools.partial(_out_kernel, max(1, tm // OUT_SUB_ROWS)),
        grid=(r // tm,),
        in_specs=[row(D_MODEL), row(C_POOL), row(D_ATT), ms(2), ms(3), ms(4), ms(5),
                  const((1, D_MODEL)), const((1, D_MODEL)), const((1, D_MODEL)),
                  weight((D_MODEL, D_MODEL)), weight((D_MODEL, D_FF)), weight((D_FF, D_MODEL))],
        out_specs=row(D_MODEL),
        out_shape=jax.ShapeDtypeStruct((r, D_MODEL), F32),
        compiler_params=pltpu.CompilerParams(vmem_limit_bytes=VMEM_LIMIT),
        name="out_ffn",
    )(x, yp, ya, mod, mod, mod, mod, ln1_post, ln2_pre, ln2_post, wo_b, w1_b, w2_b)


def _pool_kernel(pos0, t_rows, chunk, has_hist, u_ref, *refs):
    if has_hist:
        hist_ref, pw_ref, ps_ref, o_ref, ue_ref = refs
        ue_ref[:, :POOL_PAD, :] = hist_ref[...]
    else:
        pw_ref, ps_ref, o_ref, ue_ref = refs
        ue_ref[:, :POOL_PAD, :] = jnp.zeros((ue_ref.shape[0], POOL_PAD, C_POOL), F32)
    ue_ref[:, POOL_PAD:, :] = u_ref[...]
    g_elems = ue_ref.shape[0]
    n_chunks = t_rows // chunk
    for ci in range(n_chunks):
        r0 = ci * chunk
        pos = pos0 + r0 + lax.broadcasted_iota(I32, (chunk, POOL_GC), 0)
        for g, w in enumerate(POOL_WINDOWS):
            lanes = pl.ds(g * POOL_GC, POOL_GC)
            cnt = jnp.minimum(pos + 1, w).astype(F32)
            pooled = []
            for e in range(g_elems):
                cur = ue_ref[e, pl.ds(POOL_PAD + r0, chunk), lanes]
                acc = cur
                for j in range(1, w):
                    acc = acc + ue_ref[e, pl.ds(POOL_PAD + r0 - j, chunk), lanes]
                pooled.append(acc / cnt - cur)
            pooled = pooled[0] if g_elems == 1 else jnp.concatenate(pooled, axis=0)
            y = _dot(pooled.astype(BF16), pw_ref[g].astype(BF16))
            y = y * ps_ref[:, lanes]
            for e in range(g_elems):
                o_ref[e, pl.ds(r0, chunk), lanes] = y[e * chunk:(e + 1) * chunk].astype(o_ref.dtype)


def _pool_mix(u, hist, pos0, pool_w_l, pool_scale_l, g_elems, chunk):
    b, t_rows, _ = u.shape
    per_elem = lambda rows: pl.BlockSpec((g_elems, rows, C_POOL), lambda i: (i, 0, 0))
    in_specs = [per_elem(t_rows)] + ([per_elem(POOL_PAD)] if hist is not None else []) + [
        pl.BlockSpec((len(POOL_WINDOWS), POOL_GC, POOL_GC), lambda i: (0, 0, 0)),
        pl.BlockSpec((1, C_POOL), lambda i: (0, 0))]
    args = (u,) + ((hist,) if hist is not None else ()) + (pool_w_l, pool_scale_l)
    return pl.pallas_call(
        functools.partial(_pool_kernel, pos0, t_rows, chunk, hist is not None),
        grid=(b // g_elems,),
        in_specs=in_specs,
        out_specs=per_elem(t_rows),
        out_shape=jax.ShapeDtypeStruct((b, t_rows, C_POOL), F32),
        scratch_shapes=[pltpu.VMEM((g_elems, POOL_PAD + t_rows, C_POOL), F32)],
        compiler_params=pltpu.CompilerParams(vmem_limit_bytes=VMEM_LIMIT),
        name="pool_mix",
    )(*args)


def _bias_of_dist(dist, rb_ref, h):
    n = jnp.maximum(dist, 0)
    max_exact = N_BUCKETS // 2
    large = max_exact + (jnp.log(jnp.maximum(n, 1).astype(F32) / max_exact)
                         / math.log(MAX_DISTANCE / max_exact) * (N_BUCKETS - max_exact)).astype(I32)
    large = jnp.minimum(large, N_BUCKETS - 1)
    bucket = jnp.where(n < max_exact, n, large)
    out = jnp.zeros(dist.shape, F32)
    for b in range(N_BUCKETS):
        out = jnp.where(bucket == b, rb_ref[b, h], out)
    return out


def _bias_kernel(blk, rb_ref, tb_ref, s_far_ref, s_last_ref, s_new_ref):
    key = lax.broadcasted_iota(I32, (blk, blk), 0)
    qry = lax.broadcasted_iota(I32, (blk, blk), 1)
    t = lax.broadcasted_iota(I32, (SUBLANES, LANES), 0)
    jj = lax.broadcasted_iota(I32, (SUBLANES, LANES), 1)
    for h in range(N_HEADS):
        tb_ref[0, h] = _bias_of_dist(jnp.full((blk, blk), 2 * blk, I32), rb_ref, h)
        tb_ref[1, h] = _bias_of_dist(blk + qry - key, rb_ref, h)
        tb_ref[2, h] = _bias_of_dist(qry - key, rb_ref, h)
        rows = pl.ds(h * SUBLANES, SUBLANES)
        s_far_ref[rows, :] = _bias_of_dist(jnp.full((SUBLANES, LANES), 2 * PAGE_SIZE, I32), rb_ref, h)
        s_last_ref[rows, :] = _bias_of_dist(PAGE_SIZE + t - jj, rb_ref, h)
        s_new_ref[rows, :] = _bias_of_dist(t - jj, rb_ref, h)


def _bias_tables(rel_bias, blk):
    sm = jax.ShapeDtypeStruct((N_HEADS * SUBLANES, LANES), F32)
    return pl.pallas_call(
        functools.partial(_bias_kernel, blk),
        in_specs=[pl.BlockSpec(memory_space=pltpu.SMEM)],
        out_shape=[jax.ShapeDtypeStruct((3, N_HEADS, blk, blk), F32), sm, sm, sm],
        compiler_params=pltpu.CompilerParams(vmem_limit_bytes=VMEM_LIMIT),
        name="bias_tables",
    )(rel_bias)


def _kth_start(shape):
    return jnp.full(shape, INT_MIN, I32), jnp.zeros(shape, F32)


def _kth_rounds(count_ge_key, k, carry, first_round, last_round, digit_bits=1):
    def body(r, carry):
        tkey, nge = carry
        shift = 32 - digit_bits * (r + 1)
        best_key, best_cnt = tkey, nge
        for j in range(1, 1 << digit_bits):
            ckey = tkey + lax.shift_left(jnp.int32(j), shift)
            cnt = count_ge_key(ckey)
            ok = cnt >= k
            best_key = jnp.where(ok, ckey, best_key)
            best_cnt = jnp.where(ok, cnt, best_cnt)
        return best_key, best_cnt

    return lax.fori_loop(first_round, last_round, body, carry)


def _kth_finish(carry):
    tkey, nge = carry
    return jnp.where(tkey == INT_MIN, NEG_INF, _key_to_float(tkey)), nge


def _kth_largest(count_ge, k, shape, digit_bits=1):
    carry = _kth_rounds(lambda ckey: count_ge(_key_to_float(ckey)), k, _kth_start(shape),
                        0, 32 // digit_bits, digit_bits)
    return _kth_finish(carry)


def _tie_limit(count_eq_below, need, shape, bits):
    def body(it, lim):
        cand = lim + lax.shift_left(jnp.int32(1), bits - 1 - it)
        return jnp.where(count_eq_below(cand) <= need, cand, lim)

    return lax.fori_loop(0, bits, body, jnp.zeros(shape, I32))


def _count_tiles(s, first_key, pred):
    row = lax.broadcasted_iota(I32, (SUBLANES, s.shape[1]), 0)
    return _tree([pred(s[r * SUBLANES:(r + 1) * SUBLANES], row + (first_key + r * SUBLANES))
                  for r in range(s.shape[0] // SUBLANES)])


def _resolve_ties(k_sel, thr, nge, count_where, idx_bits, lim_ref):
    lim_ref[...] = jnp.full(lim_ref.shape, 1 << idx_bits, I32)

    @pl.when(jnp.max(nge) > k_sel)
    def _():
        n_gt = count_where(lambda s, idx: jnp.where(s > thr, 1.0, 0.0))
        need = k_sel - n_gt
        lim_ref[...] = _tie_limit(
            lambda cand: count_where(lambda s, idx: jnp.where(s == thr, jnp.where(idx < cand, 1.0, 0.0), 0.0)),
            need, lim_ref.shape, idx_bits)


def _select_mask(s, idx, thr, lim):
    tie = jnp.where(s == thr, jnp.where(idx < lim, 0.0, NEG_INF), NEG_INF)
    m = jnp.where(s > thr, 0.0, tie)
    return jnp.where(s == NEG_INF, NEG_INF, m)


def _attn_prompt_kernel(k_sel, blk, seq,
                        qt_ref, qit_ref, wit_ref, kb_ref, kib_ref, vbt_ref, tb_ref,
                        o_ref,
                        score_ref, trunc_ref, lg_ref, mrun_ref, lsum_ref, acc_ref, thr_ref, lim_ref):
    qb = pl.program_id(1)
    n_chunks = qb + 1
    rep = (SUBLANES, blk)
    idx_bits = max(1, (seq - 1).bit_length()) + 1

    key_row = lax.broadcasted_iota(I32, (blk, blk), 0)
    qry_col = lax.broadcasted_iota(I32, (blk, blk), 1)

    def chunk_keys(c):
        return pl.ds(pl.multiple_of(c * blk, blk), blk)

    def all_sublanes(x, op):
        return jnp.broadcast_to(op(x, axis=0, keepdims=True), rep)

    w_idx = wit_ref[...] * IDX_HEADS ** -0.5 * IDX_DIM ** -0.5

    def score_chunk(c, carry):
        keys = chunk_keys(c)
        kic = kib_ref[keys, :][:, :IDX_DIM]
        s = jnp.zeros((blk, blk), F32)
        for h in range(IDX_HEADS):
            d = _dot(kic, qit_ref[h * IDX_DIM:(h + 1) * IDX_DIM, :])
            s = s + w_idx[h:h + 1, :] * jnp.maximum(d, 0.0)
        s = jnp.where(key_row + c * blk <= qry_col + qb * blk, s, NEG_INF)
        score_ref[keys, :] = s
        hi = lax.bitcast_convert_type(s, I32) & jnp.int32(-(1 << 16))
        trunc_ref[keys, :] = lax.bitcast_convert_type(hi, F32).astype(BF16)
        return carry

    lax.fori_loop(0, n_chunks, score_chunk, 0)

    def count_where(pred):
        def body(c, acc):
            return acc + _count_tiles(score_ref[chunk_keys(c), :], c * blk, pred)
        acc = lax.fori_loop(0, n_chunks, body, jnp.zeros(rep, F32))
        return all_sublanes(acc, jnp.sum)

    def count_upper_half(ckey):
        bits = ckey ^ ((ckey >> 31) & jnp.int32(0x7FFFFFFF))
        cf = lax.bitcast_convert_type(bits & jnp.int32(-(1 << 16)), F32)
        cb = jnp.concatenate([cf, cf], axis=0).astype(BF16)
        one, zero = jnp.ones((BF16_ROWS, blk), BF16), jnp.zeros((BF16_ROWS, blk), BF16)

        def body(c, acc):
            t = trunc_ref[chunk_keys(c), :]
            return acc + _tree([jnp.where(t[r * BF16_ROWS:(r + 1) * BF16_ROWS] >= cb, one, zero)
                                for r in range(blk // BF16_ROWS)])
        acc = lax.fori_loop(0, n_chunks, body, zero)
        return all_sublanes(acc.astype(F32), jnp.sum)

    carry = _kth_rounds(count_upper_half, k_sel, _kth_start(rep), 0, 16)
    def count_ge_key(ckey):
        cf = _key_to_float(ckey)
        return count_where(lambda s, idx: jnp.where(s >= cf, 1.0, 0.0))

    carry = _kth_rounds(count_ge_key, k_sel, carry, 16, 32)
    thr, nge = _kth_finish(carry)
    thr_ref[...] = thr
    _resolve_ties(k_sel, thr, nge, count_where, idx_bits, lim_ref)

    def mask_chunk(c, carry):
        keys = chunk_keys(c)
        score_ref[keys, :] = _select_mask(score_ref[keys, :], key_row + c * blk,
                                          thr_ref[0:1, :], lim_ref[0:1, :])
        return carry

    lax.fori_loop(0, n_chunks, mask_chunk, 0)

    mrun_ref[...] = jnp.full(mrun_ref.shape, NEG_INF, F32)

    def logits_chunk(c, carry):
        keys = chunk_keys(c)
        madd = score_ref[keys, :]
        table = jnp.clip(c - (qb - 2), 0, 2)
        for h in range(N_HEADS):
            hs = slice(h * HEAD_DIM, (h + 1) * HEAD_DIM)
            s = (_dot(kb_ref[keys, hs], qt_ref[hs, :]) + tb_ref[table, h] + madd) * LOG2_E
            lg_ref[h, keys, :] = s
            mrun_ref[h] = jnp.maximum(mrun_ref[h], _fold_tiles(s, jnp.maximum))
        return carry

    lax.fori_loop(0, n_chunks, logits_chunk, 0)

    for h in range(N_HEADS):
        mrun_ref[h] = all_sublanes(mrun_ref[h], jnp.max)
    lsum_ref[...] = jnp.zeros(lsum_ref.shape, F32)
    acc_ref[...] = jnp.zeros(acc_ref.shape, F32)

    def pv_chunk(c, carry):
        keys = chunk_keys(c)
        for h in range(N_HEADS):
            hs = slice(h * HEAD_DIM, (h + 1) * HEAD_DIM)
            p = jnp.exp2(lg_ref[h, keys, :] - mrun_ref[h, 0:1, :])
            lsum_ref[h] += _fold_tiles(p)
            acc_ref[h] += _dot(vbt_ref[hs, keys], p.astype(BF16))
        return carry

    lax.fori_loop(0, n_chunks, pv_chunk, 0)

    outs =[acc_ref[h] / jnp.sum(lsum_ref[h], axis=0, keepdims=True) for h in range(N_HEADS)]
    o_ref[...] = jnp.concatenate(outs, axis=0).T.astype(o_ref.dtype)


def _attn_prompt(cfg, k_sel, qt, qit, wit, kb, kib, vbt, tb):
    blk, seq = cfg.blk, cfg.seq
    nq = seq // blk
    qspec = lambda width: pl.BlockSpec((None, width, blk), lambda b, i: (b, 0, i))
    kspec = lambda width: pl.BlockSpec((seq, width), lambda b, i: (b, 0))
    return pl.pallas_call(
        functools.partial(_attn_prompt_kernel, k_sel, blk, seq),
        grid=(cfg.batch, nq),
        in_specs=[qspec(D_ATT), qspec(IDX_HEADS * IDX_DIM), qspec(IDX_HEADS),
                  kspec(D_ATT), kspec(TAIL), pl.BlockSpec((None, D_ATT, seq), lambda b, i: (b, 0, 0)),
                  pl.BlockSpec((3, N_HEADS, blk, blk), lambda b, i: (0, 0, 0, 0), pipeline_mode=pl.Buffered(1))],
        out_specs=pl.BlockSpec((blk, D_ATT), lambda b, i: (b * nq + i, 0)),
        out_shape=jax.ShapeDtypeStruct((cfg.batch * seq, D_ATT), BF16),
        scratch_shapes=[pltpu.VMEM((seq, blk), F32),
                        pltpu.VMEM((seq, blk), BF16),
                        pltpu.VMEM((N_HEADS, seq, blk), F32),
                        pltpu.VMEM((N_HEADS, SUBLANES, blk), F32),
                        pltpu.VMEM((N_HEADS, SUBLANES, blk), F32),
                        pltpu.VMEM((N_HEADS, HEAD_DIM, blk), F32),
                        pltpu.VMEM((SUBLANES, blk), F32),
                        pltpu.VMEM((SUBLANES, blk), I32)],
        compiler_params=pltpu.CompilerParams(vmem_limit_bytes=VMEM_LIMIT),
        name="attn_prompt",
    )(qt, qit, wit, kb, kib, vbt, tb)


def _pad_rows(x, rows):
    return jnp.concatenate([x, jnp.zeros((rows - x.shape[0], x.shape[1]), x.dtype)], axis=0)


def _fetch_pages(pt_ref, layer, n_steps, n_pages, caches):
    n_slots = caches[0][1].shape[0]
    ahead = n_slots - 1
    i = pl.program_id(0)
    slot = i % n_slots

    def copy(cache, b, s, p):
        hbm, buf, sem = cache
        return pltpu.make_async_copy(hbm.at[layer, pt_ref[b, p]], buf.at[s, p], sem.at[s])

    def start(b, s):
        for p in range(n_pages):
            for cache in caches:
                copy(cache, b, s, p).start()

    @pl.when(i == 0)
    def _():
        for b in range(min(ahead, n_steps)):
            start(b, b)

    @pl.when(i + ahead < n_steps)
    def _():
        start(i + ahead, (i + ahead) % n_slots)

    for p in range(n_pages):
        for cache in caches:
            copy(cache, i, slot, p).wait()
    return slot


def _page_scratch(n_slots, n_pages, rows):
    return [pltpu.VMEM((n_slots, n_pages, rows, PAGE_SIZE), F32), pltpu.SemaphoreType.DMA((n_slots,))]


def _score_sample_kernel(layer, n_steps, n_pages,
                         pt_ref, qi_ref, wrep_ref, tailn_ref, ki_hbm, o_ref, ki_buf, ki_sem):
    slot = _fetch_pages(pt_ref, layer, n_steps, n_pages, [(ki_hbm, ki_buf, ki_sem)])
    shape = (SUBLANES, LANES)
    qi = qi_ref[...]
    w = wrep_ref[...]

    def score_block(dots):
        return _fold_tiles(w * jnp.maximum(dots * IDX_DIM ** -0.5, 0.0))

    for p in range(n_pages):
        o_ref[:, p * PAGE_SIZE:(p + 1) * PAGE_SIZE] = score_block(_dot(qi, ki_buf[slot, p].astype(BF16)))
    ki_new = _pad_rows(tailn_ref[...][:, :IDX_DIM], PAGE_SIZE).astype(BF16)
    causal = lax.broadcasted_iota(I32, shape, 1) <= lax.broadcasted_iota(I32, shape, 0)
    o_ref[:, n_pages * PAGE_SIZE:] = jnp.where(causal, score_block(_nt_dot(qi, ki_new)), NEG_INF)


def _score_sample(cfg, layer, page_table, qi_ht, wrep, tail_new, cache_ki):
    b, t_rows = cfg.dec_batch, cfg.dec_seq
    n_pages = cfg.past_len // PAGE_SIZE
    rows = N_HEADS * t_rows
    per_b = lambda r, w: pl.BlockSpec((None, r, w), lambda i, pt: (i, 0, 0))
    n_keys = (n_pages + 1) * PAGE_SIZE
    return pl.pallas_call(
        functools.partial(_score_sample_kernel, layer, b, n_pages),
        grid_spec=pltpu.PrefetchScalarGridSpec(
            num_scalar_prefetch=1, grid=(b,),
            in_specs=[per_b(rows, IDX_DIM), per_b(rows, LANES), per_b(t_rows, TAIL),
                      pl.BlockSpec(memory_space=pl.ANY)],
            out_specs=per_b(t_rows, n_keys),
            scratch_shapes=_page_scratch(SCORE_PAGE_SLOTS, n_pages, IDX_DIM)),
        out_shape=jax.ShapeDtypeStruct((b, t_rows, n_keys), F32),
        name="score_sample",
    )(page_table, qi_ht, wrep, tail_new, cache_ki)


def _threshold_kernel(k_sel, sc_ref, thr_ref, lim_ref, s_ref):
    n_keys, cols = s_ref.shape
    s_ref[...] = sc_ref[...].reshape(cols, n_keys).T
    rep = (SUBLANES, cols)
    idx_bits = (n_keys - 1).bit_length() + 1

    def count_where(pred):
        acc = _tree([_count_tiles(s_ref[c * LANES:(c + 1) * LANES, :], c * LANES, pred)
                     for c in range(n_keys // LANES)])
        return jnp.broadcast_to(jnp.sum(acc, axis=0, keepdims=True), rep)

    thr, nge = _kth_largest(lambda cf: count_where(lambda s, idx: jnp.where(s >= cf, 1.0, 0.0)), k_sel, rep)
    thr_ref[...] = thr
    _resolve_ties(k_sel, thr, nge, count_where, idx_bits, lim_ref)


def _threshold_cols(k_sel, scores, cols):
    b, t_rows, n_keys = scores.shape
    n_q = b * t_rows
    spec = pl.BlockSpec((SUBLANES, cols), lambda i: (0, i))
    return pl.pallas_call(
        functools.partial(_threshold_kernel, k_sel),
        grid=(n_q // cols,),
        in_specs=[pl.BlockSpec((cols // t_rows, t_rows, n_keys), lambda i: (i, 0, 0))],
        out_specs=[spec, spec],
        out_shape=[jax.ShapeDtypeStruct((SUBLANES, n_q), F32), jax.ShapeDtypeStruct((SUBLANES, n_q), I32)],
        scratch_shapes=[pltpu.VMEM((n_keys, cols), F32)],
        name="threshold_cols",
    )(scores)


def _attn_sample_kernel(layer, n_steps, n_pages, pt_ref,
                        sc_ref, thr_ref, lim_ref, q_ref, kn_ref, vn_ref, bfar_ref, blast_ref, bnew_ref,
                        k_hbm, v_hbm, o_ref, k_buf, k_sem, v_buf, v_sem):
    slot = _fetch_pages(pt_ref, layer, n_steps, n_pages, [(k_hbm, k_buf, k_sem), (v_hbm, v_buf, v_sem)])
    t_rows = SUBLANES
    n_blocks = n_pages + 1
    lane = lax.broadcasted_iota(I32, (t_rows, LANES), 1)
    thr, lim = thr_ref[...], lim_ref[...]
    masks = [_select_mask(sc_ref[:, b * PAGE_SIZE:(b + 1) * PAGE_SIZE], lane + b * PAGE_SIZE, thr, lim)
             for b in range(n_blocks)]

    def pad_new(x):
        return _pad_rows(x, PAGE_SIZE)

    rows = N_HEADS * t_rows
    same_head = (lax.broadcasted_iota(I32, (rows, D_ATT), 0) // t_rows
                 == lax.broadcasted_iota(I32, (rows, D_ATT), 1) // HEAD_DIM)
    q_bd = jnp.where(same_head, jnp.concatenate([q_ref[...]] * N_HEADS, axis=0), 0.0).astype(BF16)

    def tile_heads(m):
        return jnp.concatenate([m] * N_HEADS, axis=0)

    logits = []
    for p in range(n_pages):
        bias = blast_ref[...] if p == n_pages - 1 else bfar_ref[...]
        logits.append(_dot(q_bd, k_buf[slot, p].astype(BF16)) + bias + tile_heads(masks[p]))
    logits.append(_nt_dot(q_bd, pad_new(kn_ref[...]).astype(BF16)) + bnew_ref[...] + tile_heads(masks[n_pages]))

    m = logits[0]
    for lg in logits[1:]:
        m = jnp.maximum(m, lg)
    m = _lane_rep(jnp.max(m, axis=1, keepdims=True))
    lsum = jnp.zeros((rows, LANES), F32)
    acc = jnp.zeros((rows, D_ATT), F32)
    for b in range(n_blocks):
        p = jnp.exp(logits[b] - m)
        lsum = lsum + p
        if b < n_pages:
            acc = acc + _nt_dot(p.astype(BF16), v_buf[slot, b].astype(BF16))
        else:
            acc = acc + _dot(p.astype(BF16), pad_new(vn_ref[...]).astype(BF16))
    out = jnp.where(same_head, acc / jnp.sum(lsum, axis=1, keepdims=True), 0.0)
    o_ref[...] = _fold_tiles(out)


def _attn_sample(cfg, layer, page_table, scores, thr_rep, lim_rep, q, k_new, v_new,
                 s_far, s_last, s_new, cache_k, cache_v):
    b = cfg.dec_batch
    t_rows = cfg.dec_seq
    n_pages = cfg.past_len // PAGE_SIZE
    rows = N_HEADS * t_rows
    per_b = lambda r, w: pl.BlockSpec((None, r, w), lambda i, pt: (i, 0, 0))
    const = lambda r, w: pl.BlockSpec((r, w), lambda i, pt: (0, 0))
    in_hbm = pl.BlockSpec(memory_space=pl.ANY)
    in_specs = [per_b(t_rows, scores.shape[2]), per_b(t_rows, LANES), per_b(t_rows, LANES),
                per_b(t_rows, D_ATT), per_b(t_rows, D_ATT), per_b(t_rows, D_ATT),
                const(rows, LANES), const(rows, LANES), const(rows, LANES), in_hbm, in_hbm]
    return pl.pallas_call(
        functools.partial(_attn_sample_kernel, layer, b, n_pages),
        grid_spec=pltpu.PrefetchScalarGridSpec(
            num_scalar_prefetch=1, grid=(b,), in_specs=in_specs, out_specs=per_b(t_rows, D_ATT),
            scratch_shapes=(_page_scratch(ATTN_PAGE_SLOTS, n_pages, D_ATT)
                            + _page_scratch(ATTN_PAGE_SLOTS, n_pages, D_ATT))),
        out_shape=jax.ShapeDtypeStruct((b, t_rows, D_ATT), F32),
        compiler_params=pltpu.CompilerParams(vmem_limit_bytes=VMEM_LIMIT),
        name="attn_sample",
    )(page_table, scores, thr_rep, lim_rep, q, k_new, v_new, s_far, s_last, s_new, cache_k, cache_v)


def _forward(cfg, x_prompt, x_sample, cache_k, cache_v, cache_kidx, state_pool, page_table, c_prompt,
             c_sample, rel_bias, ada_w, ada_b, ln1_pre, ln1_post, ln2_pre, ln2_post, w_in, pool_w,
             pool_scale, w_out, w_ff1, w_ff2):
    depth = ada_w.shape[0]
    bp, seq, bs, ts = cfg.batch, cfg.seq, cfg.dec_batch, cfg.dec_seq
    assert ts == SUBLANES and seq % cfg.blk == 0 and cfg.past_len % PAGE_SIZE == 0
    assert cfg.blk >= PAGE_SIZE and seq >= TOPK_MAX and seq // BF16_ROWS <= 256
    k_prompt = min(TOPK_MAX, seq // 4)
    k_sample = min(TOPK_MAX, (cfg.past_len + ts) // 4)
    rp, rs = bp * seq, bs * ts
    tm_s = min(IN_SUB_ROWS, rs)

    mod = _ada_mod(jnp.concatenate([c_prompt, c_sample], axis=0), ada_w, ada_b)
    tb, s_far, s_last, s_new = _bias_tables(rel_bias, cfg.blk)

    n_phys = cache_k.shape[1]
    cache_kt = cache_k.transpose(0, 1, 3, 4, 2).reshape(depth, n_phys, D_ATT, PAGE_SIZE)
    cache_vt = cache_v.transpose(0, 1, 3, 4, 2).reshape(depth, n_phys, D_ATT, PAGE_SIZE)
    cache_kit = cache_kidx.transpose(0, 1, 3, 2)

    xp = x_prompt.reshape(rp, D_MODEL)
    xs = x_sample.reshape(rs, D_MODEL)
    state_p = ()
    pool_p = []
    outs_s = [[], [], [], []]
    w_in_b = jnp.pad(w_in, ((0, 0), (0, 0), (0, D_IN_PAD - D_IN))).astype(BF16)
    wo_b, w1_b, w2_b = w_out.astype(BF16), w_ff1.astype(BF16), w_ff2.astype(BF16)
    for l in range(depth):
        lnrow = lambda a: a[l].reshape(1, D_MODEL)
        mod_p = mod[l, :bp].reshape(bp, 1, 6 * D_MODEL)
        mod_s = jnp.repeat(mod[l, bp:], ts, axis=0)
        psc = pool_scale[l].reshape(1, C_POOL)

        tm_in = min(seq, cfg.tm if state_p else max(IN_SUB_ROWS, cfg.tm // depth))
        u, kb, tailb, *state_p, wit, qt, qit, vbt = _in_proj(
            xp, mod_p, seq, lnrow(ln1_pre), w_in_b, tm_in, layer=l, depth=depth, carried=tuple(state_p))
        u3 = u.reshape(bp, seq, C_POOL)
        y_pool = _pool_mix(u3, None, 0, pool_w[l], psc, 1, min(256, seq)).reshape(rp, C_POOL)
        y_att = _attn_prompt(cfg, k_prompt, qt, qit, wit, kb, tailb, vbt, tb)
        xp = _out_ffn(xp, y_pool, y_att, mod_p, seq, lnrow(ln1_post), lnrow(ln2_pre), lnrow(ln2_post),
                      wo_b, w1_b, w2_b, l, min(cfg.tm_out, seq))
        pool_p.append(u3[:, seq - POOL_HIST:])

        u, q, qi, tail, k, v = _in_proj(xs, mod_s, 0, lnrow(ln1_pre), w_in_b, tm_s, layer=l)
        u3 = u.reshape(bs, ts, C_POOL)
        hist = jnp.concatenate([jnp.zeros((bs, POOL_PAD - POOL_HIST, C_POOL), F32), state_pool[l]], axis=1)
        y_pool = _pool_mix(u3, hist, cfg.past_len, pool_w[l], psc, min(16, bs), ts).reshape(rs, C_POOL)
        qi_ht = qi.reshape(bs, ts, IDX_HEADS, IDX_DIM).transpose(0, 2, 1, 3).reshape(bs, IDX_HEADS * ts, IDX_DIM)
        wi = tail[:, OFF_WI - OFF_KI:OFF_WI - OFF_KI + IDX_HEADS].reshape(bs, ts, IDX_HEADS) * IDX_HEADS ** -0.5
        wrep = jnp.broadcast_to(wi.transpose(0, 2, 1).reshape(bs, IDX_HEADS * ts, 1), (bs, IDX_HEADS * ts, LANES))
        scores = _score_sample(cfg, l, page_table, qi_ht, wrep, tail.reshape(bs, ts, TAIL), cache_kit)
        thr, lim = _threshold_cols(k_sample, scores, min(256, rs))
        per_query = lambda a: jnp.broadcast_to(a[0].reshape(bs, ts, 1), (bs, ts, LANES))
        y_att = _attn_sample(cfg, l, page_table, scores, per_query(thr), per_query(lim),
                             q.astype(F32).reshape(bs, ts, D_ATT),
                             k.reshape(bs, ts, D_ATT), v.reshape(bs, ts, D_ATT),
                             s_far, s_last, s_new, cache_kt, cache_vt).reshape(rs, D_ATT)
        xs = _out_ffn(xs, y_pool, y_att, mod_s, 0, lnrow(ln1_post), lnrow(ln2_pre), lnrow(ln2_post),
                      wo_b, w1_b, w2_b, l, min(OUT_SUB_ROWS, rs))
        for lst, a in zip(outs_s, (k.reshape(bs, ts, N_HEADS, HEAD_DIM), v.reshape(bs, ts, N_HEADS, HEAD_DIM),
                                   tail[:, :IDX_DIM].reshape(bs, ts, IDX_DIM),
                                   jnp.concatenate([hist, u3], axis=1)[:, -POOL_HIST:])):
            lst.append(a)

    kt, vt, kit = state_p
    heads_last = lambda a: a.reshape(depth, bp, N_HEADS, HEAD_DIM, seq).transpose(0, 1, 4, 2, 3)
    return (xp.reshape(bp, seq, D_MODEL), xs.reshape(bs, ts, D_MODEL),
            heads_last(kt), heads_last(vt), kit.transpose(0, 1, 3, 2), jnp.stack(pool_p),
            *[jnp.stack(a) for a in outs_s])


def kernel(x_prompt, x_sample, cache_k, cache_v, cache_kidx, state_pool, page_table, c_prompt, c_sample,
           rel_bias, ada_w, ada_b, ln1_pre, ln1_post, ln2_pre, ln2_post, w_in, pool_w, pool_scale,
           w_out, w_ff1, w_ff2):
    cfg = Cfg(batch=x_prompt.shape[0], seq=x_prompt.shape[1], dec_batch=x_sample.shape[0],
              dec_seq=x_sample.shape[1], past_len=page_table.shape[1] * PAGE_SIZE, blk=256, tm=1024,
              tm_out=1024)
    return _forward(cfg, x_prompt, x_sample, cache_k, cache_v, cache_kidx, state_pool, page_table, c_prompt,
                    c_sample, rel_bias, ada_w, ada_b, ln1_pre, ln1_post, ln2_pre, ln2_post, w_in, pool_w,
                    pool_scale, w_out, w_ff1, w_ff2)
```

```python
import functools
import math
from typing import NamedTuple

import jax
import jax.numpy as jnp
from jax import lax
from jax.experimental import pallas as pl
from jax.experimental.pallas import tpu as pltpu

F32 = jnp.float32
BF16 = jnp.bfloat16
I32 = jnp.int32

D_MODEL = 1024
C_POOL = 512
D_ATT = 512
HEAD_DIM = 64
N_HEADS = 8
IDX_HEADS = 8
IDX_DIM = 64
POOL_WINDOWS = (2, 4, 8, 16)
POOL_GC = C_POOL // len(POOL_WINDOWS)
POOL_HIST = max(POOL_WINDOWS) - 1
TOPK_MAX = 256
N_BUCKETS = 32
MAX_DISTANCE = 128
D_FF = 4 * D_MODEL
EPS = 1e-6
PAGE_SIZE = 128
OFF_Q = C_POOL
OFF_K = OFF_Q + D_ATT
OFF_V = OFF_K + D_ATT
OFF_QI = OFF_V + D_ATT
OFF_KI = OFF_QI + IDX_HEADS * IDX_DIM
OFF_WI = OFF_KI + IDX_DIM
D_IN = OFF_WI + IDX_HEADS

LANES = 128
SUBLANES = 8
BF16_ROWS = 2 * SUBLANES
D_IN_PAD = ((D_IN + LANES - 1) // LANES) * LANES
TAIL = D_IN_PAD - OFF_KI
POOL_PAD = 16
OUT_SUB_ROWS = 256
IN_SUB_ROWS = 256
SCORE_PAGE_SLOTS = 4
ATTN_PAGE_SLOTS = 3
LATE_ROUNDS = (24, 26, 28, 32)
VMEM_LIMIT = 56 * 1024 * 1024

NEG_INF = float("-inf")
INT_MIN = -(2 ** 31)


class Cfg(NamedTuple):
    batch: int
    seq: int
    dec_batch: int
    dec_seq: int
    past_len: int
    blk: int
    tm: int
    tm_out: int


def _rms(x, g):
    ms = jnp.mean(x * x, axis=-1, keepdims=True)
    return x * lax.rsqrt(ms + EPS) * g


def _nt_dot(a, b):
    return lax.dot_general(a, b, (((1,), (1,)), ((), ())), preferred_element_type=F32)


def _dot(a, b):
    return jnp.dot(a, b, preferred_element_type=F32)


def _key_to_float(key):
    bits = key ^ ((key >> 31) & jnp.int32(0x7FFFFFFF))
    return lax.bitcast_convert_type(bits, F32)


def _lane_rep(col, width=LANES):
    return jnp.broadcast_to(col, (col.shape[0], width))


def _fold_tiles(x, op=jnp.add):
    return _tree([x[r * SUBLANES:(r + 1) * SUBLANES] for r in range(x.shape[0] // SUBLANES)], op)


def _tree(parts, op=jnp.add):
    while len(parts) > 1:
        parts = [op(parts[i], parts[i + 1]) if i + 1 < len(parts) else parts[i] for i in range(0, len(parts), 2)]
    return parts[0]


def _ada_kernel(c_ref, w_ref, b_ref, o_ref):
    o_ref[...] = _dot(c_ref[...].astype(BF16), w_ref[...].astype(BF16)) + b_ref[...]


def _ada_mod(c_all, ada_w, ada_b):
    depth, d, n = ada_w.shape
    rc = c_all.shape[0]
    tn = 1536
    return pl.pallas_call(
        _ada_kernel,
        grid=(depth, n // tn),
        in_specs=[pl.BlockSpec((rc, d), lambda l, j: (0, 0)),
                  pl.BlockSpec((None, d, tn), lambda l, j: (l, 0, j)),
                  pl.BlockSpec((None, 1, tn), lambda l, j: (l, 0, j))],
        out_specs=pl.BlockSpec((None, rc, tn), lambda l, j: (l, 0, j)),
        out_shape=jax.ShapeDtypeStruct((depth, rc, n), F32),
        name="ada_mod",
    )(c_all, ada_w, ada_b.reshape(depth, 1, n))


def _mod_spec(per_batch_rows, tm, chunk):
    if per_batch_rows:
        tiles = per_batch_rows // tm
        return pl.BlockSpec((None, 1, D_MODEL), lambda i: (i // tiles, 0, chunk))
    return pl.BlockSpec((tm, D_MODEL), lambda i: (i, chunk))


def _in_kernel(transposed, n_carried, n_sub, layer, x_ref, sh_ref, sc_ref, ln_ref, w_ref, *refs):
    u_ref, *refs = refs[n_carried:]
    sub = x_ref.shape[0] // n_sub
    mod = lambda ref, rows: ref[...] if ref.shape[0] == 1 else ref[rows, :]

    def put_state(ref, cols, val):
        if len(ref.shape) == 2:
            ref[:, cols] = val
        else:
            for l in range(ref.shape[0]):
                ref[l, :, cols] = val if l == layer else jnp.zeros(val.shape, val.dtype)

    def project(i):
        rows = pl.ds(i * sub, sub)
        h = _rms(x_ref[rows, :], ln_ref[...]) * (1.0 + mod(sc_ref, rows)) + mod(sh_ref, rows)
        return _dot(h.astype(BF16), w_ref[...])

    def emit(i, z):
        rows = pl.ds(i * sub, sub)
        u_ref[rows, :] = z[:, :OFF_Q]
        q = z[:, OFF_Q:OFF_K] * HEAD_DIM ** -0.5
        k = z[:, OFF_K:OFF_V]
        v = z[:, OFF_V:OFF_QI]
        qi = z[:, OFF_QI:OFF_KI]
        tail = z[:, OFF_KI:]
        if transposed:
            kb_ref, tailb_ref, kt_ref, vt_ref, kit_ref, wit_ref, qt_ref, qit_ref, vbt_ref = refs
            cols = pl.ds(i * sub, sub)
            kb_ref[rows, :] = k.astype(BF16)
            tailb_ref[rows, :] = tail.astype(BF16)
            put_state(kt_ref, cols, k.T)
            vt = v.T
            put_state(vt_ref, cols, vt)
            vbt_ref[:, cols] = vt.astype(BF16)
            tail_t = tail.T
            put_state(kit_ref, cols, tail_t[:IDX_DIM])
            wit_ref[:, cols] = tail_t[OFF_WI - OFF_KI:OFF_WI - OFF_KI + IDX_HEADS]
            qt_ref[:, cols] = q.T.astype(BF16)
            qit_ref[:, cols] = qi.T.astype(BF16)
        else:
            q_ref, qi_ref, tail_ref, k_ref, v_ref = refs
            q_ref[rows, :] = q.astype(BF16)
            qi_ref[rows, :] = qi.astype(BF16)
            tail_ref[rows, :] = tail
            k_ref[rows, :] = k
            v_ref[rows, :] = v

    z = project(0)
    for i in range(n_sub):
        z_next = project(i + 1) if i + 1 < n_sub else None
        emit(i, z)
        z = z_next


def _in_proj(x, mod, per_batch_rows, ln, w_in_b, tm, layer=0, depth=1, carried=()):
    r = x.shape[0]
    row = lambda width: pl.BlockSpec((tm, width), lambda i: (i, 0))
    const = lambda shape: pl.BlockSpec(shape, lambda i: (0, 0))
    in_specs = [row(D_MODEL), _mod_spec(per_batch_rows, tm, 0), _mod_spec(per_batch_rows, tm, 1),
                const((1, D_MODEL)), pl.BlockSpec((None, D_MODEL, D_IN_PAD), lambda i: (layer, 0, 0))]
    aliases = {}
    if per_batch_rows:
        tiles = per_batch_rows // tm
        nb = r // per_batch_rows
        rows_out = [(C_POOL, F32), (D_ATT, BF16), (TAIL, BF16)]
        state_out = [D_ATT, D_ATT, IDX_DIM]
        cols_out = [(IDX_HEADS, F32), (D_ATT, BF16), (IDX_HEADS * IDX_DIM, BF16), (D_ATT, BF16)]
    else:
        rows_out = [(C_POOL, F32), (D_ATT, BF16), (IDX_HEADS * IDX_DIM, BF16), (TAIL, F32), (D_ATT, F32), (D_ATT, F32)]
        state_out, cols_out = [], []
    out_specs = [row(w) for w, _ in rows_out]
    out_shape = [jax.ShapeDtypeStruct((r, w), dt) for w, dt in rows_out]
    for n, width in enumerate(state_out):
        if carried:
            aliases[len(in_specs)] = len(out_specs)
            in_specs.append(pl.BlockSpec(memory_space=pl.ANY))
            out_specs.append(pl.BlockSpec((None, None, width, tm), lambda i: (layer, i // tiles, 0, i % tiles)))
        else:
            out_specs.append(pl.BlockSpec((depth, None, width, tm), lambda i: (0, i // tiles, 0, i % tiles)))
        out_shape.append(jax.ShapeDtypeStruct((depth, nb, width, per_batch_rows), F32))
    for width, dt in cols_out:
        out_specs.append(pl.BlockSpec((None, width, tm), lambda i: (i // tiles, 0, i % tiles)))
        out_shape.append(jax.ShapeDtypeStruct((nb, width, per_batch_rows), dt))
    return pl.pallas_call(
        functools.partial(_in_kernel, bool(per_batch_rows), len(carried), max(1, tm // IN_SUB_ROWS), layer),
        grid=(r // tm,),
        in_specs=in_specs,
        out_specs=out_specs,
        out_shape=out_shape,
        input_output_aliases=aliases,
        compiler_params=pltpu.CompilerParams(vmem_limit_bytes=VMEM_LIMIT),
        name="in_proj",
    )(x, mod, mod, ln, w_in_b, *carried)


def _out_kernel(n_sub, x_ref, yp_ref, ya_ref, g1_ref, sh2_ref, sc2_ref, g2_ref,
                ln1_ref, ln2a_ref, ln2b_ref, wo_ref, w1_ref, w2_ref, o_ref):
    sub = x_ref.shape[0] // n_sub
    rows = [pl.ds(i * sub, sub) for i in range(n_sub)]
    mod = lambda ref, i: ref[...] if ref.shape[0] == 1 else ref[rows[i], :]

    def out_proj(i):
        return (_dot(yp_ref[rows[i], :].astype(BF16), wo_ref[:C_POOL, :])
                + _dot(ya_ref[rows[i], :].astype(BF16), wo_ref[C_POOL:, :]))

    def residual_and_prenorm(i, mixed):
        x1 = x_ref[rows[i], :] + mod(g1_ref, i) * _rms(mixed, ln1_ref[...])
        h = (_rms(x1, ln2a_ref[...]) * (1.0 + mod(sc2_ref, i)) + mod(sh2_ref, i)).astype(BF16)
        return x1, h

    def mlp_chunk(h, f, c):
        sl = slice(c * D_MODEL, (c + 1) * D_MODEL)
        a = jnp.maximum(_dot(h, w1_ref[:, sl]), 0.0)
        return f + _dot((a * a).astype(BF16), w2_ref[sl, :])

    def finish(i, x1, f):
        o_ref[rows[i], :] = x1 + mod(g2_ref, i) * _rms(f, ln2b_ref[...])

    n_chunks = D_FF // D_MODEL
    mixed = [out_proj(i) for i in range(n_sub)]
    x1, h = residual_and_prenorm(0, mixed[0])
    for i in range(n_sub):
        f = mlp_chunk(h, jnp.zeros(x1.shape, F32), 0)
        if i + 1 < n_sub:
            nxt = residual_and_prenorm(i + 1, mixed[i + 1])
        if i > 0:
            finish(i - 1, *done)
        for c in range(1, n_chunks):
            f = mlp_chunk(h, f, c)
        done = (x1, f)
        if i + 1 < n_sub:
            x1, h = nxt
    finish(n_sub - 1, *done)


def _out_ffn(x, yp, ya, mod, per_batch_rows, ln1_post, ln2_pre, ln2_post, wo_b, w1_b, w2_b, layer, tm):
    r = x.shape[0]
    row = lambda width: pl.BlockSpec((tm, width), lambda i: (i, 0))
    const = lambda shape: pl.BlockSpec(shape, lambda i: (0, 0), pipeline_mode=pl.Buffered(1))
    weight = lambda shape: pl.BlockSpec((None,) + shape, lambda i: (layer, 0, 0), pipeline_mode=pl.Buffered(1))
    ms = lambda chunk: _mod_spec(per_batch_rows, tm, chunk)
    return pl.pallas_call(
        functools.partial(_out_kernel, max(1, tm // OUT_SUB_ROWS)),
        grid=(r // tm,),
        in_specs=[row(D_MODEL), row(C_POOL), row(D_ATT), ms(2), ms(3), ms(4), ms(5),
                  const((1, D_MODEL)), const((1, D_MODEL)), const((1, D_MODEL)),
                  weight((D_MODEL, D_MODEL)), weight((D_MODEL, D_FF)), weight((D_FF, D_MODEL))],
        out_specs=row(D_MODEL),
        out_shape=jax.ShapeDtypeStruct((r, D_MODEL), F32),
        compiler_params=pltpu.CompilerParams(vmem_limit_bytes=VMEM_LIMIT),
        name="out_ffn",
    )(x, yp, ya, mod, mod, mod, mod, ln1_post, ln2_pre, ln2_post, wo_b, w1_b, w2_b)


def _pool_kernel(pos0, t_rows, chunk, has_hist, u_ref, *refs):
    if has_hist:
        hist_ref, pw_ref, ps_ref, o_ref, ue_ref = refs
        ue_ref[:, :POOL_PAD, :] = hist_ref[...]
    else:
        pw_ref, ps_ref, o_ref, ue_ref = refs
        ue_ref[:, :POOL_PAD, :] = jnp.zeros((ue_ref.shape[0], POOL_PAD, C_POOL), F32)
    ue_ref[:, POOL_PAD:, :] = u_ref[...]
    g_elems = ue_ref.shape[0]
    n_chunks = t_rows // chunk
    for ci in range(n_chunks):
        r0 = ci * chunk
        pos = pos0 + r0 + lax.broadcasted_iota(I32, (chunk, POOL_GC), 0)
        for g, w in enumerate(POOL_WINDOWS):
            lanes = pl.ds(g * POOL_GC, POOL_GC)
            cnt = jnp.minimum(pos + 1, w).astype(F32)
            pooled = []
            for e in range(g_elems):
                cur = ue_ref[e, pl.ds(POOL_PAD + r0, chunk), lanes]
                acc = cur
                for j in range(1, w):
                    acc = acc + ue_ref[e, pl.ds(POOL_PAD + r0 - j, chunk), lanes]
                pooled.append(acc / cnt - cur)
            pooled = pooled[0] if g_elems == 1 else jnp.concatenate(pooled, axis=0)
            y = _dot(pooled.astype(BF16), pw_ref[g].astype(BF16))
            y = y * ps_ref[:, lanes]
            for e in range(g_elems):
                o_ref[e, pl.ds(r0, chunk), lanes] = y[e * chunk:(e + 1) * chunk].astype(o_ref.dtype)


def _pool_mix(u, hist, pos0, pool_w_l, pool_scale_l, g_elems, chunk):
    b, t_rows, _ = u.shape
    per_elem = lambda rows: pl.BlockSpec((g_elems, rows, C_POOL), lambda i: (i, 0, 0))
    in_specs = [per_elem(t_rows)] + ([per_elem(POOL_PAD)] if hist is not None else []) + [
        pl.BlockSpec((len(POOL_WINDOWS), POOL_GC, POOL_GC), lambda i: (0, 0, 0)),
        pl.BlockSpec((1, C_POOL), lambda i: (0, 0))]
    args = (u,) + ((hist,) if hist is not None else ()) + (pool_w_l, pool_scale_l)
    return pl.pallas_call(
        functools.partial(_pool_kernel, pos0, t_rows, chunk, hist is not None),
        grid=(b // g_elems,),
        in_specs=in_specs,
        out_specs=per_elem(t_rows),
        out_shape=jax.ShapeDtypeStruct((b, t_rows, C_POOL), F32),
        scratch_shapes=[pltpu.VMEM((g_elems, POOL_PAD + t_rows, C_POOL), F32)],
        compiler_params=pltpu.CompilerParams(vmem_limit_bytes=VMEM_LIMIT),
        name="pool_mix",
    )(*args)


def _bias_of_dist(dist, rb_ref, h):
    n = jnp.maximum(dist, 0)
    max_exact = N_BUCKETS // 2
    large = max_exact + (jnp.log(jnp.maximum(n, 1).astype(F32) / max_exact)
                         / math.log(MAX_DISTANCE / max_exact) * (N_BUCKETS - max_exact)).astype(I32)
    large = jnp.minimum(large, N_BUCKETS - 1)
    bucket = jnp.where(n < max_exact, n, large)
    out = jnp.zeros(dist.shape, F32)
    for b in range(N_BUCKETS):
        out = jnp.where(bucket == b, rb_ref[b, h], out)
    return out


def _bias_kernel(blk, rb_ref, tb_ref, s_far_ref, s_last_ref, s_new_ref):
    key = lax.broadcasted_iota(I32, (blk, blk), 0)
    qry = lax.broadcasted_iota(I32, (blk, blk), 1)
    t = lax.broadcasted_iota(I32, (SUBLANES, LANES), 0)
    jj = lax.broadcasted_iota(I32, (SUBLANES, LANES), 1)
    for h in range(N_HEADS):
        tb_ref[0, h] = _bias_of_dist(jnp.full((blk, blk), 2 * blk, I32), rb_ref, h)
        tb_ref[1, h] = _bias_of_dist(blk + qry - key, rb_ref, h)
        tb_ref[2, h] = _bias_of_dist(qry - key, rb_ref, h)
        rows = pl.ds(h * SUBLANES, SUBLANES)
        s_far_ref[rows, :] = _bias_of_dist(jnp.full((SUBLANES, LANES), 2 * PAGE_SIZE, I32), rb_ref, h)
        s_last_ref[rows, :] = _bias_of_dist(PAGE_SIZE + t - jj, rb_ref, h)
        s_new_ref[rows, :] = _bias_of_dist(t - jj, rb_ref, h)


def _bias_tables(rel_bias, blk):
    sm = jax.ShapeDtypeStruct((N_HEADS * SUBLANES, LANES), F32)
    return pl.pallas_call(
        functools.partial(_bias_kernel, blk),
        in_specs=[pl.BlockSpec(memory_space=pltpu.SMEM)],
        out_shape=[jax.ShapeDtypeStruct((3, N_HEADS, blk, blk), F32), sm, sm, sm],
        compiler_params=pltpu.CompilerParams(vmem_limit_bytes=VMEM_LIMIT),
        name="bias_tables",
    )(rel_bias)


def _kth_start(shape):
    return jnp.full(shape, INT_MIN, I32), jnp.zeros(shape, F32)


def _kth_rounds(count_ge_key, k, carry, first_round, last_round, digit_bits=1):
    def body(r, carry):
        tkey, nge = carry
        shift = 32 - digit_bits * (r + 1)
        best_key, best_cnt = tkey, nge
        for j in range(1, 1 << digit_bits):
            ckey = tkey + lax.shift_left(jnp.int32(j), shift)
            cnt = count_ge_key(ckey)
            ok = cnt >= k
            best_key = jnp.where(ok, ckey, best_key)
            best_cnt = jnp.where(ok, cnt, best_cnt)
        return best_key, best_cnt

    return lax.fori_loop(first_round, last_round, body, carry)


def _kth_finish(carry):
    tkey, nge = carry
    return jnp.where(tkey == INT_MIN, NEG_INF, _key_to_float(tkey)), nge


def _kth_largest(count_ge, k, shape, digit_bits=1):
    carry = _kth_rounds(lambda ckey: count_ge(_key_to_float(ckey)), k, _kth_start(shape),
                        0, 32 // digit_bits, digit_bits)
    return _kth_finish(carry)


def _tie_limit(count_eq_below, need, shape, bits):
    def body(it, lim):
        cand = lim + lax.shift_left(jnp.int32(1), bits - 1 - it)
        return jnp.where(count_eq_below(cand) <= need, cand, lim)

    return lax.fori_loop(0, bits, body, jnp.zeros(shape, I32))


def _count_tiles(s, first_key, pred):
    row = lax.broadcasted_iota(I32, (SUBLANES, s.shape[1]), 0)
    return _tree([pred(s[r * SUBLANES:(r + 1) * SUBLANES], row + (first_key + r * SUBLANES))
                  for r in range(s.shape[0] // SUBLANES)])


def _resolve_ties(k_sel, thr, nge, count_where, idx_bits, lim_ref):
    lim_ref[...] = jnp.full(lim_ref.shape, 1 << idx_bits, I32)

    @pl.when(jnp.max(nge) > k_sel)
    def _():
        n_gt = count_where(lambda s, idx: jnp.where(s > thr, 1.0, 0.0))
        need = k_sel - n_gt
        lim_ref[...] = _tie_limit(
            lambda cand: count_where(lambda s, idx: jnp.where(s == thr, jnp.where(idx < cand, 1.0, 0.0), 0.0)),
            need, lim_ref.shape, idx_bits)


def _select_mask(s, idx, thr, lim):
    tie = jnp.where(s == thr, jnp.where(idx < lim, 0.0, NEG_INF), NEG_INF)
    m = jnp.where(s > thr, 0.0, tie)
    return jnp.where(s == NEG_INF, NEG_INF, m)


def _attn_prompt_kernel(k_sel, blk, seq,
                        qt_ref, qit_ref, wit_ref, kb_ref, kib_ref, vbt_ref, tb_ref,
                        o_ref,
                        score_ref, trunc_ref, lg_ref, mrun_ref, lsum_ref, acc_ref, thr_ref, lim_ref,
                        tkey_ref, nge_ref):
    qb = pl.program_id(1)
    n_chunks = qb + 1
    rep = (SUBLANES, blk)
    idx_bits = max(1, (seq - 1).bit_length()) + 1

    key_row = lax.broadcasted_iota(I32, (blk, blk), 0)
    qry_col = lax.broadcasted_iota(I32, (blk, blk), 1)

    def chunk_keys(c):
        return pl.ds(pl.multiple_of(c * blk, blk), blk)

    def all_sublanes(x, op):
        return jnp.broadcast_to(op(x, axis=0, keepdims=True), rep)

    w_idx = wit_ref[...] * IDX_HEADS ** -0.5 * IDX_DIM ** -0.5

    def score_chunk(c, carry):
        keys = chunk_keys(c)
        kic = kib_ref[keys, :][:, :IDX_DIM]
        s = jnp.zeros((blk, blk), F32)
        for h in range(IDX_HEADS):
            d = _dot(kic, qit_ref[h * IDX_DIM:(h + 1) * IDX_DIM, :])
            s = s + w_idx[h:h + 1, :] * jnp.maximum(d, 0.0)
        s = jnp.where(key_row + c * blk <= qry_col + qb * blk, s, NEG_INF)
        score_ref[keys, :] = s
        hi = lax.bitcast_convert_type(s, I32) & jnp.int32(-(1 << 16))
        trunc_ref[keys, :] = lax.bitcast_convert_type(hi, F32).astype(BF16)
        return carry

    lax.fori_loop(0, n_chunks, score_chunk, 0)

    def count_where(pred):
        def body(c, acc):
            return acc + _count_tiles(score_ref[chunk_keys(c), :], c * blk, pred)
        acc = lax.fori_loop(0, n_chunks, body, jnp.zeros(rep, F32))
        return all_sublanes(acc, jnp.sum)

    def count_upper_half(ckey):
        bits = ckey ^ ((ckey >> 31) & jnp.int32(0x7FFFFFFF))
        cf = lax.bitcast_convert_type(bits & jnp.int32(-(1 << 16)), F32)
        cb = jnp.concatenate([cf, cf], axis=0).astype(BF16)
        one, zero = jnp.ones((BF16_ROWS, blk), BF16), jnp.zeros((BF16_ROWS, blk), BF16)

        def body(c, acc):
            t = trunc_ref[chunk_keys(c), :]
            return acc + _tree([jnp.where(t[r * BF16_ROWS:(r + 1) * BF16_ROWS] >= cb, one, zero)
                                for r in range(blk // BF16_ROWS)])
        acc = lax.fori_loop(0, n_chunks, body, zero)
        return all_sublanes(acc.astype(F32), jnp.sum)

    carry = _kth_rounds(count_upper_half, k_sel, _kth_start(rep), 0, 16)
    def count_ge_key(ckey):
        cf = _key_to_float(ckey)
        return count_where(lambda s, idx: jnp.where(s >= cf, 1.0, 0.0))

    first_f32 = 16
    tkey_ref[...], nge_ref[...] = _kth_rounds(count_ge_key, k_sel, carry, first_f32, LATE_ROUNDS[0])
    for first, last in zip(LATE_ROUNDS[:-1], LATE_ROUNDS[1:]):
        @pl.when(jnp.max(nge_ref[...]) > k_sel)
        def _():
            tkey_ref[...], nge_ref[...] = _kth_rounds(
                count_ge_key, k_sel, (tkey_ref[...], nge_ref[...]), first, last)
    thr, nge = _kth_finish((tkey_ref[...], nge_ref[...]))
    thr_ref[...] = thr
    _resolve_ties(k_sel, thr, nge, count_where, idx_bits, lim_ref)

    def mask_chunk(c, carry):
        keys = chunk_keys(c)
        score_ref[keys, :] = _select_mask(score_ref[keys, :], key_row + c * blk,
                                          thr_ref[0:1, :], lim_ref[0:1, :])
        return carry

    lax.fori_loop(0, n_chunks, mask_chunk, 0)

    mrun_ref[...] = jnp.full(mrun_ref.shape, NEG_INF, F32)

    def logits_chunk(c, carry):
        keys = chunk_keys(c)
        madd = score_ref[keys, :]
        table = jnp.clip(c - (qb - 2), 0, 2)
        for h in range(N_HEADS):
            hs = slice(h * HEAD_DIM, (h + 1) * HEAD_DIM)
            s = _dot(kb_ref[keys, hs], qt_ref[hs, :]) + tb_ref[table, h] + madd
            lg_ref[h, keys, :] = s
            mrun_ref[h] = jnp.maximum(mrun_ref[h], _fold_tiles(s, jnp.maximum))
        return carry

    lax.fori_loop(0, n_chunks, logits_chunk, 0)

    for h in range(N_HEADS):
        mrun_ref[h] = all_sublanes(mrun_ref[h], jnp.max)
    lsum_ref[...] = jnp.zeros(lsum_ref.shape, F32)
    acc_ref[...] = jnp.zeros(acc_ref.shape, F32)

    def pv_chunk(c, carry):
        keys = chunk_keys(c)
        for h in range(N_HEADS):
            hs = slice(h * HEAD_DIM, (h + 1) * HEAD_DIM)
            p = jnp.exp(lg_ref[h, keys, :] - mrun_ref[h, 0:1, :])
            lsum_ref[h] += _fold_tiles(p)
            acc_ref[h] += _dot(vbt_ref[hs, keys], p.astype(BF16))
        return carry

    lax.fori_loop(0, n_chunks, pv_chunk, 0)

    outs =[acc_ref[h] / jnp.sum(lsum_ref[h], axis=0, keepdims=True) for h in range(N_HEADS)]
    o_ref[...] = jnp.concatenate(outs, axis=0).T.astype(o_ref.dtype)


def _attn_prompt(cfg, k_sel, qt, qit, wit, kb, kib, vbt, tb):
    blk, seq = cfg.blk, cfg.seq
    nq = seq // blk
    qspec = lambda width: pl.BlockSpec((None, width, blk), lambda b, i: (b, 0, i))
    kspec = lambda width: pl.BlockSpec((seq, width), lambda b, i: (b, 0))
    return pl.pallas_call(
        functools.partial(_attn_prompt_kernel, k_sel, blk, seq),
        grid=(cfg.batch, nq),
        in_specs=[qspec(D_ATT), qspec(IDX_HEADS * IDX_DIM), qspec(IDX_HEADS),
                  kspec(D_ATT), kspec(TAIL), pl.BlockSpec((None, D_ATT, seq), lambda b, i: (b, 0, 0)),
                  pl.BlockSpec((3, N_HEADS, blk, blk), lambda b, i: (0, 0, 0, 0), pipeline_mode=pl.Buffered(1))],
        out_specs=pl.BlockSpec((blk, D_ATT), lambda b, i: (b * nq + i, 0)),
        out_shape=jax.ShapeDtypeStruct((cfg.batch * seq, D_ATT), BF16),
        scratch_shapes=[pltpu.VMEM((seq, blk), F32),
                        pltpu.VMEM((seq, blk), BF16),
                        pltpu.VMEM((N_HEADS, seq, blk), F32),
                        pltpu.VMEM((N_HEADS, SUBLANES, blk), F32),
                        pltpu.VMEM((N_HEADS, SUBLANES, blk), F32),
                        pltpu.VMEM((N_HEADS, HEAD_DIM, blk), F32),
                        pltpu.VMEM((SUBLANES, blk), F32),
                        pltpu.VMEM((SUBLANES, blk), I32),
                        pltpu.VMEM((SUBLANES, blk), I32),
                        pltpu.VMEM((SUBLANES, blk), F32)],
        compiler_params=pltpu.CompilerParams(vmem_limit_bytes=VMEM_LIMIT),
        name="attn_prompt",
    )(qt, qit, wit, kb, kib, vbt, tb)


def _pad_rows(x, rows):
    return jnp.concatenate([x, jnp.zeros((rows - x.shape[0], x.shape[1]), x.dtype)], axis=0)


def _fetch_pages(pt_ref, layer, n_steps, n_pages, caches):
    n_slots = caches[0][1].shape[0]
    ahead = n_slots - 1
    i = pl.program_id(0)
    slot = i % n_slots

    def copy(cache, b, s, p):
        hbm, buf, sem = cache
        return pltpu.make_async_copy(hbm.at[layer, pt_ref[b, p]], buf.at[s, p], sem.at[s])

    def start(b, s):
        for p in range(n_pages):
            for cache in caches:
                copy(cache, b, s, p).start()

    @pl.when(i == 0)
    def _():
        for b in range(min(ahead, n_steps)):
            start(b, b)

    @pl.when(i + ahead < n_steps)
    def _():
        start(i + ahead, (i + ahead) % n_slots)

    for p in range(n_pages):
        for cache in caches:
            copy(cache, i, slot, p).wait()
    return slot


def _page_scratch(n_slots, n_pages, rows):
    return [pltpu.VMEM((n_slots, n_pages, rows, PAGE_SIZE), F32), pltpu.SemaphoreType.DMA((n_slots,))]


def _score_sample_kernel(layer, n_steps, n_pages,
                         pt_ref, qi_ref, wrep_ref, tailn_ref, ki_hbm, o_ref, ki_buf, ki_sem):
    slot = _fetch_pages(pt_ref, layer, n_steps, n_pages, [(ki_hbm, ki_buf, ki_sem)])
    shape = (SUBLANES, LANES)
    qi = qi_ref[...]
    w = wrep_ref[...]

    def score_block(dots):
        return _fold_tiles(w * jnp.maximum(dots * IDX_DIM ** -0.5, 0.0))

    for p in range(n_pages):
        o_ref[:, p * PAGE_SIZE:(p + 1) * PAGE_SIZE] = score_block(_dot(qi, ki_buf[slot, p].astype(BF16)))
    ki_new = _pad_rows(tailn_ref[...][:, :IDX_DIM], PAGE_SIZE).astype(BF16)
    causal = lax.broadcasted_iota(I32, shape, 1) <= lax.broadcasted_iota(I32, shape, 0)
    o_ref[:, n_pages * PAGE_SIZE:] = jnp.where(causal, score_block(_nt_dot(qi, ki_new)), NEG_INF)


def _score_sample(cfg, layer, page_table, qi_ht, wrep, tail_new, cache_ki):
    b, t_rows = cfg.dec_batch, cfg.dec_seq
    n_pages = cfg.past_len // PAGE_SIZE
    rows = N_HEADS * t_rows
    per_b = lambda r, w: pl.BlockSpec((None, r, w), lambda i, pt: (i, 0, 0))
    n_keys = (n_pages + 1) * PAGE_SIZE
    return pl.pallas_call(
        functools.partial(_score_sample_kernel, layer, b, n_pages),
        grid_spec=pltpu.PrefetchScalarGridSpec(
            num_scalar_prefetch=1, grid=(b,),
            in_specs=[per_b(rows, IDX_DIM), per_b(rows, LANES), per_b(t_rows, TAIL),
                      pl.BlockSpec(memory_space=pl.ANY)],
            out_specs=per_b(t_rows, n_keys),
            scratch_shapes=_page_scratch(SCORE_PAGE_SLOTS, n_pages, IDX_DIM)),
        out_shape=jax.ShapeDtypeStruct((b, t_rows, n_keys), F32),
        name="score_sample",
    )(page_table, qi_ht, wrep, tail_new, cache_ki)


def _threshold_kernel(k_sel, sc_ref, thr_ref, lim_ref, s_ref):
    n_keys, cols = s_ref.shape
    s_ref[...] = sc_ref[...].reshape(cols, n_keys).T
    rep = (SUBLANES, cols)
    idx_bits = (n_keys - 1).bit_length() + 1

    def count_where(pred):
        acc = _tree([_count_tiles(s_ref[c * LANES:(c + 1) * LANES, :], c * LANES, pred)
                     for c in range(n_keys // LANES)])
        return jnp.broadcast_to(jnp.sum(acc, axis=0, keepdims=True), rep)

    thr, nge = _kth_largest(lambda cf: count_where(lambda s, idx: jnp.where(s >= cf, 1.0, 0.0)), k_sel, rep)
    thr_ref[...] = thr
    _resolve_ties(k_sel, thr, nge, count_where, idx_bits, lim_ref)


def _threshold_cols(k_sel, scores, cols):
    b, t_rows, n_keys = scores.shape
    n_q = b * t_rows
    spec = pl.BlockSpec((SUBLANES, cols), lambda i: (0, i))
    return pl.pallas_call(
        functools.partial(_threshold_kernel, k_sel),
        grid=(n_q // cols,),
        in_specs=[pl.BlockSpec((cols // t_rows, t_rows, n_keys), lambda i: (i, 0, 0))],
        out_specs=[spec, spec],
        out_shape=[jax.ShapeDtypeStruct((SUBLANES, n_q), F32), jax.ShapeDtypeStruct((SUBLANES, n_q), I32)],
        scratch_shapes=[pltpu.VMEM((n_keys, cols), F32)],
        name="threshold_cols",
    )(scores)


def _attn_sample_kernel(layer, n_steps, n_pages, pt_ref,
                        sc_ref, thr_ref, lim_ref, q_ref, kn_ref, vn_ref, bfar_ref, blast_ref, bnew_ref,
                        k_hbm, v_hbm, o_ref, k_buf, k_sem, v_buf, v_sem):
    slot = _fetch_pages(pt_ref, layer, n_steps, n_pages, [(k_hbm, k_buf, k_sem), (v_hbm, v_buf, v_sem)])
    t_rows = SUBLANES
    n_blocks = n_pages + 1
    lane = lax.broadcasted_iota(I32, (t_rows, LANES), 1)
    thr, lim = thr_ref[...], lim_ref[...]
    masks = [_select_mask(sc_ref[:, b * PAGE_SIZE:(b + 1) * PAGE_SIZE], lane + b * PAGE_SIZE, thr, lim)
             for b in range(n_blocks)]

    def pad_new(x):
        return _pad_rows(x, PAGE_SIZE)

    rows = N_HEADS * t_rows
    same_head = (lax.broadcasted_iota(I32, (rows, D_ATT), 0) // t_rows
                 == lax.broadcasted_iota(I32, (rows, D_ATT), 1) // HEAD_DIM)
    q_bd = jnp.where(same_head, jnp.concatenate([q_ref[...]] * N_HEADS, axis=0), 0.0).astype(BF16)

    def tile_heads(m):
        return jnp.concatenate([m] * N_HEADS, axis=0)

    logits = []
    for p in range(n_pages):
        bias = blast_ref[...] if p == n_pages - 1 else bfar_ref[...]
        logits.append(_dot(q_bd, k_buf[slot, p].astype(BF16)) + bias + tile_heads(masks[p]))
    logits.append(_nt_dot(q_bd, pad_new(kn_ref[...]).astype(BF16)) + bnew_ref[...] + tile_heads(masks[n_pages]))

    m = logits[0]
    for lg in logits[1:]:
        m = jnp.maximum(m, lg)
    m = _lane_rep(jnp.max(m, axis=1, keepdims=True))
    lsum = jnp.zeros((rows, LANES), F32)
    acc = jnp.zeros((rows, D_ATT), F32)
    for b in range(n_blocks):
        p = jnp.exp(logits[b] - m)
        lsum = lsum + p
        if b < n_pages:
            acc = acc + _nt_dot(p.astype(BF16), v_buf[slot, b].astype(BF16))
        else:
            acc = acc + _dot(p.astype(BF16), pad_new(vn_ref[...]).astype(BF16))
    out = jnp.where(same_head, acc / jnp.sum(lsum, axis=1, keepdims=True), 0.0)
    o_ref[...] = _fold_tiles(out)


def _attn_sample(cfg, layer, page_table, scores, thr_rep, lim_rep, q, k_new, v_new,
                 s_far, s_last, s_new, cache_k, cache_v):
    b = cfg.dec_batch
    t_rows = cfg.dec_seq
    n_pages = cfg.past_len // PAGE_SIZE
    rows = N_HEADS * t_rows
    per_b = lambda r, w: pl.BlockSpec((None, r, w), lambda i, pt: (i, 0, 0))
    const = lambda r, w: pl.BlockSpec((r, w), lambda i, pt: (0, 0))
    in_hbm = pl.BlockSpec(memory_space=pl.ANY)
    in_specs = [per_b(t_rows, scores.shape[2]), per_b(t_rows, LANES), per_b(t_rows, LANES),
                per_b(t_rows, D_ATT), per_b(t_rows, D_ATT), per_b(t_rows, D_ATT),
                const(rows, LANES), const(rows, LANES), const(rows, LANES), in_hbm, in_hbm]
    return pl.pallas_call(
        functools.partial(_attn_sample_kernel, layer, b, n_pages),
        grid_spec=pltpu.PrefetchScalarGridSpec(
            num_scalar_prefetch=1, grid=(b,), in_specs=in_specs, out_specs=per_b(t_rows, D_ATT),
            scratch_shapes=(_page_scratch(ATTN_PAGE_SLOTS, n_pages, D_ATT)
                            + _page_scratch(ATTN_PAGE_SLOTS, n_pages, D_ATT))),
        out_shape=jax.ShapeDtypeStruct((b, t_rows, D_ATT), F32),
        compiler_params=pltpu.CompilerParams(vmem_limit_bytes=VMEM_LIMIT),
        name="attn_sample",
    )(page_table, scores, thr_rep, lim_rep, q, k_new, v_new, s_far, s_last, s_new, cache_k, cache_v)


def _forward(cfg, x_prompt, x_sample, cache_k, cache_v, cache_kidx, state_pool, page_table, c_prompt,
             c_sample, rel_bias, ada_w, ada_b, ln1_pre, ln1_post, ln2_pre, ln2_post, w_in, pool_w,
             pool_scale, w_out, w_ff1, w_ff2):
    depth = ada_w.shape[0]
    bp, seq, bs, ts = cfg.batch, cfg.seq, cfg.dec_batch, cfg.dec_seq
    assert ts == SUBLANES and seq % cfg.blk == 0 and cfg.past_len % PAGE_SIZE == 0
    assert cfg.blk >= PAGE_SIZE and seq >= TOPK_MAX and seq // BF16_ROWS <= 256
    k_prompt = min(TOPK_MAX, seq // 4)
    k_sample = min(TOPK_MAX, (cfg.past_len + ts) // 4)
    rp, rs = bp * seq, bs * ts
    tm_s = min(IN_SUB_ROWS, rs)

    mod = _ada_mod(jnp.concatenate([c_prompt, c_sample], axis=0), ada_w, ada_b)
    tb, s_far, s_last, s_new = _bias_tables(rel_bias, cfg.blk)

    n_phys = cache_k.shape[1]
    cache_kt = cache_k.transpose(0, 1, 3, 4, 2).reshape(depth, n_phys, D_ATT, PAGE_SIZE)
    cache_vt = cache_v.transpose(0, 1, 3, 4, 2).reshape(depth, n_phys, D_ATT, PAGE_SIZE)
    cache_kit = cache_kidx.transpose(0, 1, 3, 2)

    xp = x_prompt.reshape(rp, D_MODEL)
    xs = x_sample.reshape(rs, D_MODEL)
    state_p = ()
    pool_p = []
    outs_s = [[], [], [], []]
    w_in_b = jnp.pad(w_in, ((0, 0), (0, 0), (0, D_IN_PAD - D_IN))).astype(BF16)
    wo_b, w1_b, w2_b = w_out.astype(BF16), w_ff1.astype(BF16), w_ff2.astype(BF16)
    for l in range(depth):
        lnrow = lambda a: a[l].reshape(1, D_MODEL)
        mod_p = mod[l, :bp].reshape(bp, 1, 6 * D_MODEL)
        mod_s = jnp.repeat(mod[l, bp:], ts, axis=0)
        psc = pool_scale[l].reshape(1, C_POOL)

        tm_in = min(seq, cfg.tm if state_p else max(IN_SUB_ROWS, cfg.tm // depth))
        u, kb, tailb, *state_p, wit, qt, qit, vbt = _in_proj(
            xp, mod_p, seq, lnrow(ln1_pre), w_in_b, tm_in, layer=l, depth=depth, carried=tuple(state_p))
        u3 = u.reshape(bp, seq, C_POOL)
        y_pool = _pool_mix(u3, None, 0, pool_w[l], psc, 1, min(256, seq)).reshape(rp, C_POOL)
        y_att = _attn_prompt(cfg, k_prompt, qt, qit, wit, kb, tailb, vbt, tb)
        xp = _out_ffn(xp, y_pool, y_att, mod_p, seq, lnrow(ln1_post), lnrow(ln2_pre), lnrow(ln2_post),
                      wo_b, w1_b, w2_b, l, min(cfg.tm_out, seq))
        pool_p.append(u3[:, seq - POOL_HIST:])

        u, q, qi, tail, k, v = _in_proj(xs, mod_s, 0, lnrow(ln1_pre), w_in_b, tm_s, layer=l)
        u3 = u.reshape(bs, ts, C_POOL)
        hist = jnp.concatenate([jnp.zeros((bs, POOL_PAD - POOL_HIST, C_POOL), F32), state_pool[l]], axis=1)
        y_pool = _pool_mix(u3, hist, cfg.past_len, pool_w[l], psc, min(16, bs), ts).reshape(rs, C_POOL)
        qi_ht = qi.reshape(bs, ts, IDX_HEADS, IDX_DIM).transpose(0, 2, 1, 3).reshape(bs, IDX_HEADS * ts, IDX_DIM)
        wi = tail[:, OFF_WI - OFF_KI:OFF_WI - OFF_KI + IDX_HEADS].reshape(bs, ts, IDX_HEADS) * IDX_HEADS ** -0.5
        wrep = jnp.broadcast_to(wi.transpose(0, 2, 1).reshape(bs, IDX_HEADS * ts, 1), (bs, IDX_HEADS * ts, LANES))
        scores = _score_sample(cfg, l, page_table, qi_ht, wrep, tail.reshape(bs, ts, TAIL), cache_kit)
        thr, lim = _threshold_cols(k_sample, scores, min(256, rs))
        per_query = lambda a: jnp.broadcast_to(a[0].reshape(bs, ts, 1), (bs, ts, LANES))
        y_att = _attn_sample(cfg, l, page_table, scores, per_query(thr), per_query(lim),
                             q.astype(F32).reshape(bs, ts, D_ATT),
                             k.reshape(bs, ts, D_ATT), v.reshape(bs, ts, D_ATT),
                             s_far, s_last, s_new, cache_kt, cache_vt).reshape(rs, D_ATT)
        xs = _out_ffn(xs, y_pool, y_att, mod_s, 0, lnrow(ln1_post), lnrow(ln2_pre), lnrow(ln2_post),
                      wo_b, w1_b, w2_b, l, min(OUT_SUB_ROWS, rs))
        for lst, a in zip(outs_s, (k.reshape(bs, ts, N_HEADS, HEAD_DIM), v.reshape(bs, ts, N_HEADS, HEAD_DIM),
                                   tail[:, :IDX_DIM].reshape(bs, ts, IDX_DIM),
                                   jnp.concatenate([hist, u3], axis=1)[:, -POOL_HIST:])):
            lst.append(a)

    kt, vt, kit = state_p
    heads_last = lambda a: a.reshape(depth, bp, N_HEADS, HEAD_DIM, seq).transpose(0, 1, 4, 2, 3)
    return (xp.reshape(bp, seq, D_MODEL), xs.reshape(bs, ts, D_MODEL),
            heads_last(kt), heads_last(vt), kit.transpose(0, 1, 3, 2), jnp.stack(pool_p),
            *[jnp.stack(a) for a in outs_s])


def kernel(x_prompt, x_sample, cache_k, cache_v, cache_kidx, state_pool, page_table, c_prompt, c_sample,
           rel_bias, ada_w, ada_b, ln1_pre, ln1_post, ln2_pre, ln2_post, w_in, pool_w, pool_scale,
           w_out, w_ff1, w_ff2):
    cfg = Cfg(batch=x_prompt.shape[0], seq=x_prompt.shape[1], dec_batch=x_sample.shape[0],
              dec_seq=x_sample.shape[1], past_len=page_table.shape[1] * PAGE_SIZE, blk=256, tm=1024,
              tm_out=1024)
    return _forward(cfg, x_prompt, x_sample, cache_k, cache_v, cache_kidx, state_pool, page_table, c_prompt,
                    c_sample, rel_bias, ada_w, ada_b, ln1_pre, ln1_post, ln2_pre, ln2_post, w_in, pool_w,
                    pool_scale, w_out, w_ff1, w_ff2)
```

```python
import functools
import math
from typing import NamedTuple

import jax
import jax.numpy as jnp
from jax import lax
from jax.experimental import pallas as pl
from jax.experimental.pallas import tpu as pltpu

F32 = jnp.float32
BF16 = jnp.bfloat16
I32 = jnp.int32

D_MODEL = 1024
C_POOL = 512
D_ATT = 512
HEAD_DIM = 64
N_HEADS = 8
IDX_HEADS = 8
IDX_DIM = 64
POOL_WINDOWS = (2, 4, 8, 16)
POOL_GC = C_POOL // len(POOL_WINDOWS)
POOL_HIST = max(POOL_WINDOWS) - 1
TOPK_MAX = 256
N_BUCKETS = 32
MAX_DISTANCE = 128
D_FF = 4 * D_MODEL
EPS = 1e-6
PAGE_SIZE = 128
OFF_Q = C_POOL
OFF_K = OFF_Q + D_ATT
OFF_V = OFF_K + D_ATT
OFF_QI = OFF_V + D_ATT
OFF_KI = OFF_QI + IDX_HEADS * IDX_DIM
OFF_WI = OFF_KI + IDX_DIM
D_IN = OFF_WI + IDX_HEADS

LANES = 128
SUBLANES = 8
BF16_ROWS = 2 * SUBLANES
D_IN_PAD = ((D_IN + LANES - 1) // LANES) * LANES
TAIL = D_IN_PAD - OFF_KI
POOL_PAD = 16
OUT_SUB_ROWS = 256
IN_SUB_ROWS = 256
SCORE_PAGE_SLOTS = 4
ATTN_PAGE_SLOTS = 3
LATE_ROUNDS = (24, 26, 28, 32)
VMEM_LIMIT = 56 * 1024 * 1024

NEG_INF = float("-inf")
INT_MIN = -(2 ** 31)


class Cfg(NamedTuple):
    batch: int
    seq: int
    dec_batch: int
    dec_seq: int
    past_len: int
    blk: int
    tm: int
    tm_out: int


def _rms(x, g):
    ms = jnp.mean(x * x, axis=-1, keepdims=True)
    return x * lax.rsqrt(ms + EPS) * g


def _nt_dot(a, b):
    return lax.dot_general(a, b, (((1,), (1,)), ((), ())), preferred_element_type=F32)


def _dot(a, b):
    return jnp.dot(a, b, preferred_element_type=F32)


def _key_to_float(key):
    bits = key ^ ((key >> 31) & jnp.int32(0x7FFFFFFF))
    return lax.bitcast_convert_type(bits, F32)


def _lane_rep(col, width=LANES):
    return jnp.broadcast_to(col, (col.shape[0], width))


def _fold_tiles(x, op=jnp.add):
    return _tree([x[r * SUBLANES:(r + 1) * SUBLANES] for r in range(x.shape[0] // SUBLANES)], op)


def _tree(parts, op=jnp.add):
    while len(parts) > 1:
        parts = [op(parts[i], parts[i + 1]) if i + 1 < len(parts) else parts[i] for i in range(0, len(parts), 2)]
    return parts[0]


def _ada_kernel(c_ref, w_ref, b_ref, o_ref):
    o_ref[...] = _dot(c_ref[...].astype(BF16), w_ref[...].astype(BF16)) + b_ref[...]


def _ada_mod(c_all, ada_w, ada_b):
    depth, d, n = ada_w.shape
    rc = c_all.shape[0]
    tn = 1536
    return pl.pallas_call(
        _ada_kernel,
        grid=(depth, n // tn),
        in_specs=[pl.BlockSpec((rc, d), lambda l, j: (0, 0)),
                  pl.BlockSpec((None, d, tn), lambda l, j: (l, 0, j)),
                  pl.BlockSpec((None, 1, tn), lambda l, j: (l, 0, j))],
        out_specs=pl.BlockSpec((None, rc, tn), lambda l, j: (l, 0, j)),
        out_shape=jax.ShapeDtypeStruct((depth, rc, n), F32),
        name="ada_mod",
    )(c_all, ada_w, ada_b.reshape(depth, 1, n))


def _mod_spec(per_batch_rows, tm, chunk):
    if per_batch_rows:
        tiles = per_batch_rows // tm
        return pl.BlockSpec((None, 1, D_MODEL), lambda i: (i // tiles, 0, chunk))
    return pl.BlockSpec((tm, D_MODEL), lambda i: (i, chunk))


def _in_kernel(transposed, n_carried, n_sub, layer, x_ref, sh_ref, sc_ref, ln_ref, w_ref, *refs):
    u_ref, *refs = refs[n_carried:]
    sub = x_ref.shape[0] // n_sub
    mod = lambda ref, rows: ref[...] if ref.shape[0] == 1 else ref[rows, :]

    def put_state(ref, cols, val):
        if len(ref.shape) == 2:
            ref[:, cols] = val
        else:
            for l in range(ref.shape[0]):
                ref[l, :, cols] = val if l == layer else jnp.zeros(val.shape, val.dtype)

    def project(i):
        rows = pl.ds(i * sub, sub)
        h = _rms(x_ref[rows, :], ln_ref[...]) * (1.0 + mod(sc_ref, rows)) + mod(sh_ref, rows)
        return _dot(h.astype(BF16), w_ref[...])

    def emit(i, z):
        rows = pl.ds(i * sub, sub)
        u_ref[rows, :] = z[:, :OFF_Q]
        q = z[:, OFF_Q:OFF_K] * HEAD_DIM ** -0.5
        k = z[:, OFF_K:OFF_V]
        v = z[:, OFF_V:OFF_QI]
        qi = z[:, OFF_QI:OFF_KI]
        tail = z[:, OFF_KI:]
        if transposed:
            kb_ref, tailb_ref, kt_ref, vt_ref, kit_ref, wit_ref, qt_ref, qit_ref, vbt_ref = refs
            cols = pl.ds(i * sub, sub)
            kb_ref[rows, :] = k.astype(BF16)
            tailb_ref[rows, :] = tail.astype(BF16)
            put_state(kt_ref, cols, k.T)
            vt = v.T
            put_state(vt_ref, cols, vt)
            vbt_ref[:, cols] = vt.astype(BF16)
            tail_t = tail.T
            put_state(kit_ref, cols, tail_t[:IDX_DIM])
            wit_ref[:, cols] = tail_t[OFF_WI - OFF_KI:OFF_WI - OFF_KI + IDX_HEADS]
            qt_ref[:, cols] = q.T.astype(BF16)
            qit_ref[:, cols] = qi.T.astype(BF16)
        else:
            q_ref, qi_ref, tail_ref, k_ref, v_ref = refs
            q_ref[rows, :] = q.astype(BF16)
            qi_ref[rows, :] = qi.astype(BF16)
            tail_ref[rows, :] = tail
            k_ref[rows, :] = k
            v_ref[rows, :] = v

    z = project(0)
    for i in range(n_sub):
        z_next = project(i + 1) if i + 1 < n_sub else None
        emit(i, z)
        z = z_next


def _in_proj(x, mod, per_batch_rows, ln, w_in_b, tm, layer=0, depth=1, carried=()):
    r = x.shape[0]
    row = lambda width: pl.BlockSpec((tm, width), lambda i: (i, 0))
    const = lambda shape: pl.BlockSpec(shape, lambda i: (0, 0))
    in_specs = [row(D_MODEL), _mod_spec(per_batch_rows, tm, 0), _mod_spec(per_batch_rows, tm, 1),
                const((1, D_MODEL)), pl.BlockSpec((None, D_MODEL, D_IN_PAD), lambda i: (layer, 0, 0))]
    aliases = {}
    if per_batch_rows:
        tiles = per_batch_rows // tm
        nb = r // per_batch_rows
        rows_out = [(C_POOL, F32), (D_ATT, BF16), (TAIL, BF16)]
        state_out = [D_ATT, D_ATT, IDX_DIM]
        cols_out = [(IDX_HEADS, F32), (D_ATT, BF16), (IDX_HEADS * IDX_DIM, BF16), (D_ATT, BF16)]
    else:
        rows_out = [(C_POOL, F32), (D_ATT, BF16), (IDX_HEADS * IDX_DIM, BF16), (TAIL, F32), (D_ATT, F32), (D_ATT, F32)]
        state_out, cols_out = [], []
    out_specs = [row(w) for w, _ in rows_out]
    out_shape = [jax.ShapeDtypeStruct((r, w), dt) for w, dt in rows_out]
    for n, width in enumerate(state_out):
        if carried:
            aliases[len(in_specs)] = len(out_specs)
            in_specs.append(pl.BlockSpec(memory_space=pl.ANY))
            out_specs.append(pl.BlockSpec((None, None, width, tm), lambda i: (layer, i // tiles, 0, i % tiles)))
        else:
            out_specs.append(pl.BlockSpec((depth, None, width, tm), lambda i: (0, i // tiles, 0, i % tiles)))
        out_shape.append(jax.ShapeDtypeStruct((depth, nb, width, per_batch_rows), F32))
    for width, dt in cols_out:
        out_specs.append(pl.BlockSpec((None, width, tm), lambda i: (i // tiles, 0, i % tiles)))
        out_shape.append(jax.ShapeDtypeStruct((nb, width, per_batch_rows), dt))
    return pl.pallas_call(
        functools.partial(_in_kernel, bool(per_batch_rows), len(carried), max(1, tm // IN_SUB_ROWS), layer),
        grid=(r // tm,),
        in_specs=in_specs,
        out_specs=out_specs,
        out_shape=out_shape,
        input_output_aliases=aliases,
        compiler_params=pltpu.CompilerParams(vmem_limit_bytes=VMEM_LIMIT),
        name="in_proj",
    )(x, mod, mod, ln, w_in_b, *carried)


def _out_kernel(n_sub, x_ref, yp_ref, ya_ref, g1_ref, sh2_ref, sc2_ref, g2_ref,
                ln1_ref, ln2a_ref, ln2b_ref, wo_ref, w1_ref, w2_ref, o_ref):
    sub = x_ref.shape[0] // n_sub
    rows = [pl.ds(i * sub, sub) for i in range(n_sub)]
    mod = lambda ref, i: ref[...] if ref.shape[0] == 1 else ref[rows[i], :]

    def out_proj(i):
        return (_dot(yp_ref[rows[i], :].astype(BF16), wo_ref[:C_POOL, :])
                + _dot(ya_ref[rows[i], :].astype(BF16), wo_ref[C_POOL:, :]))

    def residual_and_prenorm(i, mixed):
        x1 = x_ref[rows[i], :] + mod(g1_ref, i) * _rms(mixed, ln1_ref[...])
        h = (_rms(x1, ln2a_ref[...]) * (1.0 + mod(sc2_ref, i)) + mod(sh2_ref, i)).astype(BF16)
        return x1, h

    def mlp_chunk(h, f, c):
        sl = slice(c * D_MODEL, (c + 1) * D_MODEL)
        a = jnp.maximum(_dot(h, w1_ref[:, sl]), 0.0)
        return f + _dot((a * a).astype(BF16), w2_ref[sl, :])

    def finish(i, x1, f):
        o_ref[rows[i], :] = x1 + mod(g2_ref, i) * _rms(f, ln2b_ref[...])

    n_chunks = D_FF // D_MODEL
    mixed = [out_proj(i) for i in range(n_sub)]
    x1, h = residual_and_prenorm(0, mixed[0])
    for i in range(n_sub):
        f = mlp_chunk(h, jnp.zeros(x1.shape, F32), 0)
        if i + 1 < n_sub:
            nxt = residual_and_prenorm(i + 1, mixed[i + 1])
        if i > 0:
            finish(i - 1, *done)
        for c in range(1, n_chunks):
            f = mlp_chunk(h, f, c)
        done = (x1, f)
        if i + 1 < n_sub:
            x1, h = nxt
    finish(n_sub - 1, *done)


def _out_ffn(x, yp, ya, mod, per_batch_rows, ln1_post, ln2_pre, ln2_post, wo_b, w1_b, w2_b, layer, tm):
    r = x.shape[0]
    row = lambda width: pl.BlockSpec((tm, width), lambda i: (i, 0))
    const = lambda shape: pl.BlockSpec(shape, lambda i: (0, 0), pipeline_mode=pl.Buffered(1))
    weight = lambda shape: pl.BlockSpec((None,) + shape, lambda i: (layer, 0, 0), pipeline_mode=pl.Buffered(1))
    ms = lambda chunk: _mod_spec(per_batch_rows, tm, chunk)
    return pl.pallas_call(
        functools.partial(_out_kernel, max(1, tm // OUT_SUB_ROWS)),
        grid=(r // tm,),
        in_specs=[row(D_MODEL), row(C_POOL), row(D_ATT), ms(2), ms(3), ms(4), ms(5),
                  const((1, D_MODEL)), const((1, D_MODEL)), const((1, D_MODEL)),
                  weight((D_MODEL, D_MODEL)), weight((D_MODEL, D_FF)), weight((D_FF, D_MODEL))],
        out_specs=row(D_MODEL),
        out_shape=jax.ShapeDtypeStruct((r, D_MODEL), F32),
        compiler_params=pltpu.CompilerParams(vmem_limit_bytes=VMEM_LIMIT),
        name="out_ffn",
    )(x, yp, ya, mod, mod, mod, mod, ln1_post, ln2_pre, ln2_post, wo_b, w1_b, w2_b)


def _pool_kernel(pos0, t_rows, chunk, has_hist, u_ref, *refs):
    if has_hist:
        hist_ref, pw_ref, ps_ref, o_ref, ue_ref = refs
        ue_ref[:, :POOL_PAD, :] = hist_ref[...]
    else:
        pw_ref, ps_ref, o_ref, ue_ref = refs
        ue_ref[:, :POOL_PAD, :] = jnp.zeros((ue_ref.shape[0], POOL_PAD, C_POOL), F32)
    ue_ref[:, POOL_PAD:, :] = u_ref[...]
    g_elems = ue_ref.shape[0]
    n_chunks = t_rows // chunk
    for ci in range(n_chunks):
        r0 = ci * chunk
        pos = pos0 + r0 + lax.broadcasted_iota(I32, (chunk, POOL_GC), 0)
        for g, w in enumerate(POOL_WINDOWS):
            lanes = pl.ds(g * POOL_GC, POOL_GC)
            cnt = jnp.minimum(pos + 1, w).astype(F32)
            pooled = []
            for e in range(g_elems):
                cur = ue_ref[e, pl.ds(POOL_PAD + r0, chunk), lanes]
                acc = cur
                for j in range(1, w):
                    acc = acc + ue_ref[e, pl.ds(POOL_PAD + r0 - j, chunk), lanes]
                pooled.append(acc / cnt - cur)
            pooled = pooled[0] if g_elems == 1 else jnp.concatenate(pooled, axis=0)
            y = _dot(pooled.astype(BF16), pw_ref[g].astype(BF16))
            y = y * ps_ref[:, lanes]
            for e in range(g_elems):
                o_ref[e, pl.ds(r0, chunk), lanes] = y[e * chunk:(e + 1) * chunk].astype(o_ref.dtype)


def _pool_mix(u, hist, pos0, pool_w_l, pool_scale_l, g_elems, chunk):
    b, t_rows, _ = u.shape
    per_elem = lambda rows: pl.BlockSpec((g_elems, rows, C_POOL), lambda i: (i, 0, 0))
    in_specs = [per_elem(t_rows)] + ([per_elem(POOL_PAD)] if hist is not None else []) + [
        pl.BlockSpec((len(POOL_WINDOWS), POOL_GC, POOL_GC), lambda i: (0, 0, 0)),
        pl.BlockSpec((1, C_POOL), lambda i: (0, 0))]
    args = (u,) + ((hist,) if hist is not None else ()) + (pool_w_l, pool_scale_l)
    return pl.pallas_call(
        functools.partial(_pool_kernel, pos0, t_rows, chunk, hist is not None),
        grid=(b // g_elems,),
        in_specs=in_specs,
        out_specs=per_elem(t_rows),
        out_shape=jax.ShapeDtypeStruct((b, t_rows, C_POOL), F32),
        scratch_shapes=[pltpu.VMEM((g_elems, POOL_PAD + t_rows, C_POOL), F32)],
        compiler_params=pltpu.CompilerParams(vmem_limit_bytes=VMEM_LIMIT),
        name="pool_mix",
    )(*args)


def _bias_of_dist(dist, rb_ref, h):
    n = jnp.maximum(dist, 0)
    max_exact = N_BUCKETS // 2
    large = max_exact + (jnp.log(jnp.maximum(n, 1).astype(F32) / max_exact)
                         / math.log(MAX_DISTANCE / max_exact) * (N_BUCKETS - max_exact)).astype(I32)
    large = jnp.minimum(large, N_BUCKETS - 1)
    bucket = jnp.where(n < max_exact, n, large)
    out = jnp.zeros(dist.shape, F32)
    for b in range(N_BUCKETS):
        out = jnp.where(bucket == b, rb_ref[b, h], out)
    return out


def _bias_kernel(blk, rb_ref, tb_ref, s_far_ref, s_last_ref, s_new_ref):
    key = lax.broadcasted_iota(I32, (blk, blk), 0)
    qry = lax.broadcasted_iota(I32, (blk, blk), 1)
    t = lax.broadcasted_iota(I32, (SUBLANES, LANES), 0)
    jj = lax.broadcasted_iota(I32, (SUBLANES, LANES), 1)
    for h in range(N_HEADS):
        tb_ref[0, h] = _bias_of_dist(jnp.full((blk, blk), 2 * blk, I32), rb_ref, h)
        tb_ref[1, h] = _bias_of_dist(blk + qry - key, rb_ref, h)
        tb_ref[2, h] = _bias_of_dist(qry - key, rb_ref, h)
        rows = pl.ds(h * SUBLANES, SUBLANES)
        s_far_ref[rows, :] = _bias_of_dist(jnp.full((SUBLANES, LANES), 2 * PAGE_SIZE, I32), rb_ref, h)
        s_last_ref[rows, :] = _bias_of_dist(PAGE_SIZE + t - jj, rb_ref, h)
        s_new_ref[rows, :] = _bias_of_dist(t - jj, rb_ref, h)


def _bias_tables(rel_bias, blk):
    sm = jax.ShapeDtypeStruct((N_HEADS * SUBLANES, LANES), F32)
    return pl.pallas_call(
        functools.partial(_bias_kernel, blk),
        in_specs=[pl.BlockSpec(memory_space=pltpu.SMEM)],
        out_shape=[jax.ShapeDtypeStruct((3, N_HEADS, blk, blk), F32), sm, sm, sm],
        compiler_params=pltpu.CompilerParams(vmem_limit_bytes=VMEM_LIMIT),
        name="bias_tables",
    )(rel_bias)


def _kth_start(shape):
    return jnp.full(shape, INT_MIN, I32), jnp.zeros(shape, F32)


def _kth_rounds(count_ge_key, k, carry, first_round, last_round, digit_bits=1):
    def body(r, carry):
        tkey, nge = carry
        shift = 32 - digit_bits * (r + 1)
        best_key, best_cnt = tkey, nge
        for j in range(1, 1 << digit_bits):
            ckey = tkey + lax.shift_left(jnp.int32(j), shift)
            cnt = count_ge_key(ckey)
            ok = cnt >= k
            best_key = jnp.where(ok, ckey, best_key)
            best_cnt = jnp.where(ok, cnt, best_cnt)
        return best_key, best_cnt

    return lax.fori_loop(first_round, last_round, body, carry)


def _kth_finish(carry):
    tkey, nge = carry
    return jnp.where(tkey == INT_MIN, NEG_INF, _key_to_float(tkey)), nge


def _kth_largest(count_ge, k, shape, digit_bits=1):
    carry = _kth_rounds(lambda ckey: count_ge(_key_to_float(ckey)), k, _kth_start(shape),
                        0, 32 // digit_bits, digit_bits)
    return _kth_finish(carry)


def _tie_limit(count_eq_below, need, shape, bits):
    def body(it, lim):
        cand = lim + lax.shift_left(jnp.int32(1), bits - 1 - it)
        return jnp.where(count_eq_below(cand) <= need, cand, lim)

    return lax.fori_loop(0, bits, body, jnp.zeros(shape, I32))


def _count_tiles(s, first_key, pred):
    row = lax.broadcasted_iota(I32, (SUBLANES, s.shape[1]), 0)
    return _tree([pred(s[r * SUBLANES:(r + 1) * SUBLANES], row + (first_key + r * SUBLANES))
                  for r in range(s.shape[0] // SUBLANES)])


def _resolve_ties(k_sel, thr, nge, count_where, idx_bits, lim_ref):
    lim_ref[...] = jnp.full(lim_ref.shape, 1 << idx_bits, I32)

    @pl.when(jnp.max(nge) > k_sel)
    def _():
        n_gt = count_where(lambda s, idx: jnp.where(s > thr, 1.0, 0.0))
        need = k_sel - n_gt
        lim_ref[...] = _tie_limit(
            lambda cand: count_where(lambda s, idx: jnp.where(s == thr, jnp.where(idx < cand, 1.0, 0.0), 0.0)),
            need, lim_ref.shape, idx_bits)


def _select_mask(s, idx, thr, lim):
    tie = jnp.where(s == thr, jnp.where(idx < lim, 0.0, NEG_INF), NEG_INF)
    m = jnp.where(s > thr, 0.0, tie)
    return jnp.where(s == NEG_INF, NEG_INF, m)


def _attn_prompt_kernel(k_sel, blk, seq,
                        qt_ref, qit_ref, wit_ref, kb_ref, kib_ref, vbt_ref, tb_ref,
                        o_ref,
                        score_ref, trunc_ref, lg_ref, mrun_ref, lsum_ref, acc_ref, thr_ref, excess_ref,
                        tkey_ref, nge_ref):
    qb = pl.program_id(1)
    n_chunks = qb + 1
    rep = (SUBLANES, blk)

    key_row = lax.broadcasted_iota(I32, (blk, blk), 0)
    qry_col = lax.broadcasted_iota(I32, (blk, blk), 1)

    def chunk_keys(c):
        return pl.ds(pl.multiple_of(c * blk, blk), blk)

    def all_sublanes(x, op):
        return jnp.broadcast_to(op(x, axis=0, keepdims=True), rep)

    w_idx = wit_ref[...] * IDX_HEADS ** -0.5 * IDX_DIM ** -0.5

    def score_chunk(c, carry):
        keys = chunk_keys(c)
        kic = kib_ref[keys, :][:, :IDX_DIM]
        s = jnp.zeros((blk, blk), F32)
        for h in range(IDX_HEADS):
            d = _dot(kic, qit_ref[h * IDX_DIM:(h + 1) * IDX_DIM, :])
            s = s + w_idx[h:h + 1, :] * jnp.maximum(d, 0.0)
        s = jnp.where(key_row + c * blk <= qry_col + qb * blk, s, NEG_INF)
        score_ref[keys, :] = s
        hi = lax.bitcast_convert_type(s, I32) & jnp.int32(-(1 << 16))
        trunc_ref[keys, :] = lax.bitcast_convert_type(hi, F32).astype(BF16)
        return carry

    lax.fori_loop(0, n_chunks, score_chunk, 0)

    def count_where(pred):
        def body(c, acc):
            return acc + _count_tiles(score_ref[chunk_keys(c), :], c * blk, pred)
        acc = lax.fori_loop(0, n_chunks, body, jnp.zeros(rep, F32))
        return all_sublanes(acc, jnp.sum)

    def count_upper_half(ckey):
        bits = ckey ^ ((ckey >> 31) & jnp.int32(0x7FFFFFFF))
        cf = lax.bitcast_convert_type(bits & jnp.int32(-(1 << 16)), F32)
        cb = jnp.concatenate([cf, cf], axis=0).astype(BF16)
        one, zero = jnp.ones((BF16_ROWS, blk), BF16), jnp.zeros((BF16_ROWS, blk), BF16)

        def body(c, acc):
            t = trunc_ref[chunk_keys(c), :]
            return acc + _tree([jnp.where(t[r * BF16_ROWS:(r + 1) * BF16_ROWS] >= cb, one, zero)
                                for r in range(blk // BF16_ROWS)])
        acc = lax.fori_loop(0, n_chunks, body, zero)
        return all_sublanes(acc.astype(F32), jnp.sum)

    carry = _kth_rounds(count_upper_half, k_sel, _kth_start(rep), 0, 16)
    def count_ge_key(ckey):
        cf = _key_to_float(ckey)
        return count_where(lambda s, idx: jnp.where(s >= cf, 1.0, 0.0))

    first_f32 = 16
    tkey_ref[...], nge_ref[...] = _kth_rounds(count_ge_key, k_sel, carry, first_f32, LATE_ROUNDS[0])
    for first, last in zip(LATE_ROUNDS[:-1], LATE_ROUNDS[1:]):
        @pl.when(jnp.max(nge_ref[...]) > k_sel)
        def _():
            tkey_ref[...], nge_ref[...] = _kth_rounds(
                count_ge_key, k_sel, (tkey_ref[...], nge_ref[...]), first, last)
    thr, nge = _kth_finish((tkey_ref[...], nge_ref[...]))
    thr_ref[...] = thr
    excess_ref[...] = nge - k_sel

    suffix_ones = jnp.where(qry_col >= key_row, 1.0, 0.0).astype(BF16)

    def mask_chunk(i, later):
        keys = chunk_keys(n_chunks - 1 - i)
        s = score_ref[keys, :]
        thr_row = thr_ref[0:1, :]
        tied = s == thr_row
        in_chunk = _dot(suffix_ones, jnp.where(tied, 1.0, 0.0).astype(BF16))
        kept = jnp.where(in_chunk + later[0:1, :] > excess_ref[0:1, :], 0.0, NEG_INF)
        m = jnp.where(s > thr_row, 0.0, jnp.where(tied, kept, NEG_INF))
        score_ref[keys, :] = jnp.where(s == NEG_INF, NEG_INF, m)
        return later + jnp.broadcast_to(in_chunk[0:1, :], rep)

    lax.fori_loop(0, n_chunks, mask_chunk, jnp.zeros(rep, F32))

    mrun_ref[...] = jnp.full(mrun_ref.shape, NEG_INF, F32)

    def logits_chunk(c, carry):
        keys = chunk_keys(c)
        madd = score_ref[keys, :]
        table = jnp.clip(c - (qb - 2), 0, 2)
        for h in range(N_HEADS):
            hs = slice(h * HEAD_DIM, (h + 1) * HEAD_DIM)
            s = _dot(kb_ref[keys, hs], qt_ref[hs, :]) + tb_ref[table, h] + madd
            lg_ref[h, keys, :] = s
            mrun_ref[h] = jnp.maximum(mrun_ref[h], _fold_tiles(s, jnp.maximum))
        return carry

    lax.fori_loop(0, n_chunks, logits_chunk, 0)

    for h in range(N_HEADS):
        mrun_ref[h] = all_sublanes(mrun_ref[h], jnp.max)
    lsum_ref[...] = jnp.zeros(lsum_ref.shape, F32)
    acc_ref[...] = jnp.zeros(acc_ref.shape, F32)

    def pv_chunk(c, carry):
        keys = chunk_keys(c)
        for h in range(N_HEADS):
            hs = slice(h * HEAD_DIM, (h + 1) * HEAD_DIM)
            p = jnp.exp(lg_ref[h, keys, :] - mrun_ref[h, 0:1, :])
            lsum_ref[h] += _fold_tiles(p)
            acc_ref[h] += _dot(vbt_ref[hs, keys], p.astype(BF16))
        return carry

    lax.fori_loop(0, n_chunks, pv_chunk, 0)

    outs =[acc_ref[h] / jnp.sum(lsum_ref[h], axis=0, keepdims=True) for h in range(N_HEADS)]
    o_ref[...] = jnp.concatenate(outs, axis=0).T.astype(o_ref.dtype)


def _attn_prompt(cfg, k_sel, qt, qit, wit, kb, kib, vbt, tb):
    blk, seq = cfg.blk, cfg.seq
    nq = seq // blk
    qspec = lambda width: pl.BlockSpec((None, width, blk), lambda b, i: (b, 0, i))
    kspec = lambda width: pl.BlockSpec((seq, width), lambda b, i: (b, 0))
    return pl.pallas_call(
        functools.partial(_attn_prompt_kernel, k_sel, blk, seq),
        grid=(cfg.batch, nq),
        in_specs=[qspec(D_ATT), qspec(IDX_HEADS * IDX_DIM), qspec(IDX_HEADS),
                  kspec(D_ATT), kspec(TAIL), pl.BlockSpec((None, D_ATT, seq), lambda b, i: (b, 0, 0)),
                  pl.BlockSpec((3, N_HEADS, blk, blk), lambda b, i: (0, 0, 0, 0), pipeline_mode=pl.Buffered(1))],
        out_specs=pl.BlockSpec((blk, D_ATT), lambda b, i: (b * nq + i, 0)),
        out_shape=jax.ShapeDtypeStruct((cfg.batch * seq, D_ATT), BF16),
        scratch_shapes=[pltpu.VMEM((seq, blk), F32),
                        pltpu.VMEM((seq, blk), BF16),
                        pltpu.VMEM((N_HEADS, seq, blk), F32),
                        pltpu.VMEM((N_HEADS, SUBLANES, blk), F32),
                        pltpu.VMEM((N_HEADS, SUBLANES, blk), F32),
                        pltpu.VMEM((N_HEADS, HEAD_DIM, blk), F32),
                        pltpu.VMEM((SUBLANES, blk), F32),
                        pltpu.VMEM((SUBLANES, blk), F32),
                        pltpu.VMEM((SUBLANES, blk), I32),
                        pltpu.VMEM((SUBLANES, blk), F32)],
        compiler_params=pltpu.CompilerParams(vmem_limit_bytes=VMEM_LIMIT),
        name="attn_prompt",
    )(qt, qit, wit, kb, kib, vbt, tb)


def _pad_rows(x, rows):
    return jnp.concatenate([x, jnp.zeros((rows - x.shape[0], x.shape[1]), x.dtype)], axis=0)


def _fetch_pages(pt_ref, layer, n_steps, n_pages, caches):
    n_slots = caches[0][1].shape[0]
    ahead = n_slots - 1
    i = pl.program_id(0)
    slot = i % n_slots

    def copy(cache, b, s, p):
        hbm, buf, sem = cache
        return pltpu.make_async_copy(hbm.at[layer, pt_ref[b, p]], buf.at[s, p], sem.at[s])

    def start(b, s):
        for p in range(n_pages):
            for cache in caches:
                copy(cache, b, s, p).start()

    @pl.when(i == 0)
    def _():
        for b in range(min(ahead, n_steps)):
            start(b, b)

    @pl.when(i + ahead < n_steps)
    def _():
        start(i + ahead, (i + ahead) % n_slots)

    for p in range(n_pages):
        for cache in caches:
            copy(cache, i, slot, p).wait()
    return slot


def _page_scratch(n_slots, n_pages, rows):
    return [pltpu.VMEM((n_slots, n_pages, rows, PAGE_SIZE), F32), pltpu.SemaphoreType.DMA((n_slots,))]


def _score_sample_kernel(layer, n_steps, n_pages,
                         pt_ref, qi_ref, wrep_ref, tailn_ref, ki_hbm, o_ref, ki_buf, ki_sem):
    slot = _fetch_pages(pt_ref, layer, n_steps, n_pages, [(ki_hbm, ki_buf, ki_sem)])
    shape = (SUBLANES, LANES)
    qi = qi_ref[...]
    w = wrep_ref[...]

    def score_block(dots):
        return _fold_tiles(w * jnp.maximum(dots * IDX_DIM ** -0.5, 0.0))

    for p in range(n_pages):
        o_ref[:, p * PAGE_SIZE:(p + 1) * PAGE_SIZE] = score_block(_dot(qi, ki_buf[slot, p].astype(BF16)))
    ki_new = _pad_rows(tailn_ref[...][:, :IDX_DIM], PAGE_SIZE).astype(BF16)
    causal = lax.broadcasted_iota(I32, shape, 1) <= lax.broadcasted_iota(I32, shape, 0)
    o_ref[:, n_pages * PAGE_SIZE:] = jnp.where(causal, score_block(_nt_dot(qi, ki_new)), NEG_INF)


def _score_sample(cfg, layer, page_table, qi_ht, wrep, tail_new, cache_ki):
    b, t_rows = cfg.dec_batch, cfg.dec_seq
    n_pages = cfg.past_len // PAGE_SIZE
    rows = N_HEADS * t_rows
    per_b = lambda r, w: pl.BlockSpec((None, r, w), lambda i, pt: (i, 0, 0))
    n_keys = (n_pages + 1) * PAGE_SIZE
    return pl.pallas_call(
        functools.partial(_score_sample_kernel, layer, b, n_pages),
        grid_spec=pltpu.PrefetchScalarGridSpec(
            num_scalar_prefetch=1, grid=(b,),
            in_specs=[per_b(rows, IDX_DIM), per_b(rows, LANES), per_b(t_rows, TAIL),
                      pl.BlockSpec(memory_space=pl.ANY)],
            out_specs=per_b(t_rows, n_keys),
            scratch_shapes=_page_scratch(SCORE_PAGE_SLOTS, n_pages, IDX_DIM)),
        out_shape=jax.ShapeDtypeStruct((b, t_rows, n_keys), F32),
        name="score_sample",
    )(page_table, qi_ht, wrep, tail_new, cache_ki)


def _threshold_kernel(k_sel, sc_ref, thr_ref, lim_ref, s_ref):
    n_keys, cols = s_ref.shape
    s_ref[...] = sc_ref[...].reshape(cols, n_keys).T
    rep = (SUBLANES, cols)
    idx_bits = (n_keys - 1).bit_length() + 1

    def count_where(pred):
        acc = _tree([_count_tiles(s_ref[c * LANES:(c + 1) * LANES, :], c * LANES, pred)
                     for c in range(n_keys // LANES)])
        return jnp.broadcast_to(jnp.sum(acc, axis=0, keepdims=True), rep)

    thr, nge = _kth_largest(lambda cf: count_where(lambda s, idx: jnp.where(s >= cf, 1.0, 0.0)), k_sel, rep)
    thr_ref[...] = thr
    _resolve_ties(k_sel, thr, nge, count_where, idx_bits, lim_ref)


def _threshold_cols(k_sel, scores, cols):
    b, t_rows, n_keys = scores.shape
    n_q = b * t_rows
    spec = pl.BlockSpec((SUBLANES, cols), lambda i: (0, i))
    return pl.pallas_call(
        functools.partial(_threshold_kernel, k_sel),
        grid=(n_q // cols,),
        in_specs=[pl.BlockSpec((cols // t_rows, t_rows, n_keys), lambda i: (i, 0, 0))],
        out_specs=[spec, spec],
        out_shape=[jax.ShapeDtypeStruct((SUBLANES, n_q), F32), jax.ShapeDtypeStruct((SUBLANES, n_q), I32)],
        scratch_shapes=[pltpu.VMEM((n_keys, cols), F32)],
        name="threshold_cols",
    )(scores)


def _attn_sample_kernel(layer, n_steps, n_pages, pt_ref,
                        sc_ref, thr_ref, lim_ref, q_ref, kn_ref, vn_ref, bfar_ref, blast_ref, bnew_ref,
                        k_hbm, v_hbm, o_ref, k_buf, k_sem, v_buf, v_sem):
    slot = _fetch_pages(pt_ref, layer, n_steps, n_pages, [(k_hbm, k_buf, k_sem), (v_hbm, v_buf, v_sem)])
    t_rows = SUBLANES
    n_blocks = n_pages + 1
    lane = lax.broadcasted_iota(I32, (t_rows, LANES), 1)
    thr, lim = thr_ref[...], lim_ref[...]
    masks = [_select_mask(sc_ref[:, b * PAGE_SIZE:(b + 1) * PAGE_SIZE], lane + b * PAGE_SIZE, thr, lim)
             for b in range(n_blocks)]

    def pad_new(x):
        return _pad_rows(x, PAGE_SIZE)

    rows = N_HEADS * t_rows
    same_head = (lax.broadcasted_iota(I32, (rows, D_ATT), 0) // t_rows
                 == lax.broadcasted_iota(I32, (rows, D_ATT), 1) // HEAD_DIM)
    q_bd = jnp.where(same_head, jnp.concatenate([q_ref[...]] * N_HEADS, axis=0), 0.0).astype(BF16)

    def tile_heads(m):
        return jnp.concatenate([m] * N_HEADS, axis=0)

    logits = []
    for p in range(n_pages):
        bias = blast_ref[...] if p == n_pages - 1 else bfar_ref[...]
        logits.append(_dot(q_bd, k_buf[slot, p].astype(BF16)) + bias + tile_heads(masks[p]))
    logits.append(_nt_dot(q_bd, pad_new(kn_ref[...]).astype(BF16)) + bnew_ref[...] + tile_heads(masks[n_pages]))

    m = logits[0]
    for lg in logits[1:]:
        m = jnp.maximum(m, lg)
    m = _lane_rep(jnp.max(m, axis=1, keepdims=True))
    lsum = jnp.zeros((rows, LANES), F32)
    acc = jnp.zeros((rows, D_ATT), F32)
    for b in range(n_blocks):
        p = jnp.exp(logits[b] - m)
        lsum = lsum + p
        if b < n_pages:
            acc = acc + _nt_dot(p.astype(BF16), v_buf[slot, b].astype(BF16))
        else:
            acc = acc + _dot(p.astype(BF16), pad_new(vn_ref[...]).astype(BF16))
    out = jnp.where(same_head, acc / jnp.sum(lsum, axis=1, keepdims=True), 0.0)
    o_ref[...] = _fold_tiles(out)


def _attn_sample(cfg, layer, page_table, scores, thr_rep, lim_rep, q, k_new, v_new,
                 s_far, s_last, s_new, cache_k, cache_v):
    b = cfg.dec_batch
    t_rows = cfg.dec_seq
    n_pages = cfg.past_len // PAGE_SIZE
    rows = N_HEADS * t_rows
    per_b = lambda r, w: pl.BlockSpec((None, r, w), lambda i, pt: (i, 0, 0))
    const = lambda r, w: pl.BlockSpec((r, w), lambda i, pt: (0, 0))
    in_hbm = pl.BlockSpec(memory_space=pl.ANY)
    in_specs = [per_b(t_rows, scores.shape[2]), per_b(t_rows, LANES), per_b(t_rows, LANES),
                per_b(t_rows, D_ATT), per_b(t_rows, D_ATT), per_b(t_rows, D_ATT),
                const(rows, LANES), const(rows, LANES), const(rows, LANES), in_hbm, in_hbm]
    return pl.pallas_call(
        functools.partial(_attn_sample_kernel, layer, b, n_pages),
        grid_spec=pltpu.PrefetchScalarGridSpec(
            num_scalar_prefetch=1, grid=(b,), in_specs=in_specs, out_specs=per_b(t_rows, D_ATT),
            scratch_shapes=(_page_scratch(ATTN_PAGE_SLOTS, n_pages, D_ATT)
                            + _page_scratch(ATTN_PAGE_SLOTS, n_pages, D_ATT))),
        out_shape=jax.ShapeDtypeStruct((b, t_rows, D_ATT), F32),
        compiler_params=pltpu.CompilerParams(vmem_limit_bytes=VMEM_LIMIT),
        name="attn_sample",
    )(page_table, scores, thr_rep, lim_rep, q, k_new, v_new, s_far, s_last, s_new, cache_k, cache_v)


def _forward(cfg, x_prompt, x_sample, cache_k, cache_v, cache_kidx, state_pool, page_table, c_prompt,
             c_sample, rel_bias, ada_w, ada_b, ln1_pre, ln1_post, ln2_pre, ln2_post, w_in, pool_w,
             pool_scale, w_out, w_ff1, w_ff2):
    depth = ada_w.shape[0]
    bp, seq, bs, ts = cfg.batch, cfg.seq, cfg.dec_batch, cfg.dec_seq
    assert ts == SUBLANES and seq % cfg.blk == 0 and cfg.past_len % PAGE_SIZE == 0
    assert cfg.blk >= PAGE_SIZE and seq >= TOPK_MAX and seq // BF16_ROWS <= 256
    k_prompt = min(TOPK_MAX, seq // 4)
    k_sample = min(TOPK_MAX, (cfg.past_len + ts) // 4)
    rp, rs = bp * seq, bs * ts
    tm_s = min(IN_SUB_ROWS, rs)

    mod = _ada_mod(jnp.concatenate([c_prompt, c_sample], axis=0), ada_w, ada_b)
    tb, s_far, s_last, s_new = _bias_tables(rel_bias, cfg.blk)

    n_phys = cache_k.shape[1]
    cache_kt = cache_k.transpose(0, 1, 3, 4, 2).reshape(depth, n_phys, D_ATT, PAGE_SIZE)
    cache_vt = cache_v.transpose(0, 1, 3, 4, 2).reshape(depth, n_phys, D_ATT, PAGE_SIZE)
    cache_kit = cache_kidx.transpose(0, 1, 3, 2)

    xp = x_prompt.reshape(rp, D_MODEL)
    xs = x_sample.reshape(rs, D_MODEL)
    state_p = ()
    pool_p = []
    outs_s = [[], [], [], []]
    w_in_b = jnp.pad(w_in, ((0, 0), (0, 0), (0, D_IN_PAD - D_IN))).astype(BF16)
    wo_b, w1_b, w2_b = w_out.astype(BF16), w_ff1.astype(BF16), w_ff2.astype(BF16)
    for l in range(depth):
        lnrow = lambda a: a[l].reshape(1, D_MODEL)
        mod_p = mod[l, :bp].reshape(bp, 1, 6 * D_MODEL)
        mod_s = jnp.repeat(mod[l, bp:], ts, axis=0)
        psc = pool_scale[l].reshape(1, C_POOL)

        tm_in = min(seq, cfg.tm if state_p else max(IN_SUB_ROWS, cfg.tm // depth))
        u, kb, tailb, *state_p, wit, qt, qit, vbt = _in_proj(
            xp, mod_p, seq, lnrow(ln1_pre), w_in_b, tm_in, layer=l, depth=depth, carried=tuple(state_p))
        u3 = u.reshape(bp, seq, C_POOL)
        y_pool = _pool_mix(u3, None, 0, pool_w[l], psc, 1, min(256, seq)).reshape(rp, C_POOL)
        y_att = _attn_prompt(cfg, k_prompt, qt, qit, wit, kb, tailb, vbt, tb)
        xp = _out_ffn(xp, y_pool, y_att, mod_p, seq, lnrow(ln1_post), lnrow(ln2_pre), lnrow(ln2_post),
                      wo_b, w1_b, w2_b, l, min(cfg.tm_out, seq))
        pool_p.append(u3[:, seq - POOL_HIST:])

        u, q, qi, tail, k, v = _in_proj(xs, mod_s, 0, lnrow(ln1_pre), w_in_b, tm_s, layer=l)
        u3 = u.reshape(bs, ts, C_POOL)
        hist = jnp.concatenate([jnp.zeros((bs, POOL_PAD - POOL_HIST, C_POOL), F32), state_pool[l]], axis=1)
        y_pool = _pool_mix(u3, hist, cfg.past_len, pool_w[l], psc, min(16, bs), ts).reshape(rs, C_POOL)
        qi_ht = qi.reshape(bs, ts, IDX_HEADS, IDX_DIM).transpose(0, 2, 1, 3).reshape(bs, IDX_HEADS * ts, IDX_DIM)
        wi = tail[:, OFF_WI - OFF_KI:OFF_WI - OFF_KI + IDX_HEADS].reshape(bs, ts, IDX_HEADS) * IDX_HEADS ** -0.5
        wrep = jnp.broadcast_to(wi.transpose(0, 2, 1).reshape(bs, IDX_HEADS * ts, 1), (bs, IDX_HEADS * ts, LANES))
        scores = _score_sample(cfg, l, page_table, qi_ht, wrep, tail.reshape(bs, ts, TAIL), cache_kit)
        thr, lim = _threshold_cols(k_sample, scores, min(256, rs))
        per_query = lambda a: jnp.broadcast_to(a[0].reshape(bs, ts, 1), (bs, ts, LANES))
        y_att = _attn_sample(cfg, l, page_table, scores, per_query(thr), per_query(lim),
                             q.astype(F32).reshape(bs, ts, D_ATT),
                             k.reshape(bs, ts, D_ATT), v.reshape(bs, ts, D_ATT),
                             s_far, s_last, s_new, cache_kt, cache_vt).reshape(rs, D_ATT)
        xs = _out_ffn(xs, y_pool, y_att, mod_s, 0, lnrow(ln1_post), lnrow(ln2_pre), lnrow(ln2_post),
                      wo_b, w1_b, w2_b, l, min(OUT_SUB_ROWS, rs))
        for lst, a in zip(outs_s, (k.reshape(bs, ts, N_HEADS, HEAD_DIM), v.reshape(bs, ts, N_HEADS, HEAD_DIM),
                                   tail[:, :IDX_DIM].reshape(bs, ts, IDX_DIM),
                                   jnp.concatenate([hist, u3], axis=1)[:, -POOL_HIST:])):
            lst.append(a)

    kt, vt, kit = state_p
    heads_last = lambda a: a.reshape(depth, bp, N_HEADS, HEAD_DIM, seq).transpose(0, 1, 4, 2, 3)
    return (xp.reshape(bp, seq, D_MODEL), xs.reshape(bs, ts, D_MODEL),
            heads_last(kt), heads_last(vt), kit.transpose(0, 1, 3, 2), jnp.stack(pool_p),
            *[jnp.stack(a) for a in outs_s])


def kernel(x_prompt, x_sample, cache_k, cache_v, cache_kidx, state_pool, page_table, c_prompt, c_sample,
           rel_bias, ada_w, ada_b, ln1_pre, ln1_post, ln2_pre, ln2_post, w_in, pool_w, pool_scale,
           w_out, w_ff1, w_ff2):
    cfg = Cfg(batch=x_prompt.shape[0], seq=x_prompt.shape[1], dec_batch=x_sample.shape[0],
              dec_seq=x_sample.shape[1], past_len=page_table.shape[1] * PAGE_SIZE, blk=256, tm=1024,
              tm_out=1024)
    return _forward(cfg, x_prompt, x_sample, cache_k, cache_v, cache_kidx, state_pool, page_table, c_prompt,
                    c_sample, rel_bias, ada_w, ada_b, ln1_pre, ln1_post, ln2_pre, ln2_post, w_in, pool_w,
                    pool_scale, w_out, w_ff1, w_ff2)
```

```python
import functools
import math
from typing import NamedTuple

import jax
import jax.numpy as jnp
from jax import lax
from jax.experimental import pallas as pl
from jax.experimental.pallas import tpu as pltpu

F32 = jnp.float32
BF16 = jnp.bfloat16
I32 = jnp.int32

D_MODEL = 1024
C_POOL = 512
D_ATT = 512
HEAD_DIM = 64
N_HEADS = 8
IDX_HEADS = 8
IDX_DIM = 64
POOL_WINDOWS = (2, 4, 8, 16)
POOL_GC = C_POOL // len(POOL_WINDOWS)
POOL_HIST = max(POOL_WINDOWS) - 1
TOPK_MAX = 256
N_BUCKETS = 32
MAX_DISTANCE = 128
D_FF = 4 * D_MODEL
EPS = 1e-6
PAGE_SIZE = 128
OFF_Q = C_POOL
OFF_K = OFF_Q + D_ATT
OFF_V = OFF_K + D_ATT
OFF_QI = OFF_V + D_ATT
OFF_KI = OFF_QI + IDX_HEADS * IDX_DIM
OFF_WI = OFF_KI + IDX_DIM
D_IN = OFF_WI + IDX_HEADS

LANES = 128
SUBLANES = 8
BF16_ROWS = 2 * SUBLANES
D_IN_PAD = ((D_IN + LANES - 1) // LANES) * LANES
TAIL = D_IN_PAD - OFF_KI
POOL_PAD = 16
OUT_SUB_ROWS = 256
IN_SUB_ROWS = 256
SCORE_PAGE_SLOTS = 4
ATTN_PAGE_SLOTS = 3
LATE_ROUNDS = (24, 26, 28, 32)
VMEM_LIMIT = 56 * 1024 * 1024

NEG_INF = float("-inf")
INT_MIN = -(2 ** 31)


class Cfg(NamedTuple):
    batch: int
    seq: int
    dec_batch: int
    dec_seq: int
    past_len: int
    blk: int
    tm: int
    tm_out: int


def _rms(x, g):
    ms = jnp.mean(x * x, axis=-1, keepdims=True)
    return x * lax.rsqrt(ms + EPS) * g


def _nt_dot(a, b):
    return lax.dot_general(a, b, (((1,), (1,)), ((), ())), preferred_element_type=F32)


def _dot(a, b):
    return jnp.dot(a, b, preferred_element_type=F32)


def _key_to_float(key):
    bits = key ^ ((key >> 31) & jnp.int32(0x7FFFFFFF))
    return lax.bitcast_convert_type(bits, F32)


def _lane_rep(col, width=LANES):
    return jnp.broadcast_to(col, (col.shape[0], width))


def _fold_tiles(x, op=jnp.add):
    return _tree([x[r * SUBLANES:(r + 1) * SUBLANES] for r in range(x.shape[0] // SUBLANES)], op)


def _tree(parts, op=jnp.add):
    while len(parts) > 1:
        parts = [op(parts[i], parts[i + 1]) if i + 1 < len(parts) else parts[i] for i in range(0, len(parts), 2)]
    return parts[0]


def _ada_kernel(c_ref, w_ref, b_ref, o_ref):
    o_ref[...] = _dot(c_ref[...].astype(BF16), w_ref[...].astype(BF16)) + b_ref[...]


def _ada_mod(c_all, ada_w, ada_b):
    depth, d, n = ada_w.shape
    rc = c_all.shape[0]
    tn = 1536
    return pl.pallas_call(
        _ada_kernel,
        grid=(depth, n // tn),
        in_specs=[pl.BlockSpec((rc, d), lambda l, j: (0, 0)),
                  pl.BlockSpec((None, d, tn), lambda l, j: (l, 0, j)),
                  pl.BlockSpec((None, 1, tn), lambda l, j: (l, 0, j))],
        out_specs=pl.BlockSpec((None, rc, tn), lambda l, j: (l, 0, j)),
        out_shape=jax.ShapeDtypeStruct((depth, rc, n), F32),
        name="ada_mod",
    )(c_all, ada_w, ada_b.reshape(depth, 1, n))


def _mod_spec(per_batch_rows, tm, chunk):
    if per_batch_rows:
        tiles = per_batch_rows // tm
        return pl.BlockSpec((None, 1, D_MODEL), lambda i: (i // tiles, 0, chunk))
    return pl.BlockSpec((tm, D_MODEL), lambda i: (i, chunk))


def _in_kernel(transposed, n_carried, n_sub, layer, x_ref, sh_ref, sc_ref, ln_ref, w_ref, *refs):
    u_ref, *refs = refs[n_carried:]
    sub = x_ref.shape[0] // n_sub
    mod = lambda ref, rows: ref[...] if ref.shape[0] == 1 else ref[rows, :]

    def put_state(ref, cols, val):
        if len(ref.shape) == 2:
            ref[:, cols] = val
        else:
            for l in range(ref.shape[0]):
                ref[l, :, cols] = val if l == layer else jnp.zeros(val.shape, val.dtype)

    def project(i):
        rows = pl.ds(i * sub, sub)
        h = _rms(x_ref[rows, :], ln_ref[...]) * (1.0 + mod(sc_ref, rows)) + mod(sh_ref, rows)
        return _dot(h.astype(BF16), w_ref[...])

    def emit(i, z):
        rows = pl.ds(i * sub, sub)
        u_ref[rows, :] = z[:, :OFF_Q]
        q = z[:, OFF_Q:OFF_K] * HEAD_DIM ** -0.5
        k = z[:, OFF_K:OFF_V]
        v = z[:, OFF_V:OFF_QI]
        qi = z[:, OFF_QI:OFF_KI]
        tail = z[:, OFF_KI:]
        if transposed:
            kb_ref, tailb_ref, kt_ref, vt_ref, kit_ref, wit_ref, qt_ref, qit_ref, vbt_ref = refs
            cols = pl.ds(i * sub, sub)
            kb_ref[rows, :] = k.astype(BF16)
            tailb_ref[rows, :] = tail.astype(BF16)
            put_state(kt_ref, cols, k.T)
            vt = v.T
            put_state(vt_ref, cols, vt)
            vbt_ref[:, cols] = vt.astype(BF16)
            tail_t = tail.T
            put_state(kit_ref, cols, tail_t[:IDX_DIM])
            wit_ref[:, cols] = tail_t[OFF_WI - OFF_KI:OFF_WI - OFF_KI + IDX_HEADS]
            qt_ref[:, cols] = q.T.astype(BF16)
            qit_ref[:, cols] = qi.T.astype(BF16)
        else:
            q_ref, qi_ref, tail_ref, k_ref, v_ref = refs
            q_ref[rows, :] = q.astype(BF16)
            qi_ref[rows, :] = qi.astype(BF16)
            tail_ref[rows, :] = tail
            k_ref[rows, :] = k
            v_ref[rows, :] = v

    z = project(0)
    for i in range(n_sub):
        z_next = project(i + 1) if i + 1 < n_sub else None
        emit(i, z)
        z = z_next


def _in_proj(x, mod, per_batch_rows, ln, w_in_b, tm, layer=0, depth=1, carried=()):
    r = x.shape[0]
    row = lambda width: pl.BlockSpec((tm, width), lambda i: (i, 0))
    const = lambda shape: pl.BlockSpec(shape, lambda i: (0, 0))
    in_specs = [row(D_MODEL), _mod_spec(per_batch_rows, tm, 0), _mod_spec(per_batch_rows, tm, 1),
                const((1, D_MODEL)), pl.BlockSpec((None, D_MODEL, D_IN_PAD), lambda i: (layer, 0, 0))]
    aliases = {}
    if per_batch_rows:
        tiles = per_batch_rows // tm
        nb = r // per_batch_rows
        rows_out = [(C_POOL, F32), (D_ATT, BF16), (TAIL, BF16)]
        state_out = [D_ATT, D_ATT, IDX_DIM]
        cols_out = [(IDX_HEADS, F32), (D_ATT, BF16), (IDX_HEADS * IDX_DIM, BF16), (D_ATT, BF16)]
    else:
        rows_out = [(C_POOL, F32), (D_ATT, BF16), (IDX_HEADS * IDX_DIM, BF16), (TAIL, F32), (D_ATT, F32), (D_ATT, F32)]
        state_out, cols_out = [], []
    out_specs = [row(w) for w, _ in rows_out]
    out_shape = [jax.ShapeDtypeStruct((r, w), dt) for w, dt in rows_out]
    for n, width in enumerate(state_out):
        if carried:
            aliases[len(in_specs)] = len(out_specs)
            in_specs.append(pl.BlockSpec(memory_space=pl.ANY))
            out_specs.append(pl.BlockSpec((None, None, width, tm), lambda i: (layer, i // tiles, 0, i % tiles)))
        else:
            out_specs.append(pl.BlockSpec((depth, None, width, tm), lambda i: (0, i // tiles, 0, i % tiles)))
        out_shape.append(jax.ShapeDtypeStruct((depth, nb, width, per_batch_rows), F32))
    for width, dt in cols_out:
        out_specs.append(pl.BlockSpec((None, width, tm), lambda i: (i // tiles, 0, i % tiles)))
        out_shape.append(jax.ShapeDtypeStruct((nb, width, per_batch_rows), dt))
    return pl.pallas_call(
        functools.partial(_in_kernel, bool(per_batch_rows), len(carried), max(1, tm // IN_SUB_ROWS), layer),
        grid=(r // tm,),
        in_specs=in_specs,
        out_specs=out_specs,
        out_shape=out_shape,
        input_output_aliases=aliases,
        compiler_params=pltpu.CompilerParams(vmem_limit_bytes=VMEM_LIMIT),
        name="in_proj",
    )(x, mod, mod, ln, w_in_b, *carried)


def _out_kernel(n_sub, x_ref, yp_ref, ya_ref, g1_ref, sh2_ref, sc2_ref, g2_ref,
                ln1_ref, ln2a_ref, ln2b_ref, wo_ref, w1_ref, w2_ref, o_ref):
    sub = x_ref.shape[0] // n_sub
    rows = [pl.ds(i * sub, sub) for i in range(n_sub)]
    mod = lambda ref, i: ref[...] if ref.shape[0] == 1 else ref[rows[i], :]

    def out_proj(i):
        return (_dot(yp_ref[rows[i], :].astype(BF16), wo_ref[:C_POOL, :])
                + _dot(ya_ref[rows[i], :].astype(BF16), wo_ref[C_POOL:, :]))

    def residual_and_prenorm(i, mixed):
        x1 = x_ref[rows[i], :] + mod(g1_ref, i) * _rms(mixed, ln1_ref[...])
        h = (_rms(x1, ln2a_ref[...]) * (1.0 + mod(sc2_ref, i)) + mod(sh2_ref, i)).astype(BF16)
        return x1, h

    def mlp_chunk(h, f, c):
        sl = slice(c * D_MODEL, (c + 1) * D_MODEL)
        a = jnp.maximum(_dot(h, w1_ref[:, sl]), 0.0)
        return f + _dot((a * a).astype(BF16), w2_ref[sl, :])

    def finish(i, x1, f):
        o_ref[rows[i], :] = x1 + mod(g2_ref, i) * _rms(f, ln2b_ref[...])

    n_chunks = D_FF // D_MODEL
    mixed = [out_proj(i) for i in range(n_sub)]
    x1, h = residual_and_prenorm(0, mixed[0])
    for i in range(n_sub):
        f = mlp_chunk(h, jnp.zeros(x1.shape, F32), 0)
        if i + 1 < n_sub:
            nxt = residual_and_prenorm(i + 1, mixed[i + 1])
        if i > 0:
            finish(i - 1, *done)
        for c in range(1, n_chunks):
            f = mlp_chunk(h, f, c)
        done = (x1, f)
        if i + 1 < n_sub:
            x1, h = nxt
    finish(n_sub - 1, *done)


def _out_ffn(x, yp, ya, mod, per_batch_rows, ln1_post, ln2_pre, ln2_post, wo_b, w1_b, w2_b, layer, tm):
    r = x.shape[0]
    row = lambda width: pl.BlockSpec((tm, width), lambda i: (i, 0))
    const = lambda shape: pl.BlockSpec(shape, lambda i: (0, 0), pipeline_mode=pl.Buffered(1))
    weight = lambda shape: pl.BlockSpec((None,) + shape, lambda i: (layer, 0, 0), pipeline_mode=pl.Buffered(1))
    ms = lambda chunk: _mod_spec(per_batch_rows, tm, chunk)
    return pl.pallas_call(
        functools.partial(_out_kernel, max(1, tm // OUT_SUB_ROWS)),
        grid=(r // tm,),
        in_specs=[row(D_MODEL), row(C_POOL), row(D_ATT), ms(2), ms(3), ms(4), ms(5),
                  const((1, D_MODEL)), const((1, D_MODEL)), const((1, D_MODEL)),
                  weight((D_MODEL, D_MODEL)), weight((D_MODEL, D_FF)), weight((D_FF, D_MODEL))],
        out_specs=row(D_MODEL),
        out_shape=jax.ShapeDtypeStruct((r, D_MODEL), F32),
        compiler_params=pltpu.CompilerParams(vmem_limit_bytes=VMEM_LIMIT),
        name="out_ffn",
    )(x, yp, ya, mod, mod, mod, mod, ln1_post, ln2_pre, ln2_post, wo_b, w1_b, w2_b)


def _pool_kernel(pos0, t_rows, chunk, has_hist, u_ref, *refs):
    if has_hist:
        hist_ref, pw_ref, ps_ref, o_ref, ue_ref = refs
        ue_ref[:, :POOL_PAD, :] = hist_ref[...]
    else:
        pw_ref, ps_ref, o_ref, ue_ref = refs
        ue_ref[:, :POOL_PAD, :] = jnp.zeros((ue_ref.shape[0], POOL_PAD, C_POOL), F32)
    ue_ref[:, POOL_PAD:, :] = u_ref[...]
    g_elems = ue_ref.shape[0]
    n_chunks = t_rows // chunk
    for ci in range(n_chunks):
        r0 = ci * chunk
        pos = pos0 + r0 + lax.broadcasted_iota(I32, (chunk, POOL_GC), 0)
        for g, w in enumerate(POOL_WINDOWS):
            lanes = pl.ds(g * POOL_GC, POOL_GC)
            cnt = jnp.minimum(pos + 1, w).astype(F32)
            pooled = []
            for e in range(g_elems):
                ext = ue_ref[e, pl.ds(r0, POOL_PAD + chunk), lanes]
                cur = ext[POOL_PAD:]
                acc = ext[POOL_PAD - (w - 1):]
                step = 1
                while step < w:
                    acc = acc[step:] + acc[:acc.shape[0] - step]
                    step *= 2
                pooled.append(acc / cnt - cur)
            pooled = pooled[0] if g_elems == 1 else jnp.concatenate(pooled, axis=0)
            y = _dot(pooled.astype(BF16), pw_ref[g].astype(BF16))
            y = y * ps_ref[:, lanes]
            for e in range(g_elems):
                o_ref[e, pl.ds(r0, chunk), lanes] = y[e * chunk:(e + 1) * chunk].astype(o_ref.dtype)


def _pool_mix(u, hist, pos0, pool_w_l, pool_scale_l, g_elems, chunk):
    b, t_rows, _ = u.shape
    per_elem = lambda rows: pl.BlockSpec((g_elems, rows, C_POOL), lambda i: (i, 0, 0))
    in_specs = [per_elem(t_rows)] + ([per_elem(POOL_PAD)] if hist is not None else []) + [
        pl.BlockSpec((len(POOL_WINDOWS), POOL_GC, POOL_GC), lambda i: (0, 0, 0)),
        pl.BlockSpec((1, C_POOL), lambda i: (0, 0))]
    args = (u,) + ((hist,) if hist is not None else ()) + (pool_w_l, pool_scale_l)
    return pl.pallas_call(
        functools.partial(_pool_kernel, pos0, t_rows, chunk, hist is not None),
        grid=(b // g_elems,),
        in_specs=in_specs,
        out_specs=per_elem(t_rows),
        out_shape=jax.ShapeDtypeStruct((b, t_rows, C_POOL), F32),
        scratch_shapes=[pltpu.VMEM((g_elems, POOL_PAD + t_rows, C_POOL), F32)],
        compiler_params=pltpu.CompilerParams(vmem_limit_bytes=VMEM_LIMIT),
        name="pool_mix",
    )(*args)


def _bias_of_dist(dist, rb_ref, h):
    n = jnp.maximum(dist, 0)
    max_exact = N_BUCKETS // 2
    large = max_exact + (jnp.log(jnp.maximum(n, 1).astype(F32) / max_exact)
                         / math.log(MAX_DISTANCE / max_exact) * (N_BUCKETS - max_exact)).astype(I32)
    large = jnp.minimum(large, N_BUCKETS - 1)
    bucket = jnp.where(n < max_exact, n, large)
    out = jnp.zeros(dist.shape, F32)
    for b in range(N_BUCKETS):
        out = jnp.where(bucket == b, rb_ref[b, h], out)
    return out


def _bias_kernel(blk, rb_ref, tb_ref, s_far_ref, s_last_ref, s_new_ref):
    key = lax.broadcasted_iota(I32, (blk, blk), 0)
    qry = lax.broadcasted_iota(I32, (blk, blk), 1)
    t = lax.broadcasted_iota(I32, (SUBLANES, LANES), 0)
    jj = lax.broadcasted_iota(I32, (SUBLANES, LANES), 1)
    for h in range(N_HEADS):
        tb_ref[0, h] = _bias_of_dist(jnp.full((blk, blk), 2 * blk, I32), rb_ref, h)
        tb_ref[1, h] = _bias_of_dist(blk + qry - key, rb_ref, h)
        tb_ref[2, h] = _bias_of_dist(qry - key, rb_ref, h)
        rows = pl.ds(h * SUBLANES, SUBLANES)
        s_far_ref[rows, :] = _bias_of_dist(jnp.full((SUBLANES, LANES), 2 * PAGE_SIZE, I32), rb_ref, h)
        s_last_ref[rows, :] = _bias_of_dist(PAGE_SIZE + t - jj, rb_ref, h)
        s_new_ref[rows, :] = _bias_of_dist(t - jj, rb_ref, h)


def _bias_tables(rel_bias, blk):
    sm = jax.ShapeDtypeStruct((N_HEADS * SUBLANES, LANES), F32)
    return pl.pallas_call(
        functools.partial(_bias_kernel, blk),
        in_specs=[pl.BlockSpec(memory_space=pltpu.SMEM)],
        out_shape=[jax.ShapeDtypeStruct((3, N_HEADS, blk, blk), F32), sm, sm, sm],
        compiler_params=pltpu.CompilerParams(vmem_limit_bytes=VMEM_LIMIT),
        name="bias_tables",
    )(rel_bias)


def _kth_start(shape):
    return jnp.full(shape, INT_MIN, I32), jnp.zeros(shape, F32)


def _kth_rounds(count_ge_key, k, carry, first_round, last_round, digit_bits=1):
    def body(r, carry):
        tkey, nge = carry
        shift = 32 - digit_bits * (r + 1)
        best_key, best_cnt = tkey, nge
        for j in range(1, 1 << digit_bits):
            ckey = tkey + lax.shift_left(jnp.int32(j), shift)
            cnt = count_ge_key(ckey)
            ok = cnt >= k
            best_key = jnp.where(ok, ckey, best_key)
            best_cnt = jnp.where(ok, cnt, best_cnt)
        return best_key, best_cnt

    return lax.fori_loop(first_round, last_round, body, carry)


def _kth_finish(carry):
    tkey, nge = carry
    return jnp.where(tkey == INT_MIN, NEG_INF, _key_to_float(tkey)), nge


def _kth_largest(count_ge, k, shape, digit_bits=1):
    carry = _kth_rounds(lambda ckey: count_ge(_key_to_float(ckey)), k, _kth_start(shape),
                        0, 32 // digit_bits, digit_bits)
    return _kth_finish(carry)


def _tie_limit(count_eq_below, need, shape, bits):
    def body(it, lim):
        cand = lim + lax.shift_left(jnp.int32(1), bits - 1 - it)
        return jnp.where(count_eq_below(cand) <= need, cand, lim)

    return lax.fori_loop(0, bits, body, jnp.zeros(shape, I32))


def _count_tiles(s, first_key, pred):
    row = lax.broadcasted_iota(I32, (SUBLANES, s.shape[1]), 0)
    return _tree([pred(s[r * SUBLANES:(r + 1) * SUBLANES], row + (first_key + r * SUBLANES))
                  for r in range(s.shape[0] // SUBLANES)])


def _resolve_ties(k_sel, thr, nge, count_where, idx_bits, lim_ref):
    lim_ref[...] = jnp.full(lim_ref.shape, 1 << idx_bits, I32)

    @pl.when(jnp.max(nge) > k_sel)
    def _():
        n_gt = count_where(lambda s, idx: jnp.where(s > thr, 1.0, 0.0))
        need = k_sel - n_gt
        lim_ref[...] = _tie_limit(
            lambda cand: count_where(lambda s, idx: jnp.where(s == thr, jnp.where(idx < cand, 1.0, 0.0), 0.0)),
            need, lim_ref.shape, idx_bits)


def _select_mask(s, idx, thr, lim):
    tie = jnp.where(s == thr, jnp.where(idx < lim, 0.0, NEG_INF), NEG_INF)
    m = jnp.where(s > thr, 0.0, tie)
    return jnp.where(s == NEG_INF, NEG_INF, m)


def _attn_prompt_kernel(k_sel, blk, seq,
                        qt_ref, qit_ref, wit_ref, kb_ref, kib_ref, vbt_ref, tb_ref,
                        o_ref,
                        score_ref, trunc_ref, lg_ref, mrun_ref, lsum_ref, acc_ref, thr_ref, excess_ref,
                        tkey_ref, nge_ref):
    qb = pl.program_id(1)
    n_chunks = qb + 1
    rep = (SUBLANES, blk)

    key_row = lax.broadcasted_iota(I32, (blk, blk), 0)
    qry_col = lax.broadcasted_iota(I32, (blk, blk), 1)

    def chunk_keys(c):
        return pl.ds(pl.multiple_of(c * blk, blk), blk)

    def all_sublanes(x, op):
        return jnp.broadcast_to(op(x, axis=0, keepdims=True), rep)

    w_idx = wit_ref[...] * IDX_HEADS ** -0.5 * IDX_DIM ** -0.5

    def score_chunk(c, carry):
        keys = chunk_keys(c)
        kic = kib_ref[keys, :][:, :IDX_DIM]
        s = jnp.zeros((blk, blk), F32)
        for h in range(IDX_HEADS):
            d = _dot(kic, qit_ref[h * IDX_DIM:(h + 1) * IDX_DIM, :])
            s = s + w_idx[h:h + 1, :] * jnp.maximum(d, 0.0)
        s = jnp.where(key_row + c * blk <= qry_col + qb * blk, s, NEG_INF)
        score_ref[keys, :] = s
        hi = lax.bitcast_convert_type(s, I32) & jnp.int32(-(1 << 16))
        trunc_ref[keys, :] = lax.bitcast_convert_type(hi, F32).astype(BF16)
        return carry

    lax.fori_loop(0, n_chunks, score_chunk, 0)

    def count_where(pred):
        def body(c, acc):
            return acc + _count_tiles(score_ref[chunk_keys(c), :], c * blk, pred)
        acc = lax.fori_loop(0, n_chunks, body, jnp.zeros(rep, F32))
        return all_sublanes(acc, jnp.sum)

    def count_upper_half(ckey):
        bits = ckey ^ ((ckey >> 31) & jnp.int32(0x7FFFFFFF))
        cf = lax.bitcast_convert_type(bits & jnp.int32(-(1 << 16)), F32)
        cb = jnp.concatenate([cf, cf], axis=0).astype(BF16)
        one, zero = jnp.ones((BF16_ROWS, blk), BF16), jnp.zeros((BF16_ROWS, blk), BF16)

        def body(c, acc):
            t = trunc_ref[chunk_keys(c), :]
            return acc + _tree([jnp.where(t[r * BF16_ROWS:(r + 1) * BF16_ROWS] >= cb, one, zero)
                                for r in range(blk // BF16_ROWS)])
        acc = lax.fori_loop(0, n_chunks, body, zero)
        return all_sublanes(acc.astype(F32), jnp.sum)

    carry = _kth_rounds(count_upper_half, k_sel, _kth_start(rep), 0, 16)
    def count_ge_key(ckey):
        cf = _key_to_float(ckey)
        return count_where(lambda s, idx: jnp.where(s >= cf, 1.0, 0.0))

    first_f32 = 16
    tkey_ref[...], nge_ref[...] = _kth_rounds(count_ge_key, k_sel, carry, first_f32, LATE_ROUNDS[0])
    for first, last in zip(LATE_ROUNDS[:-1], LATE_ROUNDS[1:]):
        @pl.when(jnp.max(nge_ref[...]) > k_sel)
        def _():
            tkey_ref[...], nge_ref[...] = _kth_rounds(
                count_ge_key, k_sel, (tkey_ref[...], nge_ref[...]), first, last)
    thr, nge = _kth_finish((tkey_ref[...], nge_ref[...]))
    thr_ref[...] = thr
    excess_ref[...] = nge - k_sel

    suffix_ones = jnp.where(qry_col >= key_row, 1.0, 0.0).astype(BF16)

    def mask_chunk(i, later):
        keys = chunk_keys(n_chunks - 1 - i)
        s = score_ref[keys, :]
        thr_row = thr_ref[0:1, :]
        tied = s == thr_row
        in_chunk = _dot(suffix_ones, jnp.where(tied, 1.0, 0.0).astype(BF16))
        kept = jnp.where(in_chunk + later[0:1, :] > excess_ref[0:1, :], 0.0, NEG_INF)
        m = jnp.where(s > thr_row, 0.0, jnp.where(tied, kept, NEG_INF))
        score_ref[keys, :] = jnp.where(s == NEG_INF, NEG_INF, m)
        return later + jnp.broadcast_to(in_chunk[0:1, :], rep)

    lax.fori_loop(0, n_chunks, mask_chunk, jnp.zeros(rep, F32))

    mrun_ref[...] = jnp.full(mrun_ref.shape, NEG_INF, F32)

    def logits_chunk(c, carry):
        keys = chunk_keys(c)
        madd = score_ref[keys, :]
        table = jnp.clip(c - (qb - 2), 0, 2)
        for h in range(N_HEADS):
            hs = slice(h * HEAD_DIM, (h + 1) * HEAD_DIM)
            s = _dot(kb_ref[keys, hs], qt_ref[hs, :]) + tb_ref[table, h] + madd
            lg_ref[h, keys, :] = s
            mrun_ref[h] = jnp.maximum(mrun_ref[h], _fold_tiles(s, jnp.maximum))
        return carry

    lax.fori_loop(0, n_chunks, logits_chunk, 0)

    for h in range(N_HEADS):
        mrun_ref[h] = all_sublanes(mrun_ref[h], jnp.max)
    lsum_ref[...] = jnp.zeros(lsum_ref.shape, F32)
    acc_ref[...] = jnp.zeros(acc_ref.shape, F32)

    def pv_chunk(c, carry):
        keys = chunk_keys(c)
        for h in range(N_HEADS):
            hs = slice(h * HEAD_DIM, (h + 1) * HEAD_DIM)
            p = jnp.exp(lg_ref[h, keys, :] - mrun_ref[h, 0:1, :])
            lsum_ref[h] += _fold_tiles(p)
            acc_ref[h] += _dot(vbt_ref[hs, keys], p.astype(BF16))
        return carry

    lax.fori_loop(0, n_chunks, pv_chunk, 0)

    outs =[acc_ref[h] / jnp.sum(lsum_ref[h], axis=0, keepdims=True) for h in range(N_HEADS)]
    o_ref[...] = jnp.concatenate(outs, axis=0).T.astype(o_ref.dtype)


def _attn_prompt(cfg, k_sel, qt, qit, wit, kb, kib, vbt, tb):
    blk, seq = cfg.blk, cfg.seq
    nq = seq // blk
    qspec = lambda width: pl.BlockSpec((None, width, blk), lambda b, i: (b, 0, i))
    kspec = lambda width: pl.BlockSpec((seq, width), lambda b, i: (b, 0))
    return pl.pallas_call(
        functools.partial(_attn_prompt_kernel, k_sel, blk, seq),
        grid=(cfg.batch, nq),
        in_specs=[qspec(D_ATT), qspec(IDX_HEADS * IDX_DIM), qspec(IDX_HEADS),
                  kspec(D_ATT), kspec(TAIL), pl.BlockSpec((None, D_ATT, seq), lambda b, i: (b, 0, 0)),
                  pl.BlockSpec((3, N_HEADS, blk, blk), lambda b, i: (0, 0, 0, 0), pipeline_mode=pl.Buffered(1))],
        out_specs=pl.BlockSpec((blk, D_ATT), lambda b, i: (b * nq + i, 0)),
        out_shape=jax.ShapeDtypeStruct((cfg.batch * seq, D_ATT), BF16),
        scratch_shapes=[pltpu.VMEM((seq, blk), F32),
                        pltpu.VMEM((seq, blk), BF16),
                        pltpu.VMEM((N_HEADS, seq, blk), F32),
                        pltpu.VMEM((N_HEADS, SUBLANES, blk), F32),
                        pltpu.VMEM((N_HEADS, SUBLANES, blk), F32),
                        pltpu.VMEM((N_HEADS, HEAD_DIM, blk), F32),
                        pltpu.VMEM((SUBLANES, blk), F32),
                        pltpu.VMEM((SUBLANES, blk), F32),
                        pltpu.VMEM((SUBLANES, blk), I32),
                        pltpu.VMEM((SUBLANES, blk), F32)],
        compiler_params=pltpu.CompilerParams(vmem_limit_bytes=VMEM_LIMIT),
        name="attn_prompt",
    )(qt, qit, wit, kb, kib, vbt, tb)


def _pad_rows(x, rows):
    return jnp.concatenate([x, jnp.zeros((rows - x.shape[0], x.shape[1]), x.dtype)], axis=0)


def _fetch_pages(pt_ref, layer, n_steps, n_pages, caches):
    n_slots = caches[0][1].shape[0]
    ahead = n_slots - 1
    i = pl.program_id(0)
    slot = i % n_slots

    def copy(cache, b, s, p):
        hbm, buf, sem = cache
        return pltpu.make_async_copy(hbm.at[layer, pt_ref[b, p]], buf.at[s, p], sem.at[s])

    def start(b, s):
        for p in range(n_pages):
            for cache in caches:
                copy(cache, b, s, p).start()

    @pl.when(i == 0)
    def _():
        for b in range(min(ahead, n_steps)):
            start(b, b)

    @pl.when(i + ahead < n_steps)
    def _():
        start(i + ahead, (i + ahead) % n_slots)

    for p in range(n_pages):
        for cache in caches:
            copy(cache, i, slot, p).wait()
    return slot


def _page_scratch(n_slots, n_pages, rows):
    return [pltpu.VMEM((n_slots, n_pages, rows, PAGE_SIZE), F32), pltpu.SemaphoreType.DMA((n_slots,))]


def _score_sample_kernel(layer, n_steps, n_pages,
                         pt_ref, qi_ref, wrep_ref, tailn_ref, ki_hbm, o_ref, ki_buf, ki_sem):
    slot = _fetch_pages(pt_ref, layer, n_steps, n_pages, [(ki_hbm, ki_buf, ki_sem)])
    shape = (SUBLANES, LANES)
    qi = qi_ref[...]
    w = wrep_ref[...]

    def score_block(dots):
        return _fold_tiles(w * jnp.maximum(dots * IDX_DIM ** -0.5, 0.0))

    for p in range(n_pages):
        o_ref[:, p * PAGE_SIZE:(p + 1) * PAGE_SIZE] = score_block(_dot(qi, ki_buf[slot, p].astype(BF16)))
    ki_new = _pad_rows(tailn_ref[...][:, :IDX_DIM], PAGE_SIZE).astype(BF16)
    causal = lax.broadcasted_iota(I32, shape, 1) <= lax.broadcasted_iota(I32, shape, 0)
    o_ref[:, n_pages * PAGE_SIZE:] = jnp.where(causal, score_block(_nt_dot(qi, ki_new)), NEG_INF)


def _score_sample(cfg, layer, page_table, qi_ht, wrep, tail_new, cache_ki):
    b, t_rows = cfg.dec_batch, cfg.dec_seq
    n_pages = cfg.past_len // PAGE_SIZE
    rows = N_HEADS * t_rows
    per_b = lambda r, w: pl.BlockSpec((None, r, w), lambda i, pt: (i, 0, 0))
    n_keys = (n_pages + 1) * PAGE_SIZE
    return pl.pallas_call(
        functools.partial(_score_sample_kernel, layer, b, n_pages),
        grid_spec=pltpu.PrefetchScalarGridSpec(
            num_scalar_prefetch=1, grid=(b,),
            in_specs=[per_b(rows, IDX_DIM), per_b(rows, LANES), per_b(t_rows, TAIL),
                      pl.BlockSpec(memory_space=pl.ANY)],
            out_specs=per_b(t_rows, n_keys),
            scratch_shapes=_page_scratch(SCORE_PAGE_SLOTS, n_pages, IDX_DIM)),
        out_shape=jax.ShapeDtypeStruct((b, t_rows, n_keys), F32),
        name="score_sample",
    )(page_table, qi_ht, wrep, tail_new, cache_ki)


def _threshold_kernel(k_sel, sc_ref, thr_ref, lim_ref, s_ref):
    n_keys, cols = s_ref.shape
    s_ref[...] = sc_ref[...].reshape(cols, n_keys).T
    rep = (SUBLANES, cols)
    idx_bits = (n_keys - 1).bit_length() + 1

    def count_where(pred):
        acc = _tree([_count_tiles(s_ref[c * LANES:(c + 1) * LANES, :], c * LANES, pred)
                     for c in range(n_keys // LANES)])
        return jnp.broadcast_to(jnp.sum(acc, axis=0, keepdims=True), rep)

    thr, nge = _kth_largest(lambda cf: count_where(lambda s, idx: jnp.where(s >= cf, 1.0, 0.0)), k_sel, rep)
    thr_ref[...] = thr
    _resolve_ties(k_sel, thr, nge, count_where, idx_bits, lim_ref)


def _threshold_cols(k_sel, scores, cols):
    b, t_rows, n_keys = scores.shape
    n_q = b * t_rows
    spec = pl.BlockSpec((SUBLANES, cols), lambda i: (0, i))
    return pl.pallas_call(
        functools.partial(_threshold_kernel, k_sel),
        grid=(n_q // cols,),
        in_specs=[pl.BlockSpec((cols // t_rows, t_rows, n_keys), lambda i: (i, 0, 0))],
        out_specs=[spec, spec],
        out_shape=[jax.ShapeDtypeStruct((SUBLANES, n_q), F32), jax.ShapeDtypeStruct((SUBLANES, n_q), I32)],
        scratch_shapes=[pltpu.VMEM((n_keys, cols), F32)],
        name="threshold_cols",
    )(scores)


def _attn_sample_kernel(layer, n_steps, n_pages, pt_ref,
                        sc_ref, thr_ref, lim_ref, q_ref, kn_ref, vn_ref, bfar_ref, blast_ref, bnew_ref,
                        k_hbm, v_hbm, o_ref, k_buf, k_sem, v_buf, v_sem):
    slot = _fetch_pages(pt_ref, layer, n_steps, n_pages, [(k_hbm, k_buf, k_sem), (v_hbm, v_buf, v_sem)])
    t_rows = SUBLANES
    n_blocks = n_pages + 1
    lane = lax.broadcasted_iota(I32, (t_rows, LANES), 1)
    thr, lim = thr_ref[...], lim_ref[...]
    masks = [_select_mask(sc_ref[:, b * PAGE_SIZE:(b + 1) * PAGE_SIZE], lane + b * PAGE_SIZE, thr, lim)
             for b in range(n_blocks)]

    def pad_new(x):
        return _pad_rows(x, PAGE_SIZE)

    rows = N_HEADS * t_rows
    same_head = (lax.broadcasted_iota(I32, (rows, D_ATT), 0) // t_rows
                 == lax.broadcasted_iota(I32, (rows, D_ATT), 1) // HEAD_DIM)
    q_bd = jnp.where(same_head, jnp.concatenate([q_ref[...]] * N_HEADS, axis=0), 0.0).astype(BF16)

    def tile_heads(m):
        return jnp.concatenate([m] * N_HEADS, axis=0)

    logits = []
    for p in range(n_pages):
        bias = blast_ref[...] if p == n_pages - 1 else bfar_ref[...]
        logits.append(_dot(q_bd, k_buf[slot, p].astype(BF16)) + bias + tile_heads(masks[p]))
    logits.append(_nt_dot(q_bd, pad_new(kn_ref[...]).astype(BF16)) + bnew_ref[...] + tile_heads(masks[n_pages]))

    m = logits[0]
    for lg in logits[1:]:
        m = jnp.maximum(m, lg)
    m = _lane_rep(jnp.max(m, axis=1, keepdims=True))
    lsum = jnp.zeros((rows, LANES), F32)
    acc = jnp.zeros((rows, D_ATT), F32)
    for b in range(n_blocks):
        p = jnp.exp(logits[b] - m)
        lsum = lsum + p
        if b < n_pages:
            acc = acc + _nt_dot(p.astype(BF16), v_buf[slot, b].astype(BF16))
        else:
            acc = acc + _dot(p.astype(BF16), pad_new(vn_ref[...]).astype(BF16))
    out = jnp.where(same_head, acc / jnp.sum(lsum, axis=1, keepdims=True), 0.0)
    o_ref[...] = _fold_tiles(out)


def _attn_sample(cfg, layer, page_table, scores, thr_rep, lim_rep, q, k_new, v_new,
                 s_far, s_last, s_new, cache_k, cache_v):
    b = cfg.dec_batch
    t_rows = cfg.dec_seq
    n_pages = cfg.past_len // PAGE_SIZE
    rows = N_HEADS * t_rows
    per_b = lambda r, w: pl.BlockSpec((None, r, w), lambda i, pt: (i, 0, 0))
    const = lambda r, w: pl.BlockSpec((r, w), lambda i, pt: (0, 0))
    in_hbm = pl.BlockSpec(memory_space=pl.ANY)
    in_specs = [per_b(t_rows, scores.shape[2]), per_b(t_rows, LANES), per_b(t_rows, LANES),
                per_b(t_rows, D_ATT), per_b(t_rows, D_ATT), per_b(t_rows, D_ATT),
                const(rows, LANES), const(rows, LANES), const(rows, LANES), in_hbm, in_hbm]
    return pl.pallas_call(
        functools.partial(_attn_sample_kernel, layer, b, n_pages),
        grid_spec=pltpu.PrefetchScalarGridSpec(
            num_scalar_prefetch=1, grid=(b,), in_specs=in_specs, out_specs=per_b(t_rows, D_ATT),
            scratch_shapes=(_page_scratch(ATTN_PAGE_SLOTS, n_pages, D_ATT)
                            + _page_scratch(ATTN_PAGE_SLOTS, n_pages, D_ATT))),
        out_shape=jax.ShapeDtypeStruct((b, t_rows, D_ATT), F32),
        compiler_params=pltpu.CompilerParams(vmem_limit_bytes=VMEM_LIMIT),
        name="attn_sample",
    )(page_table, scores, thr_rep, lim_rep, q, k_new, v_new, s_far, s_last, s_new, cache_k, cache_v)


def _forward(cfg, x_prompt, x_sample, cache_k, cache_v, cache_kidx, state_pool, page_table, c_prompt,
             c_sample, rel_bias, ada_w, ada_b, ln1_pre, ln1_post, ln2_pre, ln2_post, w_in, pool_w,
             pool_scale, w_out, w_ff1, w_ff2):
    depth = ada_w.shape[0]
    bp, seq, bs, ts = cfg.batch, cfg.seq, cfg.dec_batch, cfg.dec_seq
    assert ts == SUBLANES and seq % cfg.blk == 0 and cfg.past_len % PAGE_SIZE == 0
    assert cfg.blk >= PAGE_SIZE and seq >= TOPK_MAX and seq // BF16_ROWS <= 256
    k_prompt = min(TOPK_MAX, seq // 4)
    k_sample = min(TOPK_MAX, (cfg.past_len + ts) // 4)
    rp, rs = bp * seq, bs * ts
    tm_s = min(IN_SUB_ROWS, rs)

    mod = _ada_mod(jnp.concatenate([c_prompt, c_sample], axis=0), ada_w, ada_b)
    tb, s_far, s_last, s_new = _bias_tables(rel_bias, cfg.blk)

    n_phys = cache_k.shape[1]
    cache_kt = cache_k.transpose(0, 1, 3, 4, 2).reshape(depth, n_phys, D_ATT, PAGE_SIZE)
    cache_vt = cache_v.transpose(0, 1, 3, 4, 2).reshape(depth, n_phys, D_ATT, PAGE_SIZE)
    cache_kit = cache_kidx.transpose(0, 1, 3, 2)

    xp = x_prompt.reshape(rp, D_MODEL)
    xs = x_sample.reshape(rs, D_MODEL)
    state_p = ()
    pool_p = []
    outs_s = [[], [], [], []]
    w_in_b = jnp.pad(w_in, ((0, 0), (0, 0), (0, D_IN_PAD - D_IN))).astype(BF16)
    wo_b, w1_b, w2_b = w_out.astype(BF16), w_ff1.astype(BF16), w_ff2.astype(BF16)
    for l in range(depth):
        lnrow = lambda a: a[l].reshape(1, D_MODEL)
        mod_p = mod[l, :bp].reshape(bp, 1, 6 * D_MODEL)
        mod_s = jnp.repeat(mod[l, bp:], ts, axis=0)
        psc = pool_scale[l].reshape(1, C_POOL)

        tm_in = min(seq, cfg.tm if state_p else max(IN_SUB_ROWS, cfg.tm // depth))
        u, kb, tailb, *state_p, wit, qt, qit, vbt = _in_proj(
            xp, mod_p, seq, lnrow(ln1_pre), w_in_b, tm_in, layer=l, depth=depth, carried=tuple(state_p))
        u3 = u.reshape(bp, seq, C_POOL)
        y_pool = _pool_mix(u3, None, 0, pool_w[l], psc, 1, min(256, seq)).reshape(rp, C_POOL)
        y_att = _attn_prompt(cfg, k_prompt, qt, qit, wit, kb, tailb, vbt, tb)
        xp = _out_ffn(xp, y_pool, y_att, mod_p, seq, lnrow(ln1_post), lnrow(ln2_pre), lnrow(ln2_post),
                      wo_b, w1_b, w2_b, l, min(cfg.tm_out, seq))
        pool_p.append(u3[:, seq - POOL_HIST:])

        u, q, qi, tail, k, v = _in_proj(xs, mod_s, 0, lnrow(ln1_pre), w_in_b, tm_s, layer=l)
        u3 = u.reshape(bs, ts, C_POOL)
        hist = jnp.concatenate([jnp.zeros((bs, POOL_PAD - POOL_HIST, C_POOL), F32), state_pool[l]], axis=1)
        y_pool = _pool_mix(u3, hist, cfg.past_len, pool_w[l], psc, min(16, bs), ts).reshape(rs, C_POOL)
        qi_ht = qi.reshape(bs, ts, IDX_HEADS, IDX_DIM).transpose(0, 2, 1, 3).reshape(bs, IDX_HEADS * ts, IDX_DIM)
        wi = tail[:, OFF_WI - OFF_KI:OFF_WI - OFF_KI + IDX_HEADS].reshape(bs, ts, IDX_HEADS) * IDX_HEADS ** -0.5
        wrep = jnp.broadcast_to(wi.transpose(0, 2, 1).reshape(bs, IDX_HEADS * ts, 1), (bs, IDX_HEADS * ts, LANES))
        scores = _score_sample(cfg, l, page_table, qi_ht, wrep, tail.reshape(bs, ts, TAIL), cache_kit)
        thr, lim = _threshold_cols(k_sample, scores, min(256, rs))
        per_query = lambda a: jnp.broadcast_to(a[0].reshape(bs, ts, 1), (bs, ts, LANES))
        y_att = _attn_sample(cfg, l, page_table, scores, per_query(thr), per_query(lim),
                             q.astype(F32).reshape(bs, ts, D_ATT),
                             k.reshape(bs, ts, D_ATT), v.reshape(bs, ts, D_ATT),
                             s_far, s_last, s_new, cache_kt, cache_vt).reshape(rs, D_ATT)
        xs = _out_ffn(xs, y_pool, y_att, mod_s, 0, lnrow(ln1_post), lnrow(ln2_pre), lnrow(ln2_post),
                      wo_b, w1_b, w2_b, l, min(OUT_SUB_ROWS, rs))
        for lst, a in zip(outs_s, (k.reshape(bs, ts, N_HEADS, HEAD_DIM), v.reshape(bs, ts, N_HEADS, HEAD_DIM),
                                   tail[:, :IDX_DIM].reshape(bs, ts, IDX_DIM),
                                   jnp.concatenate([hist, u3], axis=1)[:, -POOL_HIST:])):
            lst.append(a)

    kt, vt, kit = state_p
    heads_last = lambda a: a.reshape(depth, bp, N_HEADS, HEAD_DIM, seq).transpose(0, 1, 4, 2, 3)
    return (xp.reshape(bp, seq, D_MODEL), xs.reshape(bs, ts, D_MODEL),
            heads_last(kt), heads_last(vt), kit.transpose(0, 1, 3, 2), jnp.stack(pool_p),
            *[jnp.stack(a) for a in outs_s])


def kernel(x_prompt, x_sample, cache_k, cache_v, cache_kidx, state_pool, page_table, c_prompt, c_sample,
           rel_bias, ada_w, ada_b, ln1_pre, ln1_post, ln2_pre, ln2_post, w_in, pool_w, pool_scale,
           w_out, w_ff1, w_ff2):
    cfg = Cfg(batch=x_prompt.shape[0], seq=x_prompt.shape[1], dec_batch=x_sample.shape[0],
              dec_seq=x_sample.shape[1], past_len=page_table.shape[1] * PAGE_SIZE, blk=256, tm=1024,
              tm_out=1024)
    return _forward(cfg, x_prompt, x_sample, cache_k, cache_v, cache_kidx, state_pool, page_table, c_prompt,
                    c_sample, rel_bias, ada_w, ada_b, ln1_pre, ln1_post, ln2_pre, ln2_post, w_in, pool_w,
                    pool_scale, w_out, w_ff1, w_ff2)
```

```python
import functools
import math
from typing import NamedTuple

import jax
import jax.numpy as jnp
from jax import lax
from jax.experimental import pallas as pl
from jax.experimental.pallas import tpu as pltpu

F32 = jnp.float32
BF16 = jnp.bfloat16
I32 = jnp.int32

D_MODEL = 1024
C_POOL = 512
D_ATT = 512
HEAD_DIM = 64
N_HEADS = 8
IDX_HEADS = 8
IDX_DIM = 64
POOL_WINDOWS = (2, 4, 8, 16)
POOL_GC = C_POOL // len(POOL_WINDOWS)
POOL_HIST = max(POOL_WINDOWS) - 1
TOPK_MAX = 256
N_BUCKETS = 32
MAX_DISTANCE = 128
D_FF = 4 * D_MODEL
EPS = 1e-6
PAGE_SIZE = 128
OFF_Q = C_POOL
OFF_K = OFF_Q + D_ATT
OFF_V = OFF_K + D_ATT
OFF_QI = OFF_V + D_ATT
OFF_KI = OFF_QI + IDX_HEADS * IDX_DIM
OFF_WI = OFF_KI + IDX_DIM
D_IN = OFF_WI + IDX_HEADS

LANES = 128
SUBLANES = 8
BF16_ROWS = 2 * SUBLANES
D_IN_PAD = ((D_IN + LANES - 1) // LANES) * LANES
TAIL = D_IN_PAD - OFF_KI
POOL_PAD = 16
OUT_SUB_ROWS = 256
IN_SUB_ROWS = 256
SCORE_PAGE_SLOTS = 4
ATTN_PAGE_SLOTS = 3
LATE_ROUNDS = (26, 29, 32)
VMEM_LIMIT = 56 * 1024 * 1024

NEG_INF = float("-inf")
INT_MIN = -(2 ** 31)


class Cfg(NamedTuple):
    batch: int
    seq: int
    dec_batch: int
    dec_seq: int
    past_len: int
    blk: int
    tm: int
    tm_out: int


def _rms(x, g):
    ms = jnp.mean(x * x, axis=-1, keepdims=True)
    return x * lax.rsqrt(ms + EPS) * g


def _nt_dot(a, b):
    return lax.dot_general(a, b, (((1,), (1,)), ((), ())), preferred_element_type=F32)


def _dot(a, b):
    return jnp.dot(a, b, preferred_element_type=F32)


def _key_to_float(key):
    bits = key ^ ((key >> 31) & jnp.int32(0x7FFFFFFF))
    return lax.bitcast_convert_type(bits, F32)


def _lane_rep(col, width=LANES):
    return jnp.broadcast_to(col, (col.shape[0], width))


def _fold_tiles(x, op=jnp.add):
    return _tree([x[r * SUBLANES:(r + 1) * SUBLANES] for r in range(x.shape[0] // SUBLANES)], op)


def _tree(parts, op=jnp.add):
    while len(parts) > 1:
        parts = [op(parts[i], parts[i + 1]) if i + 1 < len(parts) else parts[i] for i in range(0, len(parts), 2)]
    return parts[0]


def _ada_kernel(c_ref, w_ref, b_ref, o_ref):
    o_ref[...] = _dot(c_ref[...].astype(BF16), w_ref[...].astype(BF16)) + b_ref[...]


def _ada_mod(c_all, ada_w, ada_b):
    depth, d, n = ada_w.shape
    rc = c_all.shape[0]
    tn = 1536
    return pl.pallas_call(
        _ada_kernel,
        grid=(depth, n // tn),
        in_specs=[pl.BlockSpec((rc, d), lambda l, j: (0, 0)),
                  pl.BlockSpec((None, d, tn), lambda l, j: (l, 0, j)),
                  pl.BlockSpec((None, 1, tn), lambda l, j: (l, 0, j))],
        out_specs=pl.BlockSpec((None, rc, tn), lambda l, j: (l, 0, j)),
        out_shape=jax.ShapeDtypeStruct((depth, rc, n), F32),
        name="ada_mod",
    )(c_all, ada_w, ada_b.reshape(depth, 1, n))


def _mod_spec(per_batch_rows, tm, chunk):
    if per_batch_rows:
        tiles = per_batch_rows // tm
        return pl.BlockSpec((None, 1, D_MODEL), lambda i: (i // tiles, 0, chunk))
    return pl.BlockSpec((tm, D_MODEL), lambda i: (i, chunk))


def _in_kernel(transposed, n_carried, n_sub, layer, x_ref, sh_ref, sc_ref, ln_ref, w_ref, *refs):
    u_ref, *refs = refs[n_carried:]
    sub = x_ref.shape[0] // n_sub
    mod = lambda ref, rows: ref[...] if ref.shape[0] == 1 else ref[rows, :]

    def put_state(ref, cols, val):
        if len(ref.shape) == 2:
            ref[:, cols] = val
        else:
            for l in range(ref.shape[0]):
                ref[l, :, cols] = val if l == layer else jnp.zeros(val.shape, val.dtype)

    def project(i):
        rows = pl.ds(i * sub, sub)
        h = _rms(x_ref[rows, :], ln_ref[...]) * (1.0 + mod(sc_ref, rows)) + mod(sh_ref, rows)
        return _dot(h.astype(BF16), w_ref[...])

    def emit(i, z):
        rows = pl.ds(i * sub, sub)
        u_ref[rows, :] = z[:, :OFF_Q]
        q = z[:, OFF_Q:OFF_K] * HEAD_DIM ** -0.5
        k = z[:, OFF_K:OFF_V]
        v = z[:, OFF_V:OFF_QI]
        qi = z[:, OFF_QI:OFF_KI]
        tail = z[:, OFF_KI:]
        if transposed:
            kb_ref, tailb_ref, kt_ref, vt_ref, kit_ref, wit_ref, qt_ref, qit_ref, vbt_ref = refs
            cols = pl.ds(i * sub, sub)
            kb_ref[rows, :] = k.astype(BF16)
            tailb_ref[rows, :] = tail.astype(BF16)
            put_state(kt_ref, cols, k.T)
            vt = v.T
            put_state(vt_ref, cols, vt)
            vbt_ref[:, cols] = vt.astype(BF16)
            tail_t = tail.T
            put_state(kit_ref, cols, tail_t[:IDX_DIM])
            wit_ref[:, cols] = tail_t[OFF_WI - OFF_KI:OFF_WI - OFF_KI + IDX_HEADS]
            qt_ref[:, cols] = q.T.astype(BF16)
            qit_ref[:, cols] = qi.T.astype(BF16)
        else:
            q_ref, qi_ref, tail_ref, k_ref, v_ref = refs
            q_ref[rows, :] = q.astype(BF16)
            qi_ref[rows, :] = qi.astype(BF16)
            tail_ref[rows, :] = tail
            k_ref[rows, :] = k
            v_ref[rows, :] = v

    z = project(0)
    for i in range(n_sub):
        z_next = project(i + 1) if i + 1 < n_sub else None
        emit(i, z)
        z = z_next


def _in_proj(x, mod, per_batch_rows, ln, w_in_b, tm, layer=0, depth=1, carried=()):
    r = x.shape[0]
    row = lambda width: pl.BlockSpec((tm, width), lambda i: (i, 0))
    const = lambda shape: pl.BlockSpec(shape, lambda i: (0, 0))
    in_specs = [row(D_MODEL), _mod_spec(per_batch_rows, tm, 0), _mod_spec(per_batch_rows, tm, 1),
                const((1, D_MODEL)), pl.BlockSpec((None, D_MODEL, D_IN_PAD), lambda i: (layer, 0, 0))]
    aliases = {}
    if per_batch_rows:
        tiles = per_batch_rows // tm
        nb = r // per_batch_rows
        rows_out = [(C_POOL, F32), (D_ATT, BF16), (TAIL, BF16)]
        state_out = [D_ATT, D_ATT, IDX_DIM]
        cols_out = [(IDX_HEADS, F32), (D_ATT, BF16), (IDX_HEADS * IDX_DIM, BF16), (D_ATT, BF16)]
    else:
        rows_out = [(C_POOL, F32), (D_ATT, BF16), (IDX_HEADS * IDX_DIM, BF16), (TAIL, F32), (D_ATT, F32), (D_ATT, F32)]
        state_out, cols_out = [], []
    out_specs = [row(w) for w, _ in rows_out]
    out_shape = [jax.ShapeDtypeStruct((r, w), dt) for w, dt in rows_out]
    for n, width in enumerate(state_out):
        if carried:
            aliases[len(in_specs)] = len(out_specs)
            in_specs.append(pl.BlockSpec(memory_space=pl.ANY))
            out_specs.append(pl.BlockSpec((None, None, width, tm), lambda i: (layer, i // tiles, 0, i % tiles)))
        else:
            out_specs.append(pl.BlockSpec((depth, None, width, tm), lambda i: (0, i // tiles, 0, i % tiles)))
        out_shape.append(jax.ShapeDtypeStruct((depth, nb, width, per_batch_rows), F32))
    for width, dt in cols_out:
        out_specs.append(pl.BlockSpec((None, width, tm), lambda i: (i // tiles, 0, i % tiles)))
        out_shape.append(jax.ShapeDtypeStruct((nb, width, per_batch_rows), dt))
    return pl.pallas_call(
        functools.partial(_in_kernel, bool(per_batch_rows), len(carried), max(1, tm // IN_SUB_ROWS), layer),
        grid=(r // tm,),
        in_specs=in_specs,
        out_specs=out_specs,
        out_shape=out_shape,
        input_output_aliases=aliases,
        compiler_params=pltpu.CompilerParams(vmem_limit_bytes=VMEM_LIMIT),
        name="in_proj",
    )(x, mod, mod, ln, w_in_b, *carried)


def _out_kernel(n_sub, x_ref, yp_ref, ya_ref, g1_ref, sh2_ref, sc2_ref, g2_ref,
                ln1_ref, ln2a_ref, ln2b_ref, wo_ref, w1_ref, w2_ref, o_ref):
    sub = x_ref.shape[0] // n_sub
    rows = [pl.ds(i * sub, sub) for i in range(n_sub)]
    mod = lambda ref, i: ref[...] if ref.shape[0] == 1 else ref[rows[i], :]

    def out_proj(i):
        return (_dot(yp_ref[rows[i], :].astype(BF16), wo_ref[:C_POOL, :])
                + _dot(ya_ref[rows[i], :].astype(BF16), wo_ref[C_POOL:, :]))

    def residual_and_prenorm(i, mixed):
        x1 = x_ref[rows[i], :] + mod(g1_ref, i) * _rms(mixed, ln1_ref[...])
        h = (_rms(x1, ln2a_ref[...]) * (1.0 + mod(sc2_ref, i)) + mod(sh2_ref, i)).astype(BF16)
        return x1, h

    def mlp_chunk(h, f, c):
        sl = slice(c * D_MODEL, (c + 1) * D_MODEL)
        a = jnp.maximum(_dot(h, w1_ref[:, sl]), 0.0)
        return f + _dot((a * a).astype(BF16), w2_ref[sl, :])

    def finish(i, x1, f):
        o_ref[rows[i], :] = x1 + mod(g2_ref, i) * _rms(f, ln2b_ref[...])

    n_chunks = D_FF // D_MODEL
    mixed = [out_proj(i) for i in range(n_sub)]
    x1, h = residual_and_prenorm(0, mixed[0])
    for i in range(n_sub):
        f = mlp_chunk(h, jnp.zeros(x1.shape, F32), 0)
        if i + 1 < n_sub:
            nxt = residual_and_prenorm(i + 1, mixed[i + 1])
        if i > 0:
            finish(i - 1, *done)
        for c in range(1, n_chunks):
            f = mlp_chunk(h, f, c)
        done = (x1, f)
        if i + 1 < n_sub:
            x1, h = nxt
    finish(n_sub - 1, *done)


def _out_ffn(x, yp, ya, mod, per_batch_rows, ln1_post, ln2_pre, ln2_post, wo_b, w1_b, w2_b, layer, tm):
    r = x.shape[0]
    row = lambda width: pl.BlockSpec((tm, width), lambda i: (i, 0))
    const = lambda shape: pl.BlockSpec(shape, lambda i: (0, 0), pipeline_mode=pl.Buffered(1))
    weight = lambda shape: pl.BlockSpec((None,) + shape, lambda i: (layer, 0, 0), pipeline_mode=pl.Buffered(1))
    ms = lambda chunk: _mod_spec(per_batch_rows, tm, chunk)
    return pl.pallas_call(
        functools.partial(_out_kernel, max(1, tm // OUT_SUB_ROWS)),
        grid=(r // tm,),
        in_specs=[row(D_MODEL), row(C_POOL), row(D_ATT), ms(2), ms(3), ms(4), ms(5),
                  const((1, D_MODEL)), const((1, D_MODEL)), const((1, D_MODEL)),
                  weight((D_MODEL, D_MODEL)), weight((D_MODEL, D_FF)), weight((D_FF, D_MODEL))],
        out_specs=row(D_MODEL),
        out_shape=jax.ShapeDtypeStruct((r, D_MODEL), F32),
        compiler_params=pltpu.CompilerParams(vmem_limit_bytes=VMEM_LIMIT),
        name="out_ffn",
    )(x, yp, ya, mod, mod, mod, mod, ln1_post, ln2_pre, ln2_post, wo_b, w1_b, w2_b)


def _pool_kernel(pos0, t_rows, chunk, has_hist, u_ref, *refs):
    if has_hist:
        hist_ref, pw_ref, ps_ref, o_ref, ue_ref = refs
        ue_ref[:, :POOL_PAD, :] = hist_ref[...]
    else:
        pw_ref, ps_ref, o_ref, ue_ref = refs
        ue_ref[:, :POOL_PAD, :] = jnp.zeros((ue_ref.shape[0], POOL_PAD, C_POOL), F32)
    ue_ref[:, POOL_PAD:, :] = u_ref[...]
    g_elems = ue_ref.shape[0]
    n_chunks = t_rows // chunk
    for ci in range(n_chunks):
        r0 = ci * chunk
        pos = pos0 + r0 + lax.broadcasted_iota(I32, (chunk, POOL_GC), 0)
        for g, w in enumerate(POOL_WINDOWS):
            lanes = pl.ds(g * POOL_GC, POOL_GC)
            cnt = jnp.minimum(pos + 1, w).astype(F32)
            pooled = []
            for e in range(g_elems):
                ext = ue_ref[e, pl.ds(r0, POOL_PAD + chunk), lanes]
                cur = ext[POOL_PAD:]
                acc = ext[POOL_PAD - (w - 1):]
                step = 1
                while step < w:
                    acc = acc[step:] + acc[:acc.shape[0] - step]
                    step *= 2
                pooled.append(acc / cnt - cur)
            pooled = pooled[0] if g_elems == 1 else jnp.concatenate(pooled, axis=0)
            y = _dot(pooled.astype(BF16), pw_ref[g].astype(BF16))
            y = y * ps_ref[:, lanes]
            for e in range(g_elems):
                o_ref[e, pl.ds(r0, chunk), lanes] = y[e * chunk:(e + 1) * chunk].astype(o_ref.dtype)


def _pool_mix(u, hist, pos0, pool_w_l, pool_scale_l, g_elems, chunk):
    b, t_rows, _ = u.shape
    per_elem = lambda rows: pl.BlockSpec((g_elems, rows, C_POOL), lambda i: (i, 0, 0))
    in_specs = [per_elem(t_rows)] + ([per_elem(POOL_PAD)] if hist is not None else []) + [
        pl.BlockSpec((len(POOL_WINDOWS), POOL_GC, POOL_GC), lambda i: (0, 0, 0)),
        pl.BlockSpec((1, C_POOL), lambda i: (0, 0))]
    args = (u,) + ((hist,) if hist is not None else ()) + (pool_w_l, pool_scale_l)
    return pl.pallas_call(
        functools.partial(_pool_kernel, pos0, t_rows, chunk, hist is not None),
        grid=(b // g_elems,),
        in_specs=in_specs,
        out_specs=per_elem(t_rows),
        out_shape=jax.ShapeDtypeStruct((b, t_rows, C_POOL), BF16 if t_rows % BF16_ROWS == 0 else F32),
        scratch_shapes=[pltpu.VMEM((g_elems, POOL_PAD + t_rows, C_POOL), F32)],
        compiler_params=pltpu.CompilerParams(vmem_limit_bytes=VMEM_LIMIT),
        name="pool_mix",
    )(*args)


def _bias_of_dist(dist, rb_ref, h):
    n = jnp.maximum(dist, 0)
    max_exact = N_BUCKETS // 2
    large = max_exact + (jnp.log(jnp.maximum(n, 1).astype(F32) / max_exact)
                         / math.log(MAX_DISTANCE / max_exact) * (N_BUCKETS - max_exact)).astype(I32)
    large = jnp.minimum(large, N_BUCKETS - 1)
    bucket = jnp.where(n < max_exact, n, large)
    out = jnp.zeros(dist.shape, F32)
    for b in range(N_BUCKETS):
        out = jnp.where(bucket == b, rb_ref[b, h], out)
    return out


def _bias_kernel(blk, rb_ref, tb_ref, s_far_ref, s_last_ref, s_new_ref):
    key = lax.broadcasted_iota(I32, (blk, blk), 0)
    qry = lax.broadcasted_iota(I32, (blk, blk), 1)
    t = lax.broadcasted_iota(I32, (SUBLANES, LANES), 0)
    jj = lax.broadcasted_iota(I32, (SUBLANES, LANES), 1)
    for h in range(N_HEADS):
        tb_ref[0, h] = _bias_of_dist(jnp.full((blk, blk), 2 * blk, I32), rb_ref, h)
        tb_ref[1, h] = _bias_of_dist(blk + qry - key, rb_ref, h)
        tb_ref[2, h] = _bias_of_dist(qry - key, rb_ref, h)
        rows = pl.ds(h * SUBLANES, SUBLANES)
        s_far_ref[rows, :] = _bias_of_dist(jnp.full((SUBLANES, LANES), 2 * PAGE_SIZE, I32), rb_ref, h)
        s_last_ref[rows, :] = _bias_of_dist(PAGE_SIZE + t - jj, rb_ref, h)
        s_new_ref[rows, :] = _bias_of_dist(t - jj, rb_ref, h)


def _bias_tables(rel_bias, blk):
    sm = jax.ShapeDtypeStruct((N_HEADS * SUBLANES, LANES), F32)
    return pl.pallas_call(
        functools.partial(_bias_kernel, blk),
        in_specs=[pl.BlockSpec(memory_space=pltpu.SMEM)],
        out_shape=[jax.ShapeDtypeStruct((3, N_HEADS, blk, blk), F32), sm, sm, sm],
        compiler_params=pltpu.CompilerParams(vmem_limit_bytes=VMEM_LIMIT),
        name="bias_tables",
    )(rel_bias)


def _kth_start(shape):
    return jnp.full(shape, INT_MIN, I32), jnp.zeros(shape, F32)


def _kth_rounds(count_ge_key, k, carry, first_round, last_round, digit_bits=1):
    def body(r, carry):
        tkey, nge = carry
        shift = 32 - digit_bits * (r + 1)
        best_key, best_cnt = tkey, nge
        for j in range(1, 1 << digit_bits):
            ckey = tkey + lax.shift_left(jnp.int32(j), shift)
            cnt = count_ge_key(ckey)
            ok = cnt >= k
            best_key = jnp.where(ok, ckey, best_key)
            best_cnt = jnp.where(ok, cnt, best_cnt)
        return best_key, best_cnt

    return lax.fori_loop(first_round, last_round, body, carry)


def _kth_finish(carry):
    tkey, nge = carry
    return jnp.where(tkey == INT_MIN, NEG_INF, _key_to_float(tkey)), nge


def _kth_largest(count_ge, k, shape, digit_bits=1):
    carry = _kth_rounds(lambda ckey: count_ge(_key_to_float(ckey)), k, _kth_start(shape),
                        0, 32 // digit_bits, digit_bits)
    return _kth_finish(carry)


def _tie_limit(count_eq_below, need, shape, bits):
    def body(it, lim):
        cand = lim + lax.shift_left(jnp.int32(1), bits - 1 - it)
        return jnp.where(count_eq_below(cand) <= need, cand, lim)

    return lax.fori_loop(0, bits, body, jnp.zeros(shape, I32))


def _count_tiles(s, first_key, pred):
    row = lax.broadcasted_iota(I32, (SUBLANES, s.shape[1]), 0)
    return _tree([pred(s[r * SUBLANES:(r + 1) * SUBLANES], row + (first_key + r * SUBLANES))
                  for r in range(s.shape[0] // SUBLANES)])


def _resolve_ties(k_sel, thr, nge, count_where, idx_bits, lim_ref):
    lim_ref[...] = jnp.full(lim_ref.shape, 1 << idx_bits, I32)

    @pl.when(jnp.max(nge) > k_sel)
    def _():
        n_gt = count_where(lambda s, idx: jnp.where(s > thr, 1.0, 0.0))
        need = k_sel - n_gt
        lim_ref[...] = _tie_limit(
            lambda cand: count_where(lambda s, idx: jnp.where(s == thr, jnp.where(idx < cand, 1.0, 0.0), 0.0)),
            need, lim_ref.shape, idx_bits)


def _select_mask(s, idx, thr, lim):
    tie = jnp.where(s == thr, jnp.where(idx < lim, 0.0, NEG_INF), NEG_INF)
    m = jnp.where(s > thr, 0.0, tie)
    return jnp.where(s == NEG_INF, NEG_INF, m)


def _attn_prompt_kernel(k_sel, blk, seq,
                        qt_ref, qit_ref, wit_ref, kb_ref, kib_ref, vbt_ref, tb_ref,
                        o_ref,
                        score_ref, trunc_ref, lg_ref, mrun_ref, lsum_ref, acc_ref, thr_ref, excess_ref,
                        tkey_ref, nge_ref):
    qb = pl.program_id(1)
    n_chunks = qb + 1
    rep = (SUBLANES, blk)

    key_row = lax.broadcasted_iota(I32, (blk, blk), 0)
    qry_col = lax.broadcasted_iota(I32, (blk, blk), 1)

    def chunk_keys(c):
        return pl.ds(pl.multiple_of(c * blk, blk), blk)

    def all_sublanes(x, op):
        return jnp.broadcast_to(op(x, axis=0, keepdims=True), rep)

    w_idx = wit_ref[...] * IDX_HEADS ** -0.5 * IDX_DIM ** -0.5

    def score_chunk(c, carry):
        keys = chunk_keys(c)
        kic = kib_ref[keys, :][:, :IDX_DIM]
        s = jnp.zeros((blk, blk), F32)
        for h in range(IDX_HEADS):
            d = _dot(kic, qit_ref[h * IDX_DIM:(h + 1) * IDX_DIM, :])
            s = s + w_idx[h:h + 1, :] * jnp.maximum(d, 0.0)
        s = jnp.where(key_row + c * blk <= qry_col + qb * blk, s, NEG_INF)
        score_ref[keys, :] = s
        hi = lax.bitcast_convert_type(s, I32) & jnp.int32(-(1 << 16))
        trunc_ref[keys, :] = lax.bitcast_convert_type(hi, F32).astype(BF16)
        return carry

    lax.fori_loop(0, n_chunks, score_chunk, 0)

    def count_where(pred):
        def body(c, acc):
            return acc + _count_tiles(score_ref[chunk_keys(c), :], c * blk, pred)
        acc = lax.fori_loop(0, n_chunks, body, jnp.zeros(rep, F32))
        return all_sublanes(acc, jnp.sum)

    def count_upper_half(ckey):
        bits = ckey ^ ((ckey >> 31) & jnp.int32(0x7FFFFFFF))
        cf = lax.bitcast_convert_type(bits & jnp.int32(-(1 << 16)), F32)
        cb = jnp.concatenate([cf, cf], axis=0).astype(BF16)
        one, zero = jnp.ones((BF16_ROWS, blk), BF16), jnp.zeros((BF16_ROWS, blk), BF16)

        def body(c, acc):
            t = trunc_ref[chunk_keys(c), :]
            return acc + _tree([jnp.where(t[r * BF16_ROWS:(r + 1) * BF16_ROWS] >= cb, one, zero)
                                for r in range(blk // BF16_ROWS)])
        acc = lax.fori_loop(0, n_chunks, body, zero)
        return all_sublanes(acc.astype(F32), jnp.sum)

    carry = _kth_rounds(count_upper_half, k_sel, _kth_start(rep), 0, 16)
    def count_ge_key(ckey):
        cf = _key_to_float(ckey)
        return count_where(lambda s, idx: jnp.where(s >= cf, 1.0, 0.0))

    first_f32 = 16
    tkey_ref[...], nge_ref[...] = _kth_rounds(count_ge_key, k_sel, carry, first_f32, LATE_ROUNDS[0])
    for first, last in zip(LATE_ROUNDS[:-1], LATE_ROUNDS[1:]):
        @pl.when(jnp.max(nge_ref[...]) > k_sel)
        def _():
            tkey_ref[...], nge_ref[...] = _kth_rounds(
                count_ge_key, k_sel, (tkey_ref[...], nge_ref[...]), first, last)
    thr, nge = _kth_finish((tkey_ref[...], nge_ref[...]))
    thr_ref[...] = thr
    excess_ref[...] = nge - k_sel

    suffix_ones = jnp.where(qry_col >= key_row, 1.0, 0.0).astype(BF16)

    def mask_chunk(i, later):
        keys = chunk_keys(n_chunks - 1 - i)
        s = score_ref[keys, :]
        thr_row = thr_ref[0:1, :]
        tied = s == thr_row
        in_chunk = _dot(suffix_ones, jnp.where(tied, 1.0, 0.0).astype(BF16))
        kept = jnp.where(in_chunk + later[0:1, :] > excess_ref[0:1, :], 0.0, NEG_INF)
        m = jnp.where(s > thr_row, 0.0, jnp.where(tied, kept, NEG_INF))
        score_ref[keys, :] = jnp.where(s == NEG_INF, NEG_INF, m)
        return later + jnp.broadcast_to(in_chunk[0:1, :], rep)

    lax.fori_loop(0, n_chunks, mask_chunk, jnp.zeros(rep, F32))

    mrun_ref[...] = jnp.full(mrun_ref.shape, NEG_INF, F32)

    def logits_chunk(c, carry):
        keys = chunk_keys(c)
        madd = score_ref[keys, :]
        table = jnp.clip(c - (qb - 2), 0, 2)
        for h in range(N_HEADS):
            hs = slice(h * HEAD_DIM, (h + 1) * HEAD_DIM)
            s = _dot(kb_ref[keys, hs], qt_ref[hs, :]) + tb_ref[table, h] + madd
            lg_ref[h, keys, :] = s
            mrun_ref[h] = jnp.maximum(mrun_ref[h], _fold_tiles(s, jnp.maximum))
        return carry

    lax.fori_loop(0, n_chunks, logits_chunk, 0)

    for h in range(N_HEADS):
        mrun_ref[h] = all_sublanes(mrun_ref[h], jnp.max)
    lsum_ref[...] = jnp.zeros(lsum_ref.shape, F32)
    acc_ref[...] = jnp.zeros(acc_ref.shape, F32)

    def pv_chunk(c, carry):
        keys = chunk_keys(c)
        for h in range(N_HEADS):
            hs = slice(h * HEAD_DIM, (h + 1) * HEAD_DIM)
            p = jnp.exp(lg_ref[h, keys, :] - mrun_ref[h, 0:1, :])
            lsum_ref[h] += _fold_tiles(p)
            acc_ref[h] += _dot(vbt_ref[hs, keys], p.astype(BF16))
        return carry

    lax.fori_loop(0, n_chunks, pv_chunk, 0)

    outs =[acc_ref[h] / jnp.sum(lsum_ref[h], axis=0, keepdims=True) for h in range(N_HEADS)]
    o_ref[...] = jnp.concatenate(outs, axis=0).T.astype(o_ref.dtype)


def _attn_prompt(cfg, k_sel, qt, qit, wit, kb, kib, vbt, tb):
    blk, seq = cfg.blk, cfg.seq
    nq = seq // blk
    qspec = lambda width: pl.BlockSpec((None, width, blk), lambda b, i: (b, 0, i))
    kspec = lambda width: pl.BlockSpec((seq, width), lambda b, i: (b, 0))
    return pl.pallas_call(
        functools.partial(_attn_prompt_kernel, k_sel, blk, seq),
        grid=(cfg.batch, nq),
        in_specs=[qspec(D_ATT), qspec(IDX_HEADS * IDX_DIM), qspec(IDX_HEADS),
                  kspec(D_ATT), kspec(TAIL), pl.BlockSpec((None, D_ATT, seq), lambda b, i: (b, 0, 0)),
                  pl.BlockSpec((3, N_HEADS, blk, blk), lambda b, i: (0, 0, 0, 0), pipeline_mode=pl.Buffered(1))],
        out_specs=pl.BlockSpec((blk, D_ATT), lambda b, i: (b * nq + i, 0)),
        out_shape=jax.ShapeDtypeStruct((cfg.batch * seq, D_ATT), BF16),
        scratch_shapes=[pltpu.VMEM((seq, blk), F32),
                        pltpu.VMEM((seq, blk), BF16),
                        pltpu.VMEM((N_HEADS, seq, blk), F32),
                        pltpu.VMEM((N_HEADS, SUBLANES, blk), F32),
                        pltpu.VMEM((N_HEADS, SUBLANES, blk), F32),
                        pltpu.VMEM((N_HEADS, HEAD_DIM, blk), F32),
                        pltpu.VMEM((SUBLANES, blk), F32),
                        pltpu.VMEM((SUBLANES, blk), F32),
                        pltpu.VMEM((SUBLANES, blk), I32),
                        pltpu.VMEM((SUBLANES, blk), F32)],
        compiler_params=pltpu.CompilerParams(vmem_limit_bytes=VMEM_LIMIT),
        name="attn_prompt",
    )(qt, qit, wit, kb, kib, vbt, tb)


def _pad_rows(x, rows):
    return jnp.concatenate([x, jnp.zeros((rows - x.shape[0], x.shape[1]), x.dtype)], axis=0)


def _fetch_pages(pt_ref, layer, n_steps, n_pages, caches):
    n_slots = caches[0][1].shape[0]
    ahead = n_slots - 1
    i = pl.program_id(0)
    slot = i % n_slots

    def copy(cache, b, s, p):
        hbm, buf, sem = cache
        return pltpu.make_async_copy(hbm.at[layer, pt_ref[b, p]], buf.at[s, p], sem.at[s])

    def start(b, s):
        for p in range(n_pages):
            for cache in caches:
                copy(cache, b, s, p).start()

    @pl.when(i == 0)
    def _():
        for b in range(min(ahead, n_steps)):
            start(b, b)

    @pl.when(i + ahead < n_steps)
    def _():
        start(i + ahead, (i + ahead) % n_slots)

    for p in range(n_pages):
        for cache in caches:
            copy(cache, i, slot, p).wait()
    return slot


def _page_scratch(n_slots, n_pages, rows):
    return [pltpu.VMEM((n_slots, n_pages, rows, PAGE_SIZE), F32), pltpu.SemaphoreType.DMA((n_slots,))]


def _score_sample_kernel(layer, n_steps, n_pages,
                         pt_ref, qi_ref, wrep_ref, tailn_ref, ki_hbm, o_ref, ki_buf, ki_sem):
    slot = _fetch_pages(pt_ref, layer, n_steps, n_pages, [(ki_hbm, ki_buf, ki_sem)])
    shape = (SUBLANES, LANES)
    qi = qi_ref[...]
    w = wrep_ref[...]

    def score_block(dots):
        return _fold_tiles(w * jnp.maximum(dots * IDX_DIM ** -0.5, 0.0))

    for p in range(n_pages):
        o_ref[:, p * PAGE_SIZE:(p + 1) * PAGE_SIZE] = score_block(_dot(qi, ki_buf[slot, p].astype(BF16)))
    ki_new = _pad_rows(tailn_ref[...][:, :IDX_DIM], PAGE_SIZE).astype(BF16)
    causal = lax.broadcasted_iota(I32, shape, 1) <= lax.broadcasted_iota(I32, shape, 0)
    o_ref[:, n_pages * PAGE_SIZE:] = jnp.where(causal, score_block(_nt_dot(qi, ki_new)), NEG_INF)


def _score_sample(cfg, layer, page_table, qi_ht, wrep, tail_new, cache_ki):
    b, t_rows = cfg.dec_batch, cfg.dec_seq
    n_pages = cfg.past_len // PAGE_SIZE
    rows = N_HEADS * t_rows
    per_b = lambda r, w: pl.BlockSpec((None, r, w), lambda i, pt: (i, 0, 0))
    n_keys = (n_pages + 1) * PAGE_SIZE
    return pl.pallas_call(
        functools.partial(_score_sample_kernel, layer, b, n_pages),
        grid_spec=pltpu.PrefetchScalarGridSpec(
            num_scalar_prefetch=1, grid=(b,),
            in_specs=[per_b(rows, IDX_DIM), per_b(rows, LANES), per_b(t_rows, TAIL),
                      pl.BlockSpec(memory_space=pl.ANY)],
            out_specs=per_b(t_rows, n_keys),
            scratch_shapes=_page_scratch(SCORE_PAGE_SLOTS, n_pages, IDX_DIM)),
        out_shape=jax.ShapeDtypeStruct((b, t_rows, n_keys), F32),
        name="score_sample",
    )(page_table, qi_ht, wrep, tail_new, cache_ki)


def _threshold_kernel(k_sel, sc_ref, thr_ref, lim_ref, s_ref):
    n_keys, cols = s_ref.shape
    s_ref[...] = sc_ref[...].reshape(cols, n_keys).T
    rep = (SUBLANES, cols)
    idx_bits = (n_keys - 1).bit_length() + 1

    def count_where(pred):
        acc = _tree([_count_tiles(s_ref[c * LANES:(c + 1) * LANES, :], c * LANES, pred)
                     for c in range(n_keys // LANES)])
        return jnp.broadcast_to(jnp.sum(acc, axis=0, keepdims=True), rep)

    thr, nge = _kth_largest(lambda cf: count_where(lambda s, idx: jnp.where(s >= cf, 1.0, 0.0)), k_sel, rep)
    thr_ref[...] = thr
    _resolve_ties(k_sel, thr, nge, count_where, idx_bits, lim_ref)


def _threshold_cols(k_sel, scores, cols):
    b, t_rows, n_keys = scores.shape
    n_q = b * t_rows
    spec = pl.BlockSpec((SUBLANES, cols), lambda i: (0, i))
    return pl.pallas_call(
        functools.partial(_threshold_kernel, k_sel),
        grid=(n_q // cols,),
        in_specs=[pl.BlockSpec((cols // t_rows, t_rows, n_keys), lambda i: (i, 0, 0))],
        out_specs=[spec, spec],
        out_shape=[jax.ShapeDtypeStruct((SUBLANES, n_q), F32), jax.ShapeDtypeStruct((SUBLANES, n_q), I32)],
        scratch_shapes=[pltpu.VMEM((n_keys, cols), F32)],
        name="threshold_cols",
    )(scores)


def _attn_sample_kernel(layer, n_steps, n_pages, pt_ref,
                        sc_ref, thr_ref, lim_ref, q_ref, kn_ref, vn_ref, bfar_ref, blast_ref, bnew_ref,
                        k_hbm, v_hbm, o_ref, k_buf, k_sem, v_buf, v_sem):
    slot = _fetch_pages(pt_ref, layer, n_steps, n_pages, [(k_hbm, k_buf, k_sem), (v_hbm, v_buf, v_sem)])
    t_rows = SUBLANES
    n_blocks = n_pages + 1
    lane = lax.broadcasted_iota(I32, (t_rows, LANES), 1)
    thr, lim = thr_ref[...], lim_ref[...]
    masks = [_select_mask(sc_ref[:, b * PAGE_SIZE:(b + 1) * PAGE_SIZE], lane + b * PAGE_SIZE, thr, lim)
             for b in range(n_blocks)]

    def pad_new(x):
        return _pad_rows(x, PAGE_SIZE)

    rows = N_HEADS * t_rows
    same_head = (lax.broadcasted_iota(I32, (rows, D_ATT), 0) // t_rows
                 == lax.broadcasted_iota(I32, (rows, D_ATT), 1) // HEAD_DIM)
    q_bd = jnp.where(same_head, jnp.concatenate([q_ref[...]] * N_HEADS, axis=0), 0.0).astype(BF16)

    def tile_heads(m):
        return jnp.concatenate([m] * N_HEADS, axis=0)

    logits = []
    for p in range(n_pages):
        bias = blast_ref[...] if p == n_pages - 1 else bfar_ref[...]
        logits.append(_dot(q_bd, k_buf[slot, p].astype(BF16)) + bias + tile_heads(masks[p]))
    logits.append(_nt_dot(q_bd, pad_new(kn_ref[...]).astype(BF16)) + bnew_ref[...] + tile_heads(masks[n_pages]))

    m = logits[0]
    for lg in logits[1:]:
        m = jnp.maximum(m, lg)
    m = _lane_rep(jnp.max(m, axis=1, keepdims=True))
    lsum = jnp.zeros((rows, LANES), F32)
    acc = jnp.zeros((rows, D_ATT), F32)
    for b in range(n_blocks):
        p = jnp.exp(logits[b] - m)
        lsum = lsum + p
        if b < n_pages:
            acc = acc + _nt_dot(p.astype(BF16), v_buf[slot, b].astype(BF16))
        else:
            acc = acc + _dot(p.astype(BF16), pad_new(vn_ref[...]).astype(BF16))
    out = jnp.where(same_head, acc / jnp.sum(lsum, axis=1, keepdims=True), 0.0)
    o_ref[...] = _fold_tiles(out)


def _attn_sample(cfg, layer, page_table, scores, thr_rep, lim_rep, q, k_new, v_new,
                 s_far, s_last, s_new, cache_k, cache_v):
    b = cfg.dec_batch
    t_rows = cfg.dec_seq
    n_pages = cfg.past_len // PAGE_SIZE
    rows = N_HEADS * t_rows
    per_b = lambda r, w: pl.BlockSpec((None, r, w), lambda i, pt: (i, 0, 0))
    const = lambda r, w: pl.BlockSpec((r, w), lambda i, pt: (0, 0))
    in_hbm = pl.BlockSpec(memory_space=pl.ANY)
    in_specs = [per_b(t_rows, scores.shape[2]), per_b(t_rows, LANES), per_b(t_rows, LANES),
                per_b(t_rows, D_ATT), per_b(t_rows, D_ATT), per_b(t_rows, D_ATT),
                const(rows, LANES), const(rows, LANES), const(rows, LANES), in_hbm, in_hbm]
    return pl.pallas_call(
        functools.partial(_attn_sample_kernel, layer, b, n_pages),
        grid_spec=pltpu.PrefetchScalarGridSpec(
            num_scalar_prefetch=1, grid=(b,), in_specs=in_specs, out_specs=per_b(t_rows, D_ATT),
            scratch_shapes=(_page_scratch(ATTN_PAGE_SLOTS, n_pages, D_ATT)
                            + _page_scratch(ATTN_PAGE_SLOTS, n_pages, D_ATT))),
        out_shape=jax.ShapeDtypeStruct((b, t_rows, D_ATT), F32),
        compiler_params=pltpu.CompilerParams(vmem_limit_bytes=VMEM_LIMIT),
        name="attn_sample",
    )(page_table, scores, thr_rep, lim_rep, q, k_new, v_new, s_far, s_last, s_new, cache_k, cache_v)


def _forward(cfg, x_prompt, x_sample, cache_k, cache_v, cache_kidx, state_pool, page_table, c_prompt,
             c_sample, rel_bias, ada_w, ada_b, ln1_pre, ln1_post, ln2_pre, ln2_post, w_in, pool_w,
             pool_scale, w_out, w_ff1, w_ff2):
    depth = ada_w.shape[0]
    bp, seq, bs, ts = cfg.batch, cfg.seq, cfg.dec_batch, cfg.dec_seq
    assert ts == SUBLANES and seq % cfg.blk == 0 and cfg.past_len % PAGE_SIZE == 0
    assert cfg.blk >= PAGE_SIZE and seq >= TOPK_MAX and seq // BF16_ROWS <= 256
    k_prompt = min(TOPK_MAX, seq // 4)
    k_sample = min(TOPK_MAX, (cfg.past_len + ts) // 4)
    rp, rs = bp * seq, bs * ts
    tm_s = min(IN_SUB_ROWS, rs)

    mod = _ada_mod(jnp.concatenate([c_prompt, c_sample], axis=0), ada_w, ada_b)
    tb, s_far, s_last, s_new = _bias_tables(rel_bias, cfg.blk)

    n_phys = cache_k.shape[1]
    cache_kt = cache_k.transpose(0, 1, 3, 4, 2).reshape(depth, n_phys, D_ATT, PAGE_SIZE)
    cache_vt = cache_v.transpose(0, 1, 3, 4, 2).reshape(depth, n_phys, D_ATT, PAGE_SIZE)
    cache_kit = cache_kidx.transpose(0, 1, 3, 2)

    xp = x_prompt.reshape(rp, D_MODEL)
    xs = x_sample.reshape(rs, D_MODEL)
    state_p = ()
    pool_p = []
    outs_s = [[], [], [], []]
    w_in_b = jnp.pad(w_in, ((0, 0), (0, 0), (0, D_IN_PAD - D_IN))).astype(BF16)
    wo_b, w1_b, w2_b = w_out.astype(BF16), w_ff1.astype(BF16), w_ff2.astype(BF16)
    for l in range(depth):
        lnrow = lambda a: a[l].reshape(1, D_MODEL)
        mod_p = mod[l, :bp].reshape(bp, 1, 6 * D_MODEL)
        mod_s = jnp.repeat(mod[l, bp:], ts, axis=0)
        psc = pool_scale[l].reshape(1, C_POOL)

        tm_in = min(seq, cfg.tm if state_p else max(IN_SUB_ROWS, cfg.tm // depth))
        u, kb, tailb, *state_p, wit, qt, qit, vbt = _in_proj(
            xp, mod_p, seq, lnrow(ln1_pre), w_in_b, tm_in, layer=l, depth=depth, carried=tuple(state_p))
        u3 = u.reshape(bp, seq, C_POOL)
        y_pool = _pool_mix(u3, None, 0, pool_w[l], psc, 1, min(256, seq)).reshape(rp, C_POOL)
        y_att = _attn_prompt(cfg, k_prompt, qt, qit, wit, kb, tailb, vbt, tb)
        xp = _out_ffn(xp, y_pool, y_att, mod_p, seq, lnrow(ln1_post), lnrow(ln2_pre), lnrow(ln2_post),
                      wo_b, w1_b, w2_b, l, min(cfg.tm_out, seq))
        pool_p.append(u3[:, seq - POOL_HIST:])

        u, q, qi, tail, k, v = _in_proj(xs, mod_s, 0, lnrow(ln1_pre), w_in_b, tm_s, layer=l)
        u3 = u.reshape(bs, ts, C_POOL)
        hist = jnp.concatenate([jnp.zeros((bs, POOL_PAD - POOL_HIST, C_POOL), F32), state_pool[l]], axis=1)
        y_pool = _pool_mix(u3, hist, cfg.past_len, pool_w[l], psc, min(16, bs), ts).reshape(rs, C_POOL)
        qi_ht = qi.reshape(bs, ts, IDX_HEADS, IDX_DIM).transpose(0, 2, 1, 3).reshape(bs, IDX_HEADS * ts, IDX_DIM)
        wi = tail[:, OFF_WI - OFF_KI:OFF_WI - OFF_KI + IDX_HEADS].reshape(bs, ts, IDX_HEADS) * IDX_HEADS ** -0.5
        wrep = jnp.broadcast_to(wi.transpose(0, 2, 1).reshape(bs, IDX_HEADS * ts, 1), (bs, IDX_HEADS * ts, LANES))
        scores = _score_sample(cfg, l, page_table, qi_ht, wrep, tail.reshape(bs, ts, TAIL), cache_kit)
        thr, lim = _threshold_cols(k_sample, scores, min(256, rs))
        per_query = lambda a: jnp.broadcast_to(a[0].reshape(bs, ts, 1), (bs, ts, LANES))
        y_att = _attn_sample(cfg, l, page_table, scores, per_query(thr), per_query(lim),
                             q.astype(F32).reshape(bs, ts, D_ATT),
                             k.reshape(bs, ts, D_ATT), v.reshape(bs, ts, D_ATT),
                             s_far, s_last, s_new, cache_kt, cache_vt).reshape(rs, D_ATT)
        xs = _out_ffn(xs, y_pool, y_att, mod_s, 0, lnrow(ln1_post), lnrow(ln2_pre), lnrow(ln2_post),
                      wo_b, w1_b, w2_b, l, min(OUT_SUB_ROWS, rs))
        for lst, a in zip(outs_s, (k.reshape(bs, ts, N_HEADS, HEAD_DIM), v.reshape(bs, ts, N_HEADS, HEAD_DIM),
                                   tail[:, :IDX_DIM].reshape(bs, ts, IDX_DIM),
                                   jnp.concatenate([hist, u3], axis=1)[:, -POOL_HIST:])):
            lst.append(a)

    kt, vt, kit = state_p
    heads_last = lambda a: a.reshape(depth, bp, N_HEADS, HEAD_DIM, seq).transpose(0, 1, 4, 2, 3)
    return (xp.reshape(bp, seq, D_MODEL), xs.reshape(bs, ts, D_MODEL),
            heads_last(kt), heads_last(vt), kit.transpose(0, 1, 3, 2), jnp.stack(pool_p),
            *[jnp.stack(a) for a in outs_s])


def kernel(x_prompt, x_sample, cache_k, cache_v, cache_kidx, state_pool, page_table, c_prompt, c_sample,
           rel_bias, ada_w, ada_b, ln1_pre, ln1_post, ln2_pre, ln2_post, w_in, pool_w, pool_scale,
           w_out, w_ff1, w_ff2):
    cfg = Cfg(batch=x_prompt.shape[0], seq=x_prompt.shape[1], dec_batch=x_sample.shape[0],
              dec_seq=x_sample.shape[1], past_len=page_table.shape[1] * PAGE_SIZE, blk=256, tm=1024,
              tm_out=1024)
    return _forward(cfg, x_prompt, x_sample, cache_k, cache_v, cache_kidx, state_pool, page_table, c_prompt,
                    c_sample, rel_bias, ada_w, ada_b, ln1_pre, ln1_post, ln2_pre, ln2_post, w_in, pool_w,
                    pool_scale, w_out, w_ff1, w_ff2)
```
